```python
import math
import jax, jax.numpy as jnp
from jax import lax
import numpy as np

D_MODEL = 1024
BATCH = 16
SEQ = 2048
DEPTH = 4

CHUNK = 64
Q_BLOCK = 128
NORM_EPS = 1e-6
M_HEADS = 4
M_HEAD_DIM = 128
M_WIDTH = M_HEADS * M_HEAD_DIM
CONV_WIDTH = 4
A_HEADS = 8
Q_LORA = 256
KV_LORA = 128
QK_NOPE = 64
QK_ROPE = 32
QK_HEAD = QK_NOPE + QK_ROPE
V_HEAD = 64
A_WIDTH = A_HEADS * V_HEAD
ROPE_THETA = 10000.0
S5_GROUP = 16
S5_GROUPS = D_MODEL // S5_GROUP
S5_STATE = 64
D_FF = 2816
N_EXPERTS = 8
TOP_K = 2
D_FF_EXPERT = 3584
IN_SIZES = (M_WIDTH, M_WIDTH, M_WIDTH, M_HEADS, M_HEADS, M_WIDTH, Q_LORA, KV_LORA, QK_ROPE)
IN_WIDTH = sum(IN_SIZES)
F32 = jnp.float32

kernel_name = 'hybrid_mlstm_mla_s5_moe_trunk'


def rms_norm(x, g):
    xf = x.astype(F32)
    y = xf * lax.rsqrt(jnp.mean(xf * xf, axis=-1, keepdims=True) + NORM_EPS)
    return (y * g.astype(F32)).astype(x.dtype)


def swiglu(t, w1, w3, w2):
    return (jax.nn.silu(t @ w1) * (t @ w3)) @ w2


def causal_depthwise_conv(x, w, b):
    y = lax.conv_general_dilated(
        x, w[:, None, :].astype(x.dtype), window_strides=(1,),
        padding=((CONV_WIDTH - 1, 0),), dimension_numbers=('NWC', 'WIO', 'NWC'),
        feature_group_count=x.shape[-1])
    return y + b.astype(x.dtype)


def rope_tables(positions):
    inv_freq = ROPE_THETA ** (-jnp.arange(0, QK_ROPE, 2, dtype=F32) / QK_ROPE)
    ang = positions.astype(F32)[..., None] * inv_freq
    return jnp.cos(ang), jnp.sin(ang)


def apply_rope(x, cos, sin):
    x1, x2 = jnp.split(x.astype(F32), 2, axis=-1)
    c = cos[:, :, None, :]
    s = sin[:, :, None, :]
    return jnp.concatenate([x1 * c - x2 * s, x2 * c + x1 * s], axis=-1)


def mlstm_chunk_step(carry, xs):
    c_mem, n_mem, m_prev = carry
    q, k, v, ig, lf = xs
    tril = jnp.asarray(np.tril(np.ones((CHUNK, CHUNK), dtype=bool)))
    b = jnp.cumsum(lf, axis=-1)
    d_log = jnp.where(tril, b[..., :, None] - b[..., None, :] + ig[..., None, :], -jnp.inf)
    inter = b + m_prev[..., None]
    m_t = jnp.maximum(inter, jnp.max(d_log, axis=-1))
    w_inter = jnp.exp(inter - m_t)
    s = jnp.einsum('bhld,bhsd->bhls', q, k) * jnp.exp(d_log - m_t[..., None])
    num = w_inter[..., None] * jnp.einsum('bhld,bhde->bhle', q, c_mem) + jnp.einsum('bhls,bhse->bhle', s, v)
    den = w_inter * jnp.einsum('bhld,bhd->bhl', q, n_mem) + jnp.sum(s, axis=-1)
    h = num / jnp.maximum(jnp.abs(den), jnp.exp(-m_t))[..., None]
    total = b[..., -1]
    log_in = total[..., None] - b + ig
    m_new = jnp.maximum(total + m_prev, jnp.max(log_in, axis=-1))
    a_prev = jnp.exp(total + m_prev - m_new)
    w_in = jnp.exp(log_in - m_new[..., None])
    c_new = a_prev[..., None, None] * c_mem + jnp.einsum('bhl,bhld,bhle->bhde', w_in, k, v)
    n_new = a_prev[..., None] * n_mem + jnp.einsum('bhl,bhld->bhd', w_in, k)
    return (c_new, n_new, m_new), h


def mlstm_mixer(q_raw, k_raw, v, ig_raw, fg_raw, og_raw, conv_w, conv_b, b_igate, b_fgate, mh_norm_g):
    bsz, seq, _ = q_raw.shape
    n_chunks = seq // CHUNK
    qk = jax.nn.silu(causal_depthwise_conv(jnp.concatenate([q_raw, k_raw], axis=-1), conv_w, conv_b))
    q, k = jnp.split(qk, 2, axis=-1)

    def to_chunks(t):
        t = t.astype(F32).reshape(bsz, n_chunks, CHUNK, M_HEADS, M_HEAD_DIM)
        return t.transpose(1, 0, 3, 2, 4)

    def gate_chunks(t):
        return t.reshape(bsz, n_chunks, CHUNK, M_HEADS).transpose(1, 0, 3, 2)

    qc = to_chunks(q) * (M_HEAD_DIM ** -0.5)
    kc = to_chunks(k)
    vc = to_chunks(v)
    igc = gate_chunks(ig_raw.astype(F32) + b_igate.astype(F32))
    lfc = gate_chunks(jax.nn.log_sigmoid(fg_raw.astype(F32) + b_fgate.astype(F32)))
    init = (jnp.zeros((bsz, M_HEADS, M_HEAD_DIM, M_HEAD_DIM), F32),
            jnp.zeros((bsz, M_HEADS, M_HEAD_DIM), F32),
            jnp.zeros((bsz, M_HEADS), F32))
    _, h = lax.scan(mlstm_chunk_step, init, (qc, kc, vc, igc, lfc))
    h = h.transpose(1, 0, 3, 2, 4).reshape(bsz, seq, M_HEADS, M_HEAD_DIM)
    h = rms_norm(h, mh_norm_g.reshape(M_HEADS, M_HEAD_DIM))
    out = jax.nn.sigmoid(og_raw.astype(F32)) * h.reshape(bsz, seq, M_WIDTH)
    return out.astype(q_raw.dtype)


def mla_mixer(cq, ckv, kr, cos, sin, cq_norm_g, w_uq, ckv_norm_g, w_ukv, qn_g, kn_g):
    bsz, seq, _ = cq.shape
    q = (rms_norm(cq, cq_norm_g) @ w_uq).reshape(bsz, seq, A_HEADS, QK_HEAD)
    kv = (rms_norm(ckv, ckv_norm_g) @ w_ukv).reshape(bsz, seq, A_HEADS, QK_NOPE + V_HEAD)
    k_nope, v = jnp.split(kv, [QK_NOPE], axis=-1)
    k_rope = jnp.broadcast_to(kr[:, :, None, :], (bsz, seq, A_HEADS, QK_ROPE))
    k = jnp.concatenate([k_nope, k_rope.astype(k_nope.dtype)], axis=-1)
    q = rms_norm(q, qn_g)
    k = rms_norm(k, kn_g)
    q = jnp.concatenate([q[..., :QK_NOPE].astype(F32), apply_rope(q[..., QK_NOPE:], cos, sin)], axis=-1)
    k = jnp.concatenate([k[..., :QK_NOPE].astype(F32), apply_rope(k[..., QK_NOPE:], cos, sin)], axis=-1)
    q = q * (QK_HEAD ** -0.5)
    chunk_id = np.arange(seq) // CHUNK
    outs = []
    for start in range(0, seq, Q_BLOCK):
        end = start + Q_BLOCK
        scores = jnp.einsum('bqhd,bkhd->bhqk', q[:, start:end], k[:, :end])
        mask = jnp.asarray(chunk_id[None, :end] <= chunk_id[start:end, None])
        probs = jax.nn.softmax(jnp.where(mask, scores, -jnp.inf), axis=-1)
        outs.append(jnp.einsum('bhqk,bkhd->bqhd', probs.astype(v.dtype), v[:, :end]))
    return jnp.concatenate(outs, axis=1).reshape(bsz, seq, A_WIDTH)


def s5_mixer(u, lam_re, lam_im, log_dt, b_re, b_im, c_re, c_im, d_skip):
    bsz, seq, _ = u.shape
    n_chunks = seq // CHUNK
    lam = lax.complex(jnp.minimum(lam_re.astype(F32), -1e-4), lam_im.astype(F32))
    dt = jnp.exp(log_dt.astype(F32))[:, None]
    a_bar = jnp.exp(lam * dt)
    b_bar = ((a_bar - 1.0) / lam)[..., None] * lax.complex(b_re.astype(F32), b_im.astype(F32))
    c_mat = lax.complex(c_re.astype(F32), c_im.astype(F32))
    a_seq = jnp.broadcast_to(a_bar, (bsz, CHUNK, S5_GROUPS, S5_STATE))
    uc = u.astype(F32).reshape(bsz, n_chunks, CHUNK, S5_GROUPS, S5_GROUP).transpose(1, 0, 2, 3, 4)

    def combine(e1, e2):
        a1, b1 = e1
        a2, b2 = e2
        return a1 * a2, a2 * b1 + b2

    def step(state, u_chunk):
        bu = jnp.einsum('blgh,gph->blgp', u_chunk, b_bar)
        bu = bu.at[:, 0].add(a_bar * state)
        _, xs = lax.associative_scan(combine, (a_seq, bu), axis=1)
        y = jnp.einsum('blgp,ghp->blgh', xs, c_mat).real
        return xs[:, -1], y

    init = jnp.zeros((bsz, S5_GROUPS, S5_STATE), jnp.complex64)
    _, y = lax.scan(step, init, uc)
    y = y.transpose(1, 0, 2, 3, 4).reshape(bsz, seq, D_MODEL)
    return y + d_skip.astype(F32) * u.astype(F32)


def moe_swiglu(h, router_w, w1, w3, w2):
    bsz, seq, dm = h.shape
    t = h.reshape(-1, dm)
    logits = (t @ router_w).astype(F32)
    top_v, top_i = lax.top_k(logits, TOP_K)
    probs = jax.nn.softmax(top_v, axis=-1)
    gates = jnp.sum(jax.nn.one_hot(top_i, N_EXPERTS, dtype=F32) * probs[..., None], axis=1)
    out = jnp.zeros(t.shape, F32)
    for e in range(N_EXPERTS):
        out = out + gates[:, e:e + 1] * swiglu(t, w1[e], w3[e], w2[e]).astype(F32)
    return out.reshape(bsz, seq, dm).astype(h.dtype)


def even_layer(x, cos, sin, norm1_g, w_in, conv_w, conv_b, b_igate, b_fgate, mh_norm_g, cq_norm_g, w_uq,
               ckv_norm_g, w_ukv, qn_g, kn_g, w_o, norm2_g, ffn_w1, ffn_w3, ffn_w2):
    h = rms_norm(x, norm1_g)
    offsets = np.cumsum(IN_SIZES)[:-1].tolist()
    q_m, k_m, v_m, ig, fg, og, cq, ckv, kr = jnp.split(h @ w_in, offsets, axis=-1)
    m_out = mlstm_mixer(q_m, k_m, v_m, ig, fg, og, conv_w, conv_b, b_igate, b_fgate, mh_norm_g)
    a_out = mla_mixer(cq, ckv, kr, cos, sin, cq_norm_g, w_uq, ckv_norm_g, w_ukv, qn_g, kn_g)
    mixed = jnp.concatenate([m_out.astype(x.dtype), a_out.astype(x.dtype)], axis=-1) @ w_o
    x = x + mixed.astype(x.dtype)
    return x + swiglu(rms_norm(x, norm2_g), ffn_w1, ffn_w3, ffn_w2).astype(x.dtype)


def odd_layer(x, norm1_g, lam_re, lam_im, log_dt, b_re, b_im, c_re, c_im, d_skip, glu_wv, glu_wg,
              norm2_g, router_w, moe_w1, moe_w3, moe_w2):
    y = jax.nn.gelu(s5_mixer(rms_norm(x, norm1_g), lam_re, lam_im, log_dt, b_re, b_im, c_re, c_im, d_skip))
    mixed = (y @ glu_wv.astype(F32)) * jax.nn.sigmoid(y @ glu_wg.astype(F32))
    x = x + mixed.astype(x.dtype)
    return x + moe_swiglu(rms_norm(x, norm2_g), router_w, moe_w1, moe_w3, moe_w2)


def _normal(k, shape, scale):
    return scale * jax.random.normal(k, shape, F32)


def _gain(k, d):
    return 1.0 + 0.02 * jax.random.normal(k, (d,), F32)


def _even_params(key):
    ks = jax.random.split(key, 18)
    return [
        ('norm1_g', _gain(ks[0], D_MODEL)),
        ('w_in', _normal(ks[1], (D_MODEL, IN_WIDTH), D_MODEL ** -0.5)),
        ('conv_w', _normal(ks[2], (CONV_WIDTH, 2 * M_WIDTH), CONV_WIDTH ** -0.5)),
        ('conv_b', _normal(ks[3], (2 * M_WIDTH,), 0.01)),
        ('b_igate', _normal(ks[4], (M_HEADS,), 0.1)),
        ('b_fgate', jnp.linspace(3.0, 6.0, M_HEADS, dtype=F32) + _normal(ks[5], (M_HEADS,), 0.1)),
        ('mh_norm_g', _gain(ks[6], M_WIDTH)),
        ('cq_norm_g', _gain(ks[7], Q_LORA)),
        ('w_uq', _normal(ks[8], (Q_LORA, A_HEADS * QK_HEAD), Q_LORA ** -0.5)),
        ('ckv_norm_g', _gain(ks[9], KV_LORA)),
        ('w_ukv', _normal(ks[10], (KV_LORA, A_HEADS * (QK_NOPE + V_HEAD)), KV_LORA ** -0.5)),
        ('qn_g', _gain(ks[11], QK_HEAD)),
        ('kn_g', _gain(ks[12], QK_HEAD)),
        ('w_o', _normal(ks[13], (M_WIDTH + A_WIDTH, D_MODEL), (M_WIDTH + A_WIDTH) ** -0.5)),
        ('norm2_g', _gain(ks[14], D_MODEL)),
        ('ffn_w1', _normal(ks[15], (D_MODEL, D_FF), D_MODEL ** -0.5)),
        ('ffn_w3', _normal(ks[16], (D_MODEL, D_FF), D_MODEL ** -0.5)),
        ('ffn_w2', _normal(ks[17], (D_FF, D_MODEL), D_FF ** -0.5)),
    ]


def _odd_params(key):
    ks = jax.random.split(key, 16)
    gp = (S5_GROUPS, S5_STATE)
    return [
        ('norm1_g', _gain(ks[0], D_MODEL)),
        ('lam_re', -0.5 + _normal(ks[1], gp, 0.01)),
        ('lam_im', math.pi * jnp.arange(S5_STATE, dtype=F32)[None, :] + _normal(ks[2], gp, 0.01)),
        ('log_dt', jax.random.uniform(ks[3], (S5_GROUPS,), F32, minval=math.log(1e-3), maxval=math.log(1e-1))),
        ('b_re', _normal(ks[4], (S5_GROUPS, S5_STATE, S5_GROUP), (2 * S5_GROUP) ** -0.5)),
        ('b_im', _normal(ks[5], (S5_GROUPS, S5_STATE, S5_GROUP), (2 * S5_GROUP) ** -0.5)),
        ('c_re', _normal(ks[6], (S5_GROUPS, S5_GROUP, S5_STATE), S5_STATE ** -0.5)),
        ('c_im', _normal(ks[7], (S5_GROUPS, S5_GROUP, S5_STATE), S5_STATE ** -0.5)),
        ('d_skip', _normal(ks[8], (D_MODEL,), 1.0)),
        ('glu_wv', _normal(ks[9], (D_MODEL, D_MODEL), D_MODEL ** -0.5)),
        ('glu_wg', _normal(ks[10], (D_MODEL, D_MODEL), D_MODEL ** -0.5)),
        ('norm2_g', _gain(ks[11], D_MODEL)),
        ('router_w', _normal(ks[12], (D_MODEL, N_EXPERTS), D_MODEL ** -0.5)),
        ('moe_w1', _normal(ks[13], (N_EXPERTS, D_MODEL, D_FF_EXPERT), D_MODEL ** -0.5)),
        ('moe_w3', _normal(ks[14], (N_EXPERTS, D_MODEL, D_FF_EXPERT), D_MODEL ** -0.5)),
        ('moe_w2', _normal(ks[15], (N_EXPERTS, D_FF_EXPERT, D_MODEL), D_FF_EXPERT ** -0.5)),
    ]


def setup_inputs(seed: int = 0) -> dict:
    key = jax.random.key(seed)
    k_x, k_pos, k_layers = jax.random.split(key, 3)
    layer_keys = jax.random.split(k_layers, DEPTH)
    offset = jax.random.randint(k_pos, (BATCH, 1), 0, 4096, dtype=jnp.int32)
    inputs = {
        'x': jax.random.normal(k_x, (BATCH, SEQ, D_MODEL), F32),
        'positions': offset + jnp.arange(SEQ, dtype=jnp.int32)[None, :],
    }
    for i in range(DEPTH):
        params = _even_params(layer_keys[i]) if i % 2 == 0 else _odd_params(layer_keys[i])
        for name, arr in params:
            inputs['l%d_%s' % (i, name)] = arr
    return inputs


def reference(x, positions,
              l0_norm1_g, l0_w_in, l0_conv_w, l0_conv_b, l0_b_igate, l0_b_fgate, l0_mh_norm_g, l0_cq_norm_g,
              l0_w_uq, l0_ckv_norm_g, l0_w_ukv, l0_qn_g, l0_kn_g, l0_w_o, l0_norm2_g, l0_ffn_w1, l0_ffn_w3, l0_ffn_w2,
              l1_norm1_g, l1_lam_re, l1_lam_im, l1_log_dt, l1_b_re, l1_b_im, l1_c_re, l1_c_im, l1_d_skip,
              l1_glu_wv, l1_glu_wg, l1_norm2_g, l1_router_w, l1_moe_w1, l1_moe_w3, l1_moe_w2,
              l2_norm1_g, l2_w_in, l2_conv_w, l2_conv_b, l2_b_igate, l2_b_fgate, l2_mh_norm_g, l2_cq_norm_g,
              l2_w_uq, l2_ckv_norm_g, l2_w_ukv, l2_qn_g, l2_kn_g, l2_w_o, l2_norm2_g, l2_ffn_w1, l2_ffn_w3, l2_ffn_w2,
              l3_norm1_g, l3_lam_re, l3_lam_im, l3_log_dt, l3_b_re, l3_b_im, l3_c_re, l3_c_im, l3_d_skip,
              l3_glu_wv, l3_glu_wg, l3_norm2_g, l3_router_w, l3_moe_w1, l3_moe_w3, l3_moe_w2):
    cos, sin = rope_tables(positions)
    layer_params = (
        (l0_norm1_g, l0_w_in, l0_conv_w, l0_conv_b, l0_b_igate, l0_b_fgate, l0_mh_norm_g, l0_cq_norm_g,
         l0_w_uq, l0_ckv_norm_g, l0_w_ukv, l0_qn_g, l0_kn_g, l0_w_o, l0_norm2_g, l0_ffn_w1, l0_ffn_w3, l0_ffn_w2),
        (l1_norm1_g, l1_lam_re, l1_lam_im, l1_log_dt, l1_b_re, l1_b_im, l1_c_re, l1_c_im, l1_d_skip,
         l1_glu_wv, l1_glu_wg, l1_norm2_g, l1_router_w, l1_moe_w1, l1_moe_w3, l1_moe_w2),
        (l2_norm1_g, l2_w_in, l2_conv_w, l2_conv_b, l2_b_igate, l2_b_fgate, l2_mh_norm_g, l2_cq_norm_g,
         l2_w_uq, l2_ckv_norm_g, l2_w_ukv, l2_qn_g, l2_kn_g, l2_w_o, l2_norm2_g, l2_ffn_w1, l2_ffn_w3, l2_ffn_w2),
        (l3_norm1_g, l3_lam_re, l3_lam_im, l3_log_dt, l3_b_re, l3_b_im, l3_c_re, l3_c_im, l3_d_skip,
         l3_glu_wv, l3_glu_wg, l3_norm2_g, l3_router_w, l3_moe_w1, l3_moe_w3, l3_moe_w2),
    )
    for i in range(DEPTH):
        if i % 2 == 0:
            x = even_layer(x, cos, sin, *layer_params[i])
        else:
            x = odd_layer(x, *layer_params[i])
    return x
```

```python
import functools
import math

import jax
import jax.numpy as jnp
import numpy as np
from jax import lax
from jax.experimental import pallas as pl
from jax.experimental.pallas import tpu as pltpu

F32 = jnp.float32
BF16 = jnp.bfloat16

D_MODEL = 1024
NORM_EPS = 1e-6
M_HEADS = 4
M_HEAD_DIM = 128
M_WIDTH = M_HEADS * M_HEAD_DIM
CONV_WIDTH = 4
A_HEADS = 8
Q_LORA = 256
KV_LORA = 128
QK_NOPE = 64
QK_ROPE = 32
QK_HEAD = QK_NOPE + QK_ROPE
V_HEAD = 64
A_WIDTH = A_HEADS * V_HEAD
ROPE_THETA = 10000.0
MLA_CHUNK = 64
S5_GROUP = 16
S5_GROUPS = D_MODEL // S5_GROUP
S5_STATE = 64
D_FF = 2816
N_EXPERTS = 8
D_FF_EXPERT = 3584

LANES = 128
HEAD_PAD = 128
Y_COLS = 4 * M_WIDTH + Q_LORA + KV_LORA
IG_LANE = 0
FG_LANE = 8
KR_LANE = QK_NOPE

VMEM_LIMIT = 56 * 1024 * 1024
NEG_BIG = -1e30


def _cparams(*sem):
    return pltpu.CompilerParams(dimension_semantics=sem, vmem_limit_bytes=VMEM_LIMIT)


def _rms(x, g):
    return x * lax.rsqrt(jnp.mean(x * x, axis=-1, keepdims=True) + NORM_EPS) * g


def _sigmoid(x):
    return 1.0 / (1.0 + jnp.exp(-x))


def _dot(a, b):
    return jnp.dot(a, b, preferred_element_type=F32)


def _dot_nt(a, b):
    return lax.dot_general(a, b, (((1,), (1,)), ((), ())), preferred_element_type=F32)


def _dot_tn(a, b):
    return lax.dot_general(a, b, (((0,), (0,)), ((), ())), preferred_element_type=F32)


def _const_spec(shape):
    nd = len(shape)
    return pl.BlockSpec(shape, lambda *_: (0,) * nd, pipeline_mode=pl.Buffered(1))


def _inproj_kernel(x_ref, g_ref, w_ref, y_ref, misc_ref):
    xn = _rms(x_ref[...], g_ref[...]).astype(BF16)
    y = _dot(xn, w_ref[...])
    y_ref[...] = y[:, :Y_COLS].astype(BF16)
    misc_ref[...] = y[:, Y_COLS:]


def _inproj(x, g, w_packed, tm):
    t = x.shape[0]
    return pl.pallas_call(
        _inproj_kernel,
        grid=(t // tm,),
        in_specs=[pl.BlockSpec((tm, D_MODEL), lambda i: (i, 0)),
                  _const_spec((1, D_MODEL)),
                  _const_spec((D_MODEL, Y_COLS + LANES))],
        out_specs=[pl.BlockSpec((tm, Y_COLS), lambda i: (i, 0)),
                   pl.BlockSpec((tm, LANES), lambda i: (i, 0))],
        out_shape=[jax.ShapeDtypeStruct((t, Y_COLS), BF16),
                   jax.ShapeDtypeStruct((t, LANES), F32)],
        compiler_params=_cparams("parallel"),
        name="inproj",
    )(x, g, w_packed)


def _cummax_lanes(r, width):
    lane = lax.broadcasted_iota(jnp.int32, r.shape, 1)
    s = 1
    while s < width:
        shifted = pltpu.roll(r, s, axis=1)
        r = jnp.where(lane >= s, jnp.maximum(r, shifted), r)
        s *= 2
    return r


def _mlstm_kernel(q_ref, k_ref, v_ref, og_ref, misc_ref, cwq_ref, cwk_ref, cbq_ref, cbk_ref, gb_ref, hg_ref,
                  out_ref, qpad, kpad, c_scr, n_scr, m_scr, *, lc):
    @pl.when(pl.program_id(1) == 0)
    def _():
        qpad[0:8, :] = jnp.zeros((8, M_WIDTH), F32)
        kpad[0:8, :] = jnp.zeros((8, M_WIDTH), F32)
        c_scr[...] = jnp.zeros_like(c_scr)
        n_scr[...] = jnp.zeros_like(n_scr)
        m_scr[...] = jnp.zeros_like(m_scr)

    qpad[8:8 + lc, :] = q_ref[...].astype(F32)
    kpad[8:8 + lc, :] = k_ref[...].astype(F32)

    def conv_silu(pad, cw_ref, cb_ref):
        acc = jnp.broadcast_to(cb_ref[...], (lc, M_WIDTH))
        for j in range(CONV_WIDTH):
            off = 8 - (CONV_WIDTH - 1) + j
            acc = acc + pad[off:off + lc, :] * cw_ref[j:j + 1, :]
        return acc * _sigmoid(acc)

    qc = conv_silu(qpad, cwq_ref, cbq_ref) * (M_HEAD_DIM ** -0.5)
    kc = conv_silu(kpad, cwk_ref, cbk_ref)
    qpad[0:8, :] = qpad[lc:lc + 8, :]
    kpad[0:8, :] = kpad[lc:lc + 8, :]

    gt = (misc_ref[...] + gb_ref[...]).T
    ig = gt[IG_LANE:IG_LANE + 8, :]
    fpre = gt[FG_LANE:FG_LANE + 8, :]
    lf = jnp.minimum(fpre, 0.0) - jnp.log(1.0 + jnp.exp(-jnp.abs(fpre)))
    hi = lf.astype(BF16).astype(F32)
    r1 = lf - hi
    mid = r1.astype(BF16).astype(F32)
    lo = r1 - mid
    stack = jnp.concatenate([hi, mid, lo, jnp.zeros_like(lo)], axis=0).astype(BF16)
    srow = lax.broadcasted_iota(jnp.int32, (lc, lc), 0)
    scol = lax.broadcasted_iota(jnp.int32, (lc, lc), 1)
    upper = (srow <= scol).astype(BF16)
    cs = _dot(stack, upper)
    b = cs[0:8] + cs[8:16] + cs[16:24]

    r = ig - b
    cm = _cummax_lanes(r, lc)
    mprev = m_scr[:, 0:1]
    mx = jnp.maximum(mprev, cm)
    w_inter = jnp.exp(mprev - mx)
    e_neg = jnp.exp(-(b + mx))
    total = b[:, lc - 1:lc]
    mnew = total + mx[:, lc - 1:lc]
    a_prev = jnp.exp(total + mprev - mnew)
    w_in = jnp.exp(total + r - mnew)
    m_scr[...] = jnp.broadcast_to(mnew, (8, LANES))

    cols = jnp.concatenate([-mx, w_inter, e_neg, w_in, jnp.zeros((LANES - 32, lc), F32)], axis=0).T
    tril = scol <= srow

    for h in range(M_HEADS):
        sl = slice(h * M_HEAD_DIM, (h + 1) * M_HEAD_DIM)
        qh = qc[:, sl]
        kh = kc[:, sl]
        vh = v_ref[:, sl]
        qb = qh.astype(BF16)
        u_col = cols[:, h:h + 1]
        wi_col = cols[:, 8 + h:9 + h]
        en_col = cols[:, 16 + h:17 + h]
        win_col = cols[:, 24 + h:25 + h]
        dmat = jnp.where(tril, u_col + r[h:h + 1, :], NEG_BIG)
        p = jnp.exp(dmat) * _dot_nt(qb, kh.astype(BF16))
        c_h = c_scr[h]
        n_h = n_scr[h:h + 1, :]
        num = wi_col * _dot(qb, c_h.astype(BF16)) + _dot(p.astype(BF16), vh)
        den = wi_col * jnp.sum(qh * n_h, axis=-1, keepdims=True) + jnp.sum(p, axis=-1, keepdims=True)
        ht = num / jnp.maximum(jnp.abs(den), en_col)
        kw = kh * win_col
        ah = a_prev[h:h + 1, :]
        c_scr[h] = ah * c_h + _dot_tn(kw.astype(BF16), vh)
        n_scr[h:h + 1, :] = ah * n_h + jnp.sum(kw, axis=0, keepdims=True)
        hn = _rms(ht, hg_ref[:, sl])
        out_ref[:, sl] = (_sigmoid(og_ref[:, sl].astype(F32)) * hn).astype(BF16)


def _mlstm(y, misc, cwq, cwk, cbq, cbk, gate_bias, hg, bsz, seq, lc):
    t = bsz * seq
    nch = seq // lc
    row = lambda b, c: b * nch + c
    return pl.pallas_call(
        functools.partial(_mlstm_kernel, lc=lc),
        grid=(bsz, nch),
        in_specs=[pl.BlockSpec((lc, M_WIDTH), lambda b, c: (row(b, c), 0)),
                  pl.BlockSpec((lc, M_WIDTH), lambda b, c: (row(b, c), 1)),
                  pl.BlockSpec((lc, M_WIDTH), lambda b, c: (row(b, c), 2)),
                  pl.BlockSpec((lc, M_WIDTH), lambda b, c: (row(b, c), 3)),
                  pl.BlockSpec((lc, LANES), lambda b, c: (row(b, c), 0)),
                  _const_spec((CONV_WIDTH, M_WIDTH)), _const_spec((CONV_WIDTH, M_WIDTH)),
                  _const_spec((1, M_WIDTH)), _const_spec((1, M_WIDTH)),
                  _const_spec((1, LANES)), _const_spec((1, M_WIDTH))],
        out_specs=pl.BlockSpec((lc, M_WIDTH), lambda b, c: (row(b, c), 0)),
        out_shape=jax.ShapeDtypeStruct((t, M_WIDTH), BF16),
        scratch_shapes=[pltpu.VMEM((lc + 8, M_WIDTH), F32), pltpu.VMEM((lc + 8, M_WIDTH), F32),
                        pltpu.VMEM((M_HEADS, M_HEAD_DIM, M_HEAD_DIM), F32),
                        pltpu.VMEM((8, M_HEAD_DIM), F32), pltpu.VMEM((8, LANES), F32)],
        compiler_params=_cparams("parallel", "arbitrary"),
        name="mlstm",
    )(y, y, y, y, misc, cwq, cwk, cbq, cbk, gate_bias, hg)


def _mla_prep_kernel(cq_ref, ckv_ref, misc_ref, cos_ref, sa_ref, sb_ref, cqg_ref, wuq_ref, ckvg_ref, wuk_ref,
                     wuv_ref, qng_ref, kng_ref, q_ref, k_ref, v_ref):
    cqn = _rms(cq_ref[...].astype(F32), cqg_ref[...]).astype(BF16)
    qa = _dot(cqn, wuq_ref[...])
    ckvn = _rms(ckv_ref[...].astype(F32), ckvg_ref[...]).astype(BF16)
    ka = _dot(ckvn, wuk_ref[...])
    v_ref[...] = _dot(ckvn, wuv_ref[...]).astype(BF16)
    lane = lax.broadcasted_iota(jnp.int32, misc_ref.shape, 1)
    kr = jnp.where((lane >= KR_LANE) & (lane < KR_LANE + QK_ROPE), misc_ref[...], 0.0)
    cos = cos_ref[...]
    sa = sa_ref[...]
    sb = sb_ref[...]

    def norm_rope(xh, g):
        xh = xh * lax.rsqrt(jnp.sum(xh * xh, axis=-1, keepdims=True) * (1.0 / QK_HEAD) + NORM_EPS) * g
        half = QK_ROPE // 2
        return xh * cos + pltpu.roll(xh, HEAD_PAD - half, axis=1) * sa + pltpu.roll(xh, half, axis=1) * sb

    for h in range(A_HEADS):
        sl = slice(h * HEAD_PAD, (h + 1) * HEAD_PAD)
        q_ref[0, h] = (norm_rope(qa[:, sl], qng_ref[...]) * (QK_HEAD ** -0.5)).astype(BF16)
        k_ref[0, h] = norm_rope(ka[:, sl] + kr, kng_ref[...]).astype(BF16)


def _mla_prep(y, misc, cos_t, sin_a, sin_b, cqg, wuq, ckvg, wuk, wuv, qng, kng, bsz, seq, ts):
    t = bsz * seq
    nt = seq // ts
    row = lambda b, i: (b * nt + i, 0)
    cq_blk = (4 * M_WIDTH) // Q_LORA
    ckv_blk = (4 * M_WIDTH + Q_LORA) // KV_LORA
    hw = A_HEADS * HEAD_PAD
    return pl.pallas_call(
        _mla_prep_kernel,
        grid=(bsz, nt),
        in_specs=[pl.BlockSpec((ts, Q_LORA), lambda b, i: (b * nt + i, cq_blk)),
                  pl.BlockSpec((ts, KV_LORA), lambda b, i: (b * nt + i, ckv_blk)),
                  pl.BlockSpec((ts, LANES), row), pl.BlockSpec((ts, LANES), row),
                  pl.BlockSpec((ts, LANES), row), pl.BlockSpec((ts, LANES), row),
                  _const_spec((1, Q_LORA)), _const_spec((Q_LORA, hw)),
                  _const_spec((1, KV_LORA)), _const_spec((KV_LORA, hw)), _const_spec((KV_LORA, A_WIDTH)),
                  _const_spec((1, HEAD_PAD)), _const_spec((1, HEAD_PAD))],
        out_specs=[pl.BlockSpec((1, A_HEADS, ts, HEAD_PAD), lambda b, i: (b, 0, i, 0)),
                   pl.BlockSpec((1, A_HEADS, ts, HEAD_PAD), lambda b, i: (b, 0, i, 0)),
                   pl.BlockSpec((ts, A_WIDTH), row)],
        out_shape=[jax.ShapeDtypeStruct((bsz, A_HEADS, seq, HEAD_PAD), BF16),
                   jax.ShapeDtypeStruct((bsz, A_HEADS, seq, HEAD_PAD), BF16),
                   jax.ShapeDtypeStruct((t, A_WIDTH), BF16)],
        compiler_params=_cparams("parallel", "parallel"),
        name="mla_prep",
    )(y, y, misc, cos_t, sin_a, sin_b, cqg, wuq, ckvg, wuk, wuv, qng, kng)


def _attn_kernel(q_ref, k_ref, v_ref, o_ref, *, tq):
    i = pl.program_id(1)
    row = lax.broadcasted_iota(jnp.int32, (tq, tq), 0)
    col = lax.broadcasted_iota(jnp.int32, (tq, tq), 1)
    dmask = (col // MLA_CHUNK) <= (row // MLA_CHUNK)
    lane = lax.broadcasted_iota(jnp.int32, (tq, LANES), 1)
    for hp in range(A_HEADS // 2):
        vsl = slice(hp * LANES, (hp + 1) * LANES)
        outs = []
        for hh in range(2):
            h = 2 * hp + hh
            q = q_ref[0, h]

            def update(kt, carry, masked):
                m, l, acc = carry
                off = pl.multiple_of(kt * tq, tq)
                s = _dot_nt(q, k_ref[0, h, pl.ds(off, tq), :])
                if masked:
                    s = jnp.where(dmask, s, NEG_BIG)
                mn = jnp.maximum(m, jnp.max(s, axis=-1, keepdims=True))
                alpha = jnp.exp(m - mn)
                p = jnp.exp(s - mn)
                l = alpha * l + jnp.sum(p, axis=-1, keepdims=True)
                acc = alpha * acc + _dot(p.astype(BF16), v_ref[pl.ds(off, tq), vsl])
                return mn, l, acc

            init = (jnp.full((tq, 1), NEG_BIG, F32), jnp.zeros((tq, 1), F32), jnp.zeros((tq, LANES), F32))
            carry = lax.fori_loop(0, i, functools.partial(update, masked=False), init)
            m, l, acc = update(i, carry, True)
            outs.append(acc / l)
        o_ref[:, vsl] = jnp.where(lane < V_HEAD, outs[0], outs[1]).astype(BF16)


def _attention(q, k, v, bsz, seq, tq):
    nq = seq // tq
    return pl.pallas_call(
        functools.partial(_attn_kernel, tq=tq),
        grid=(bsz, nq),
        in_specs=[pl.BlockSpec((1, A_HEADS, tq, HEAD_PAD), lambda b, i: (b, 0, i, 0)),
                  pl.BlockSpec((1, A_HEADS, seq, HEAD_PAD), lambda b, i: (b, 0, 0, 0)),
                  pl.BlockSpec((seq, A_WIDTH), lambda b, i: (b, 0))],
        out_specs=pl.BlockSpec((tq, A_WIDTH), lambda b, i: (b * nq + i, 0)),
        out_shape=jax.ShapeDtypeStruct((bsz * seq, A_WIDTH), BF16),
        compiler_params=_cparams("parallel", "arbitrary"),
        name="mla_attention",
    )(q, k, v)


def _outproj_ffn_kernel(m_ref, a_ref, x_ref, wom_ref, woa_ref, g_ref, w1_ref, w3_ref, w2_ref, o_ref):
    x1 = x_ref[...] + _dot(m_ref[...], wom_ref[...]) + _dot(a_ref[...], woa_ref[...])
    hn = _rms(x1, g_ref[...]).astype(BF16)
    a = _dot(hn, w1_ref[...])
    b = _dot(hn, w3_ref[...])
    o_ref[...] = x1 + _dot((a * _sigmoid(a) * b).astype(BF16), w2_ref[...])


def _outproj_ffn(m_out, a_out, x, wom, woa, g, w1, w3, w2, tm):
    t = x.shape[0]
    return pl.pallas_call(
        _outproj_ffn_kernel,
        grid=(t // tm,),
        in_specs=[pl.BlockSpec((tm, M_WIDTH), lambda i: (i, 0)),
                  pl.BlockSpec((tm, A_WIDTH), lambda i: (i, 0)),
                  pl.BlockSpec((tm, D_MODEL), lambda i: (i, 0)),
                  _const_spec((M_WIDTH, D_MODEL)), _const_spec((A_WIDTH, D_MODEL)), _const_spec((1, D_MODEL)),
                  _const_spec((D_MODEL, D_FF)), _const_spec((D_MODEL, D_FF)), _const_spec((D_FF, D_MODEL))],
        out_specs=pl.BlockSpec((tm, D_MODEL), lambda i: (i, 0)),
        out_shape=jax.ShapeDtypeStruct((t, D_MODEL), F32),
        compiler_params=_cparams("parallel"),
        name="outproj_ffn",
    )(m_out, a_out, x, wom, woa, g, w1, w3, w2)


S5_CHUNK = 16
S5_ROW = S5_CHUNK * S5_GROUP


def _norm_kernel(x_ref, g_ref, o_ref):
    o_ref[...] = _rms(x_ref[...], g_ref[...]).astype(BF16)


def _norm(x, g, tm):
    t = x.shape[0]
    return pl.pallas_call(
        _norm_kernel,
        grid=(t // tm,),
        in_specs=[pl.BlockSpec((tm, D_MODEL), lambda i: (i, 0)), _const_spec((1, D_MODEL))],
        out_specs=pl.BlockSpec((tm, D_MODEL), lambda i: (i, 0)),
        out_shape=jax.ShapeDtypeStruct((t, D_MODEL), BF16),
        compiler_params=_cparams("parallel"),
        name="norm",
    )(x, g)


def _gelu_tanh(x):
    return 0.5 * x * (1.0 + jnp.tanh(math.sqrt(2.0 / math.pi) * (x + 0.044715 * (x * x * x))))


def _s5_kernel(u_ref, w_ref, cst_ref, a1_ref, a2_ref, dk_ref, z_ref, vv_scr, xp_scr, *, nch, bsz):
    two_p = 2 * S5_STATE
    u = u_ref[0]
    r = _dot(u, w_ref[0])
    vv_scr[...] = r[:, S5_ROW:]
    a1 = a1_ref[0]
    a2 = a2_ref[0]

    def body(c, carry):
        p, q = carry
        off = pl.multiple_of(c * bsz, bsz)
        xp_scr[pl.ds(off, bsz), :] = p
        vv = vv_scr[pl.ds(off, bsz), :]
        pn = a1 * p + a2 * q + vv[:, :two_p]
        qn = a1 * q - a2 * p + vv[:, two_p:]
        return pn, qn

    zero = jnp.zeros((bsz, two_p), F32)
    lax.fori_loop(0, nch, body, (zero, zero))
    y = r[:, :S5_ROW] + _dot(xp_scr[...].astype(BF16), cst_ref[0]) + dk_ref[0] * u.astype(F32)
    z_ref[0] = _gelu_tanh(y).astype(BF16)


def _s5(u_g, w_all, cst, a1, a2, dk, nch, bsz):
    g, n, _ = u_g.shape
    two_p = 2 * S5_STATE
    wcols = S5_ROW + 2 * two_p
    return pl.pallas_call(
        functools.partial(_s5_kernel, nch=nch, bsz=bsz),
        grid=(g,),
        in_specs=[pl.BlockSpec((1, n, S5_ROW), lambda i: (i, 0, 0)),
                  pl.BlockSpec((1, S5_ROW, wcols), lambda i: (i, 0, 0)),
                  pl.BlockSpec((1, two_p, S5_ROW), lambda i: (i, 0, 0)),
                  pl.BlockSpec((1, 1, two_p), lambda i: (i, 0, 0)),
                  pl.BlockSpec((1, 1, two_p), lambda i: (i, 0, 0)),
                  pl.BlockSpec((1, 1, S5_ROW), lambda i: (i, 0, 0))],
        out_specs=pl.BlockSpec((1, n, S5_ROW), lambda i: (i, 0, 0)),
        out_shape=jax.ShapeDtypeStruct((g, n, S5_ROW), BF16),
        scratch_shapes=[pltpu.VMEM((n, 2 * two_p), F32), pltpu.VMEM((n, two_p), F32)],
        compiler_params=_cparams("parallel"),
        name="s5",
    )(u_g, w_all, cst, a1, a2, dk)


def _s5_weights(lam_re, lam_im, log_dt, b_re, b_im, c_re, c_im, d_skip):
    lc, h, p = S5_CHUNK, S5_GROUP, S5_STATE
    lam = lax.complex(jnp.minimum(lam_re.astype(F32), -1e-4), lam_im.astype(F32))
    dt = jnp.exp(log_dt.astype(F32))[:, None]
    ldt = lam * dt
    a_bar = jnp.exp(ldt)
    b_bar = ((a_bar - 1.0) / lam)[..., None] * lax.complex(b_re.astype(F32), b_im.astype(F32))
    c_mat = lax.complex(c_re.astype(F32), c_im.astype(F32))
    steps = jnp.arange(lc + 1, dtype=F32)
    apow = jnp.exp(ldt[None] * steps[:, None, None].astype(ldt.dtype))
    g = lam.shape[0]
    ker = jnp.einsum('gip,kgp,gpj->gkij', c_mat, apow[:lc], b_bar).real
    lag = np.arange(lc)[None, :] - np.arange(lc)[:, None]
    toe = jnp.where((lag >= 0)[None, :, :, None, None], ker[:, np.clip(lag, 0, lc - 1)], 0.0)
    toe = toe.transpose(0, 1, 4, 2, 3).reshape(g, lc * h, lc * h)
    wst = apow[:lc][::-1].transpose(1, 0, 2)[:, :, None, :] * jnp.swapaxes(b_bar, 1, 2)[:, None, :, :]
    wst = wst.reshape(g, lc * h, p)
    w_all = jnp.concatenate([toe, wst.real, wst.imag, wst.imag, wst.real], axis=-1).astype(BF16)
    ca = c_mat[:, None, :, :] * apow[1:lc + 1].transpose(1, 0, 2)[:, :, None, :]
    ca = ca.reshape(g, lc * h, p)
    cst = jnp.concatenate([ca.real, -ca.imag], axis=-1).transpose(0, 2, 1).astype(BF16)
    al = apow[lc]
    a1 = jnp.concatenate([al.real, al.real], axis=-1)[:, None, :]
    a2 = jnp.concatenate([-al.imag, al.imag], axis=-1)[:, None, :]
    dk = jnp.tile(d_skip.astype(F32).reshape(g, 1, h), (1, lc, 1)).reshape(g, 1, lc * h)
    return w_all, cst, a1, a2, dk


def _glu_kernel(z_ref, x_ref, wv_ref, wg_ref, g_ref, x2_ref, xn_ref):
    z = z_ref[...]
    x2 = x_ref[...] + _dot(z, wv_ref[...]) * _sigmoid(_dot(z, wg_ref[...]))
    x2_ref[...] = x2
    xn_ref[...] = _rms(x2, g_ref[...]).astype(BF16)


def _glu(z, x, wv, wg, g, tm):
    t = x.shape[0]
    return pl.pallas_call(
        _glu_kernel,
        grid=(t // tm,),
        in_specs=[pl.BlockSpec((tm, D_MODEL), lambda i: (i, 0)), pl.BlockSpec((tm, D_MODEL), lambda i: (i, 0)),
                  _const_spec((D_MODEL, D_MODEL)), _const_spec((D_MODEL, D_MODEL)), _const_spec((1, D_MODEL))],
        out_specs=[pl.BlockSpec((tm, D_MODEL), lambda i: (i, 0)), pl.BlockSpec((tm, D_MODEL), lambda i: (i, 0))],
        out_shape=[jax.ShapeDtypeStruct((t, D_MODEL), F32), jax.ShapeDtypeStruct((t, D_MODEL), BF16)],
        compiler_params=_cparams("parallel"),
        name="glu",
    )(z, x, wv, wg, g)


def _router_kernel(xn_ref, wh_ref, wl_ref, pos_ref, gate_ref, posc_ref, cnt_ref, *, tb):
    xn = xn_ref[...]
    lg = (_dot_nt(wh_ref[...], xn) + _dot_nt(wl_ref[...], xn))[0:N_EXPERTS]
    eidx = lax.broadcasted_iota(jnp.int32, (N_EXPERTS, tb), 0)
    m1 = jnp.max(lg, axis=0, keepdims=True)
    i1 = jnp.min(jnp.where(lg == m1, eidx, N_EXPERTS), axis=0, keepdims=True)
    rest = jnp.where(eidx == i1, -jnp.inf, lg)
    m2 = jnp.max(rest, axis=0, keepdims=True)
    i2 = jnp.min(jnp.where(rest == m2, eidx, N_EXPERTS), axis=0, keepdims=True)
    e21 = jnp.exp(m2 - m1)
    p1 = 1.0 / (1.0 + e21)
    sel1 = eidx == i1
    sel2 = eidx == i2
    sel = sel1 | sel2
    gate_ref[...] = jnp.where(sel1, p1, jnp.where(sel2, e21 * p1, 0.0))
    onehot = jnp.concatenate([sel.astype(F32), jnp.zeros((N_EXPERTS, tb), F32)], axis=0).astype(BF16)
    srow = lax.broadcasted_iota(jnp.int32, (tb, tb), 0)
    scol = lax.broadcasted_iota(jnp.int32, (tb, tb), 1)
    rank = _dot(onehot, (srow < scol).astype(BF16))[0:N_EXPERTS]
    pos = jnp.where(sel, rank, -1.0)
    pos_ref[...] = pos
    posc_ref[...] = jnp.concatenate([pos, jnp.full((LANES - N_EXPERTS, tb), -1.0, F32)], axis=0).T
    cnt = jnp.sum(sel.astype(F32), axis=1, keepdims=True)
    cnt_ref[0] = jnp.broadcast_to(cnt, (N_EXPERTS, LANES))


def _router(xn, wr_hi, wr_lo, tb):
    t = xn.shape[0]
    nb = t // tb
    return pl.pallas_call(
        functools.partial(_router_kernel, tb=tb),
        grid=(nb,),
        in_specs=[pl.BlockSpec((tb, D_MODEL), lambda i: (i, 0)),
                  _const_spec((2 * N_EXPERTS, D_MODEL)), _const_spec((2 * N_EXPERTS, D_MODEL))],
        out_specs=[pl.BlockSpec((N_EXPERTS, tb), lambda i: (0, i)), pl.BlockSpec((N_EXPERTS, tb), lambda i: (0, i)),
                   pl.BlockSpec((tb, LANES), lambda i: (i, 0)),
                   pl.BlockSpec((1, N_EXPERTS, LANES), lambda i: (i, 0, 0))],
        out_shape=[jax.ShapeDtypeStruct((N_EXPERTS, t), F32), jax.ShapeDtypeStruct((N_EXPERTS, t), F32),
                   jax.ShapeDtypeStruct((t, LANES), F32), jax.ShapeDtypeStruct((nb, N_EXPERTS, LANES), F32)],
        compiler_params=_cparams("parallel"),
        name="router",
    )(xn, wr_hi, wr_lo)


MOE_ROW_TILE = 128
MOE_FF_SPLIT = 2


def _moe_kernel(cnt_ref, xn_ref, x_ref, pos_ref, gate_ref, posc_ref, w1_ref, w3_ref, w2_ref, o_ref, xe_scr, *, tb):
    b = pl.program_id(0)
    e = pl.program_id(1)
    f = pl.program_id(2)
    rt = MOE_ROW_TILE
    ntiles = (cnt_ref[b * N_EXPERTS + e] + rt - 1) // rt

    @pl.when((e == 0) & (f == 0))
    def _():
        o_ref[...] = x_ref[...]

    prow = pos_ref[pl.ds(e, 1), :]
    grow = gate_ref[pl.ds(e, 1), :]
    lane = lax.broadcasted_iota(jnp.int32, (tb, LANES), 1)
    pcol = jnp.sum(jnp.where(lane == e, posc_ref[...], 0.0), axis=-1, keepdims=True)
    slot_r = lax.broadcasted_iota(jnp.int32, (rt, tb), 0).astype(F32)
    slot_c = lax.broadcasted_iota(jnp.int32, (tb, rt), 1).astype(F32)

    @pl.when(f == 0)
    def _():
        def gather(r, carry):
            base = pl.multiple_of(r * rt, rt)
            onehot = (prow - base.astype(F32) == slot_r).astype(BF16)
            xe_scr[pl.ds(base, rt), :] = _dot(onehot, xn_ref[...]).astype(BF16)
            return carry
        lax.fori_loop(0, ntiles, gather, 0)

    def ffn(r, carry):
        base = pl.multiple_of(r * rt, rt)
        xe = xe_scr[pl.ds(base, rt), :]
        a = _dot(xe, w1_ref[0])
        g = (a * _sigmoid(a) * _dot(xe, w3_ref[0])).astype(BF16)
        y = _dot(g, w2_ref[0])
        hit = prow - base.astype(F32) == slot_r
        gs = jnp.sum(jnp.where(hit, grow, 0.0), axis=-1, keepdims=True)
        scatter = (pcol - base.astype(F32) == slot_c).astype(BF16)
        o_ref[...] += _dot(scatter, (y * gs).astype(BF16))
        return carry

    lax.fori_loop(0, ntiles, ffn, 0)


def _moe(cnt, xn, x, pos, gate, posc, w1, w3, w2, tb):
    t = x.shape[0]
    nb = t // tb
    fs = D_FF_EXPERT // MOE_FF_SPLIT
    grid_spec = pltpu.PrefetchScalarGridSpec(
        num_scalar_prefetch=1,
        grid=(nb, N_EXPERTS, MOE_FF_SPLIT),
        in_specs=[pl.BlockSpec((tb, D_MODEL), lambda b, e, f, c: (b, 0)),
                  pl.BlockSpec((tb, D_MODEL), lambda b, e, f, c: (b, 0)),
                  pl.BlockSpec((N_EXPERTS, tb), lambda b, e, f, c: (0, b)),
                  pl.BlockSpec((N_EXPERTS, tb), lambda b, e, f, c: (0, b)),
                  pl.BlockSpec((tb, LANES), lambda b, e, f, c: (b, 0)),
                  pl.BlockSpec((1, D_MODEL, fs), lambda b, e, f, c: (e, 0, f)),
                  pl.BlockSpec((1, D_MODEL, fs), lambda b, e, f, c: (e, 0, f)),
                  pl.BlockSpec((1, fs, D_MODEL), lambda b, e, f, c: (e, f, 0))],
        out_specs=pl.BlockSpec((tb, D_MODEL), lambda b, e, f, c: (b, 0)),
        scratch_shapes=[pltpu.VMEM((tb, D_MODEL), BF16)],
    )
    return pl.pallas_call(
        functools.partial(_moe_kernel, tb=tb),
        grid_spec=grid_spec,
        out_shape=jax.ShapeDtypeStruct((t, D_MODEL), F32),
        compiler_params=_cparams("parallel", "arbitrary", "arbitrary"),
        name="moe",
    )(cnt, xn, x, pos, gate, posc, w1, w3, w2)


def _rope_tables(positions):
    half = QK_ROPE // 2
    inv_freq = ROPE_THETA ** (-jnp.arange(0, QK_ROPE, 2, dtype=F32) / QK_ROPE)
    ang = positions.astype(F32).reshape(-1, 1) * inv_freq
    cos, sin = jnp.cos(ang), jnp.sin(ang)
    t = ang.shape[0]
    ones_lo = jnp.ones((t, QK_NOPE), F32)
    zeros_lo = jnp.zeros((t, QK_NOPE), F32)
    zeros_h = jnp.zeros((t, half), F32)
    tail1 = jnp.ones((t, HEAD_PAD - QK_HEAD), F32)
    tail0 = jnp.zeros((t, HEAD_PAD - QK_HEAD), F32)
    cos_t = jnp.concatenate([ones_lo, cos, cos, tail1], axis=-1)
    sin_a = jnp.concatenate([zeros_lo, -sin, zeros_h, tail0], axis=-1)
    sin_b = jnp.concatenate([zeros_lo, zeros_h, sin, tail0], axis=-1)
    return cos_t, sin_a, sin_b


def _pad_heads(w, per_head_in, keep, heads):
    k = w.shape[0]
    w = w.reshape(k, heads, per_head_in)[:, :, :keep]
    return jnp.pad(w, ((0, 0), (0, 0), (0, HEAD_PAD - keep))).reshape(k, heads * HEAD_PAD)


def _even_layer(x, rope, bsz, seq, norm1_g, w_in, conv_w, conv_b, b_igate, b_fgate, mh_norm_g, cq_norm_g, w_uq,
                ckv_norm_g, w_ukv, qn_g, kn_g, w_o, norm2_g, ffn_w1, ffn_w3, ffn_w2):
    row = lambda v: v.astype(F32).reshape(1, -1)
    mw = M_WIDTH
    o_ig, o_fg, o_og = 3 * mw, 3 * mw + M_HEADS, 3 * mw + 2 * M_HEADS
    o_cq = o_og + mw
    o_ckv, o_kr = o_cq + Q_LORA, o_cq + Q_LORA + KV_LORA
    misc_w = jnp.zeros((D_MODEL, LANES), F32)
    misc_w = misc_w.at[:, IG_LANE:IG_LANE + M_HEADS].set(w_in[:, o_ig:o_fg])
    misc_w = misc_w.at[:, FG_LANE:FG_LANE + M_HEADS].set(w_in[:, o_fg:o_og])
    misc_w = misc_w.at[:, KR_LANE:KR_LANE + QK_ROPE].set(w_in[:, o_kr:o_kr + QK_ROPE])
    w_packed = jnp.concatenate([w_in[:, :3 * mw], w_in[:, o_og:o_cq], w_in[:, o_cq:o_ckv], w_in[:, o_ckv:o_kr],
                                misc_w], axis=-1).astype(BF16)
    gate_bias = jnp.zeros((1, LANES), F32)
    gate_bias = gate_bias.at[0, IG_LANE:IG_LANE + M_HEADS].set(b_igate.astype(F32))
    gate_bias = gate_bias.at[0, FG_LANE:FG_LANE + M_HEADS].set(b_fgate.astype(F32))

    tm = min(512, x.shape[0])
    y, misc = _inproj(x, row(norm1_g), w_packed, tm)

    lc = min(256, seq)
    m_out = _mlstm(y, misc, conv_w[:, :mw].astype(F32), conv_w[:, mw:].astype(F32), row(conv_b[:mw]),
                   row(conv_b[mw:]), gate_bias, row(mh_norm_g), bsz, seq, lc)

    wuq = _pad_heads(w_uq, QK_HEAD, QK_HEAD, A_HEADS).astype(BF16)
    wuk = _pad_heads(w_ukv, QK_NOPE + V_HEAD, QK_NOPE, A_HEADS).astype(BF16)
    wuv = w_ukv.reshape(KV_LORA, A_HEADS, QK_NOPE + V_HEAD)[:, :, QK_NOPE:].reshape(KV_LORA, A_WIDTH).astype(BF16)
    pad_g = lambda g: jnp.pad(g.astype(F32), (0, HEAD_PAD - QK_HEAD)).reshape(1, HEAD_PAD)
    ts = min(512, seq)
    q, k, v = _mla_prep(y, misc, rope[0], rope[1], rope[2], row(cq_norm_g), wuq, row(ckv_norm_g), wuk, wuv,
                        pad_g(qn_g), pad_g(kn_g), bsz, seq, ts)
    a_out = _attention(q, k, v, bsz, seq, min(256, seq))

    return _outproj_ffn(m_out, a_out, x, w_o[:mw].astype(BF16), w_o[mw:].astype(BF16), row(norm2_g),
                        ffn_w1.astype(BF16), ffn_w3.astype(BF16), ffn_w2.astype(BF16), tm)


def _odd_layer(x, bsz, seq, norm1_g, lam_re, lam_im, log_dt, b_re, b_im, c_re, c_im, d_skip, glu_wv, glu_wg,
               norm2_g, router_w, moe_w1, moe_w3, moe_w2):
    row = lambda v: v.astype(F32).reshape(1, -1)
    t = x.shape[0]
    tm = min(512, t)
    u = _norm(x, row(norm1_g), tm)
    nch = seq // S5_CHUNK
    u_g = u.reshape(bsz, nch, S5_CHUNK, S5_GROUPS, S5_GROUP).transpose(3, 1, 0, 2, 4)
    u_g = u_g.reshape(S5_GROUPS, nch * bsz, S5_ROW)
    w_all, cst, a1, a2, dk = _s5_weights(lam_re, lam_im, log_dt, b_re, b_im, c_re, c_im, d_skip)
    z_g = _s5(u_g, w_all, cst, a1, a2, dk, nch, bsz)
    z = z_g.reshape(S5_GROUPS, nch, bsz, S5_CHUNK, S5_GROUP).transpose(2, 1, 3, 0, 4).reshape(t, D_MODEL)
    x2, xn = _glu(z, x, glu_wv.astype(BF16), glu_wg.astype(BF16), row(norm2_g), tm)

    tb = min(1024, t)
    wr = jnp.pad(router_w.astype(F32).T, ((0, N_EXPERTS), (0, 0)))
    wr_hi = wr.astype(BF16)
    wr_lo = (wr - wr_hi.astype(F32)).astype(BF16)
    pos, gate, posc, cnt = _router(xn, wr_hi, wr_lo, tb)
    cnt = cnt[:, :, 0].astype(jnp.int32).reshape(-1)
    return _moe(cnt, xn, x2, pos, gate, posc, moe_w1.astype(BF16), moe_w3.astype(BF16), moe_w2.astype(BF16), tb)


def kernel(x, positions, l0_norm1_g, l0_w_in, l0_conv_w, l0_conv_b, l0_b_igate, l0_b_fgate, l0_mh_norm_g, l0_cq_norm_g, l0_w_uq, l0_ckv_norm_g, l0_w_ukv, l0_qn_g, l0_kn_g, l0_w_o, l0_norm2_g, l0_ffn_w1, l0_ffn_w3, l0_ffn_w2, l1_norm1_g, l1_lam_re, l1_lam_im, l1_log_dt, l1_b_re, l1_b_im, l1_c_re, l1_c_im, l1_d_skip, l1_glu_wv, l1_glu_wg, l1_norm2_g, l1_router_w, l1_moe_w1, l1_moe_w3, l1_moe_w2, l2_norm1_g, l2_w_in, l2_conv_w, l2_conv_b, l2_b_igate, l2_b_fgate, l2_mh_norm_g, l2_cq_norm_g, l2_w_uq, l2_ckv_norm_g, l2_w_ukv, l2_qn_g, l2_kn_g, l2_w_o, l2_norm2_g, l2_ffn_w1, l2_ffn_w3, l2_ffn_w2, l3_norm1_g, l3_lam_re, l3_lam_im, l3_log_dt, l3_b_re, l3_b_im, l3_c_re, l3_c_im, l3_d_skip, l3_glu_wv, l3_glu_wg, l3_norm2_g, l3_router_w, l3_moe_w1, l3_moe_w3, l3_moe_w2):
    bsz, seq, _ = x.shape
    rope = _rope_tables(positions)
    h = x.reshape(bsz * seq, D_MODEL)
    h = _even_layer(h, rope, bsz, seq, l0_norm1_g, l0_w_in, l0_conv_w, l0_conv_b, l0_b_igate, l0_b_fgate,
                    l0_mh_norm_g, l0_cq_norm_g, l0_w_uq, l0_ckv_norm_g, l0_w_ukv, l0_qn_g, l0_kn_g, l0_w_o,
                    l0_norm2_g, l0_ffn_w1, l0_ffn_w3, l0_ffn_w2)
    h = _odd_layer(h, bsz, seq, l1_norm1_g, l1_lam_re, l1_lam_im, l1_log_dt, l1_b_re, l1_b_im, l1_c_re, l1_c_im,
                   l1_d_skip, l1_glu_wv, l1_glu_wg, l1_norm2_g, l1_router_w, l1_moe_w1, l1_moe_w3, l1_moe_w2)
    h = _even_layer(h, rope, bsz, seq, l2_norm1_g, l2_w_in, l2_conv_w, l2_conv_b, l2_b_igate, l2_b_fgate,
                    l2_mh_norm_g, l2_cq_norm_g, l2_w_uq, l2_ckv_norm_g, l2_w_ukv, l2_qn_g, l2_kn_g, l2_w_o,
                    l2_norm2_g, l2_ffn_w1, l2_ffn_w3, l2_ffn_w2)
    h = _odd_layer(h, bsz, seq, l3_norm1_g, l3_lam_re, l3_lam_im, l3_log_dt, l3_b_re, l3_b_im, l3_c_re, l3_c_im,
                   l3_d_skip, l3_glu_wv, l3_glu_wg, l3_norm2_g, l3_router_w, l3_moe_w1, l3_moe_w3, l3_moe_w2)
    return h.reshape(bsz, seq, D_MODEL)
```

```python
import functools
import math

import jax
import jax.numpy as jnp
import numpy as np
from jax import lax
from jax.experimental import pallas as pl
from jax.experimental.pallas import tpu as pltpu

F32 = jnp.float32
BF16 = jnp.bfloat16

D_MODEL = 1024
NORM_EPS = 1e-6
M_HEADS = 4
M_HEAD_DIM = 128
M_WIDTH = M_HEADS * M_HEAD_DIM
CONV_WIDTH = 4
A_HEADS = 8
Q_LORA = 256
KV_LORA = 128
QK_NOPE = 64
QK_ROPE = 32
QK_HEAD = QK_NOPE + QK_ROPE
V_HEAD = 64
A_WIDTH = A_HEADS * V_HEAD
ROPE_THETA = 10000.0
MLA_CHUNK = 64
S5_GROUP = 16
S5_GROUPS = D_MODEL // S5_GROUP
S5_STATE = 64
D_FF = 2816
N_EXPERTS = 8
D_FF_EXPERT = 3584

LANES = 128
HEAD_PAD = 128
Y_COLS = 4 * M_WIDTH + Q_LORA + KV_LORA
IG_LANE = 0
FG_LANE = 8
KR_LANE = QK_NOPE

VMEM_LIMIT = 56 * 1024 * 1024
NEG_BIG = -1e30
LOG2_E = math.log2(math.e)


def _cparams(*sem):
    return pltpu.CompilerParams(dimension_semantics=sem, vmem_limit_bytes=VMEM_LIMIT)


def _rms(x, g):
    return x * lax.rsqrt(jnp.mean(x * x, axis=-1, keepdims=True) + NORM_EPS) * g


def _sigmoid(x):
    return 1.0 / (1.0 + jnp.exp(-x))


def _dot(a, b):
    return jnp.dot(a, b, preferred_element_type=F32)


def _dot_nt(a, b):
    return lax.dot_general(a, b, (((1,), (1,)), ((), ())), preferred_element_type=F32)


def _dot_tn(a, b):
    return lax.dot_general(a, b, (((0,), (0,)), ((), ())), preferred_element_type=F32)


def _const_spec(shape):
    nd = len(shape)
    return pl.BlockSpec(shape, lambda *_: (0,) * nd, pipeline_mode=pl.Buffered(1))


def _inproj_kernel(x_ref, g_ref, w_ref, y_ref, misc_ref):
    xn = _rms(x_ref[...], g_ref[...]).astype(BF16)
    y = _dot(xn, w_ref[...])
    y_ref[...] = y[:, :Y_COLS].astype(BF16)
    misc_ref[...] = y[:, Y_COLS:]


def _inproj(x, g, w_packed, tm):
    t = x.shape[0]
    return pl.pallas_call(
        _inproj_kernel,
        grid=(t // tm,),
        in_specs=[pl.BlockSpec((tm, D_MODEL), lambda i: (i, 0)),
                  _const_spec((1, D_MODEL)),
                  _const_spec((D_MODEL, Y_COLS + LANES))],
        out_specs=[pl.BlockSpec((tm, Y_COLS), lambda i: (i, 0)),
                   pl.BlockSpec((tm, LANES), lambda i: (i, 0))],
        out_shape=[jax.ShapeDtypeStruct((t, Y_COLS), BF16),
                   jax.ShapeDtypeStruct((t, LANES), F32)],
        compiler_params=_cparams("parallel"),
        name="inproj",
    )(x, g, w_packed)


def _cummax_lanes(r, width):
    lane = lax.broadcasted_iota(jnp.int32, r.shape, 1)
    s = 1
    while s < width:
        shifted = pltpu.roll(r, s, axis=1)
        r = jnp.where(lane >= s, jnp.maximum(r, shifted), r)
        s *= 2
    return r


def _mlstm_kernel(q_ref, k_ref, v_ref, og_ref, misc_ref, cwq_ref, cwk_ref, cbq_ref, cbk_ref, gb_ref, hg_ref,
                  out_ref, qpad, kpad, c_scr, n_scr, m_scr, *, lc):
    @pl.when(pl.program_id(1) == 0)
    def _():
        qpad[0:8, :] = jnp.zeros((8, M_WIDTH), F32)
        kpad[0:8, :] = jnp.zeros((8, M_WIDTH), F32)
        c_scr[...] = jnp.zeros_like(c_scr)
        n_scr[...] = jnp.zeros_like(n_scr)
        m_scr[...] = jnp.zeros_like(m_scr)

    qpad[8:8 + lc, :] = q_ref[...].astype(F32)
    kpad[8:8 + lc, :] = k_ref[...].astype(F32)

    def conv_silu(pad, cw_ref, cb_ref):
        acc = jnp.broadcast_to(cb_ref[...], (lc, M_WIDTH))
        for j in range(CONV_WIDTH):
            off = 8 - (CONV_WIDTH - 1) + j
            acc = acc + pad[off:off + lc, :] * cw_ref[j:j + 1, :]
        return acc * _sigmoid(acc)

    qc = conv_silu(qpad, cwq_ref, cbq_ref) * (M_HEAD_DIM ** -0.5)
    kc = conv_silu(kpad, cwk_ref, cbk_ref)
    qpad[0:8, :] = qpad[lc:lc + 8, :]
    kpad[0:8, :] = kpad[lc:lc + 8, :]

    gt = (misc_ref[...] + gb_ref[...]).T
    ig = gt[IG_LANE:IG_LANE + 8, :]
    fpre = gt[FG_LANE:FG_LANE + 8, :]
    lf = jnp.minimum(fpre, 0.0) - jnp.log(1.0 + jnp.exp(-jnp.abs(fpre)))
    hi = lf.astype(BF16).astype(F32)
    r1 = lf - hi
    mid = r1.astype(BF16).astype(F32)
    lo = r1 - mid
    stack = jnp.concatenate([hi, mid, lo, jnp.zeros_like(lo)], axis=0).astype(BF16)
    srow = lax.broadcasted_iota(jnp.int32, (lc, lc), 0)
    scol = lax.broadcasted_iota(jnp.int32, (lc, lc), 1)
    upper = (srow <= scol).astype(BF16)
    cs = _dot(stack, upper)
    b = cs[0:8] + cs[8:16] + cs[16:24]

    r = ig - b
    cm = _cummax_lanes(r, lc)
    mprev = m_scr[:, 0:1]
    mx = jnp.maximum(mprev, cm)
    w_inter = jnp.exp(mprev - mx)
    e_neg = jnp.exp(-(b + mx))
    total = b[:, lc - 1:lc]
    mnew = total + mx[:, lc - 1:lc]
    a_prev = jnp.exp(total + mprev - mnew)
    w_in = jnp.exp(total + r - mnew)
    m_scr[...] = jnp.broadcast_to(mnew, (8, LANES))

    cols = jnp.concatenate([-mx, w_inter, e_neg, w_in, jnp.zeros((LANES - 32, lc), F32)], axis=0).T
    tril = scol <= srow

    for h in range(M_HEADS):
        sl = slice(h * M_HEAD_DIM, (h + 1) * M_HEAD_DIM)
        qh = qc[:, sl]
        kh = kc[:, sl]
        vh = v_ref[:, sl]
        qb = qh.astype(BF16)
        u_col = cols[:, h:h + 1]
        wi_col = cols[:, 8 + h:9 + h]
        en_col = cols[:, 16 + h:17 + h]
        win_col = cols[:, 24 + h:25 + h]
        dmat = jnp.where(tril, u_col + r[h:h + 1, :], NEG_BIG)
        p = jnp.exp(dmat) * _dot_nt(qb, kh.astype(BF16))
        c_h = c_scr[h]
        n_h = n_scr[h:h + 1, :]
        num = wi_col * _dot(qb, c_h.astype(BF16)) + _dot(p.astype(BF16), vh)
        den = wi_col * jnp.sum(qh * n_h, axis=-1, keepdims=True) + jnp.sum(p, axis=-1, keepdims=True)
        ht = num / jnp.maximum(jnp.abs(den), en_col)
        kw = kh * win_col
        ah = a_prev[h:h + 1, :]
        c_scr[h] = ah * c_h + _dot_tn(kw.astype(BF16), vh)
        n_scr[h:h + 1, :] = ah * n_h + jnp.sum(kw, axis=0, keepdims=True)
        hn = _rms(ht, hg_ref[:, sl])
        out_ref[:, sl] = (_sigmoid(og_ref[:, sl].astype(F32)) * hn).astype(BF16)


def _mlstm(y, misc, cwq, cwk, cbq, cbk, gate_bias, hg, bsz, seq, lc):
    t = bsz * seq
    nch = seq // lc
    row = lambda b, c: b * nch + c
    return pl.pallas_call(
        functools.partial(_mlstm_kernel, lc=lc),
        grid=(bsz, nch),
        in_specs=[pl.BlockSpec((lc, M_WIDTH), lambda b, c: (row(b, c), 0)),
                  pl.BlockSpec((lc, M_WIDTH), lambda b, c: (row(b, c), 1)),
                  pl.BlockSpec((lc, M_WIDTH), lambda b, c: (row(b, c), 2)),
                  pl.BlockSpec((lc, M_WIDTH), lambda b, c: (row(b, c), 3)),
                  pl.BlockSpec((lc, LANES), lambda b, c: (row(b, c), 0)),
                  _const_spec((CONV_WIDTH, M_WIDTH)), _const_spec((CONV_WIDTH, M_WIDTH)),
                  _const_spec((1, M_WIDTH)), _const_spec((1, M_WIDTH)),
                  _const_spec((1, LANES)), _const_spec((1, M_WIDTH))],
        out_specs=pl.BlockSpec((lc, M_WIDTH), lambda b, c: (row(b, c), 0)),
        out_shape=jax.ShapeDtypeStruct((t, M_WIDTH), BF16),
        scratch_shapes=[pltpu.VMEM((lc + 8, M_WIDTH), F32), pltpu.VMEM((lc + 8, M_WIDTH), F32),
                        pltpu.VMEM((M_HEADS, M_HEAD_DIM, M_HEAD_DIM), F32),
                        pltpu.VMEM((8, M_HEAD_DIM), F32), pltpu.VMEM((8, LANES), F32)],
        compiler_params=_cparams("parallel", "arbitrary"),
        name="mlstm",
    )(y, y, y, y, misc, cwq, cwk, cbq, cbk, gate_bias, hg)


def _mla_prep_kernel(cq_ref, ckv_ref, misc_ref, cos_ref, sa_ref, sb_ref, cqg_ref, wuq_ref, ckvg_ref, wuk_ref,
                     wuv_ref, qng_ref, kng_ref, q_ref, k_ref, v_ref):
    cqn = _rms(cq_ref[...].astype(F32), cqg_ref[...]).astype(BF16)
    qa = _dot(cqn, wuq_ref[...])
    ckvn = _rms(ckv_ref[...].astype(F32), ckvg_ref[...]).astype(BF16)
    ka = _dot(ckvn, wuk_ref[...])
    va = _dot(ckvn, wuv_ref[...])
    lane = lax.broadcasted_iota(jnp.int32, misc_ref.shape, 1)
    kr = jnp.where((lane >= KR_LANE) & (lane < KR_LANE + QK_ROPE), misc_ref[...], 0.0)
    cos = cos_ref[...]
    sa = sa_ref[...]
    sb = sb_ref[...]

    def norm_rope(xh, g):
        xh = xh * lax.rsqrt(jnp.sum(xh * xh, axis=-1, keepdims=True) * (1.0 / QK_HEAD) + NORM_EPS) * g
        half = QK_ROPE // 2
        return xh * cos + pltpu.roll(xh, HEAD_PAD - half, axis=1) * sa + pltpu.roll(xh, half, axis=1) * sb

    for h in range(A_HEADS):
        sl = slice(h * HEAD_PAD, (h + 1) * HEAD_PAD)
        q_ref[0, h, 0] = (norm_rope(qa[:, sl], qng_ref[...]) * (LOG2_E * QK_HEAD ** -0.5)).T.astype(BF16)
        k_ref[0, h] = norm_rope(ka[:, sl] + kr, kng_ref[...]).astype(BF16)
        v_ref[0, h, 0] = jnp.where(lane < V_HEAD, va[:, sl], 1.0).T.astype(BF16)


def _mla_prep(y, misc, cos_t, sin_a, sin_b, cqg, wuq, ckvg, wuk, wuv, qng, kng, bsz, seq, ts):
    t = bsz * seq
    nt = seq // ts
    row = lambda b, i: (b * nt + i, 0)
    cq_blk = (4 * M_WIDTH) // Q_LORA
    ckv_blk = (4 * M_WIDTH + Q_LORA) // KV_LORA
    hw = A_HEADS * HEAD_PAD
    return pl.pallas_call(
        _mla_prep_kernel,
        grid=(bsz, nt),
        in_specs=[pl.BlockSpec((ts, Q_LORA), lambda b, i: (b * nt + i, cq_blk)),
                  pl.BlockSpec((ts, KV_LORA), lambda b, i: (b * nt + i, ckv_blk)),
                  pl.BlockSpec((ts, LANES), row), pl.BlockSpec((ts, LANES), row),
                  pl.BlockSpec((ts, LANES), row), pl.BlockSpec((ts, LANES), row),
                  _const_spec((1, Q_LORA)), _const_spec((Q_LORA, hw)),
                  _const_spec((1, KV_LORA)), _const_spec((KV_LORA, hw)), _const_spec((KV_LORA, hw)),
                  _const_spec((1, HEAD_PAD)), _const_spec((1, HEAD_PAD))],
        out_specs=[pl.BlockSpec((1, A_HEADS, 1, HEAD_PAD, ts), lambda b, i: (b, 0, i, 0, 0)),
                   pl.BlockSpec((1, A_HEADS, ts, HEAD_PAD), lambda b, i: (b, 0, i, 0)),
                   pl.BlockSpec((1, A_HEADS, 1, HEAD_PAD, ts), lambda b, i: (b, 0, i, 0, 0))],
        out_shape=[jax.ShapeDtypeStruct((bsz, A_HEADS, nt, HEAD_PAD, ts), BF16),
                   jax.ShapeDtypeStruct((bsz, A_HEADS, seq, HEAD_PAD), BF16),
                   jax.ShapeDtypeStruct((bsz, A_HEADS, nt, HEAD_PAD, ts), BF16)],
        compiler_params=_cparams("parallel", "parallel"),
        name="mla_prep",
    )(y, y, misc, cos_t, sin_a, sin_b, cqg, wuq, ckvg, wuk, wuv, qng, kng)


def _attn_kernel(q_ref, k_ref, v_ref, o_ref, *scratch, tq):
    m_refs, acc_refs = scratch[:A_HEADS], scratch[A_HEADS:]
    i = pl.program_id(1)
    key = lax.broadcasted_iota(jnp.int32, (tq, tq), 0)
    qry = lax.broadcasted_iota(jnp.int32, (tq, tq), 1)
    dmask = (key // MLA_CHUNK) <= (qry // MLA_CHUNK)
    for h in range(A_HEADS):
        m_refs[h][...] = jnp.full((1, tq), NEG_BIG, F32)
        acc_refs[h][...] = jnp.zeros((HEAD_PAD, tq), F32)

    def update(kt, masked):
        off = pl.multiple_of(kt * tq, tq)
        scores = [_dot(k_ref[0, h, pl.ds(off, tq), :], q_ref[0, h, 0]) for h in range(A_HEADS)]
        for h in range(A_HEADS):
            st = scores[h]
            if masked:
                st = jnp.where(dmask, st, NEG_BIG)
            m = m_refs[h][...]
            mn = jnp.maximum(m, jnp.max(st, axis=0, keepdims=True))
            p = jnp.exp2(st - mn)
            acc_refs[h][...] = jnp.exp2(m - mn) * acc_refs[h][...] + _dot(v_ref[0, h, kt], p.astype(BF16))
            m_refs[h][...] = mn

    def body(kt, carry):
        update(kt, False)
        return carry

    lax.fori_loop(0, i, body, 0)
    update(i, True)
    for hp in range(A_HEADS // 2):
        a0 = acc_refs[2 * hp][...]
        a1 = acc_refs[2 * hp + 1][...]
        pair = jnp.concatenate([a0[:V_HEAD] / a0[V_HEAD:V_HEAD + 1], a1[:V_HEAD] / a1[V_HEAD:V_HEAD + 1]], axis=0)
        o_ref[:, hp * LANES:(hp + 1) * LANES] = pair.T.astype(BF16)


def _attention(q_t, k, v_t, bsz, seq, tq):
    nq = seq // tq
    return pl.pallas_call(
        functools.partial(_attn_kernel, tq=tq),
        grid=(bsz, nq),
        in_specs=[pl.BlockSpec((1, A_HEADS, 1, HEAD_PAD, tq), lambda b, i: (b, 0, i, 0, 0)),
                  pl.BlockSpec((1, A_HEADS, seq, HEAD_PAD), lambda b, i: (b, 0, 0, 0)),
                  pl.BlockSpec((1, A_HEADS, nq, HEAD_PAD, tq), lambda b, i: (b, 0, 0, 0, 0))],
        out_specs=pl.BlockSpec((tq, A_WIDTH), lambda b, i: (b * nq + i, 0)),
        out_shape=jax.ShapeDtypeStruct((bsz * seq, A_WIDTH), BF16),
        scratch_shapes=([pltpu.VMEM((1, tq), F32)] * A_HEADS + [pltpu.VMEM((HEAD_PAD, tq), F32)] * A_HEADS),
        compiler_params=_cparams("parallel", "arbitrary"),
        name="mla_attention",
    )(q_t, k, v_t)


def _outproj_ffn_kernel(m_ref, a_ref, x_ref, wom_ref, woa_ref, g_ref, w1_ref, w3_ref, w2_ref, o_ref):
    x1 = x_ref[...] + _dot(m_ref[...], wom_ref[...]) + _dot(a_ref[...], woa_ref[...])
    hn = _rms(x1, g_ref[...]).astype(BF16)
    a = _dot(hn, w1_ref[...])
    b = _dot(hn, w3_ref[...])
    o_ref[...] = x1 + _dot((a * _sigmoid(a) * b).astype(BF16), w2_ref[...])


def _outproj_ffn(m_out, a_out, x, wom, woa, g, w1, w3, w2, tm):
    t = x.shape[0]
    return pl.pallas_call(
        _outproj_ffn_kernel,
        grid=(t // tm,),
        in_specs=[pl.BlockSpec((tm, M_WIDTH), lambda i: (i, 0)),
                  pl.BlockSpec((tm, A_WIDTH), lambda i: (i, 0)),
                  pl.BlockSpec((tm, D_MODEL), lambda i: (i, 0)),
                  _const_spec((M_WIDTH, D_MODEL)), _const_spec((A_WIDTH, D_MODEL)), _const_spec((1, D_MODEL)),
                  _const_spec((D_MODEL, D_FF)), _const_spec((D_MODEL, D_FF)), _const_spec((D_FF, D_MODEL))],
        out_specs=pl.BlockSpec((tm, D_MODEL), lambda i: (i, 0)),
        out_shape=jax.ShapeDtypeStruct((t, D_MODEL), F32),
        compiler_params=_cparams("parallel"),
        name="outproj_ffn",
    )(m_out, a_out, x, wom, woa, g, w1, w3, w2)


S5_CHUNK = 16
S5_ROW = S5_CHUNK * S5_GROUP


def _norm_kernel(x_ref, g_ref, o_ref):
    o_ref[...] = _rms(x_ref[...], g_ref[...]).astype(BF16)


def _norm(x, g, tm):
    t = x.shape[0]
    return pl.pallas_call(
        _norm_kernel,
        grid=(t // tm,),
        in_specs=[pl.BlockSpec((tm, D_MODEL), lambda i: (i, 0)), _const_spec((1, D_MODEL))],
        out_specs=pl.BlockSpec((tm, D_MODEL), lambda i: (i, 0)),
        out_shape=jax.ShapeDtypeStruct((t, D_MODEL), BF16),
        compiler_params=_cparams("parallel"),
        name="norm",
    )(x, g)


def _gelu_tanh(x):
    return 0.5 * x * (1.0 + jnp.tanh(math.sqrt(2.0 / math.pi) * (x + 0.044715 * (x * x * x))))


def _s5_kernel(u_ref, w_ref, cst_ref, a1_ref, a2_ref, dk_ref, z_ref, vv_scr, xp_scr, *, nch, bsz):
    two_p = 2 * S5_STATE
    u = u_ref[0]
    r = _dot(u, w_ref[0])
    vv_scr[...] = r[:, S5_ROW:]
    a1 = a1_ref[0]
    a2 = a2_ref[0]

    def body(c, carry):
        p, q = carry
        off = pl.multiple_of(c * bsz, bsz)
        xp_scr[pl.ds(off, bsz), :] = p
        vv = vv_scr[pl.ds(off, bsz), :]
        pn = a1 * p + a2 * q + vv[:, :two_p]
        qn = a1 * q - a2 * p + vv[:, two_p:]
        return pn, qn

    zero = jnp.zeros((bsz, two_p), F32)
    lax.fori_loop(0, nch, body, (zero, zero))
    y = r[:, :S5_ROW] + _dot(xp_scr[...].astype(BF16), cst_ref[0]) + dk_ref[0] * u.astype(F32)
    z_ref[0] = _gelu_tanh(y).astype(BF16)


def _s5(u_g, w_all, cst, a1, a2, dk, nch, bsz):
    g, n, _ = u_g.shape
    two_p = 2 * S5_STATE
    wcols = S5_ROW + 2 * two_p
    return pl.pallas_call(
        functools.partial(_s5_kernel, nch=nch, bsz=bsz),
        grid=(g,),
        in_specs=[pl.BlockSpec((1, n, S5_ROW), lambda i: (i, 0, 0)),
                  pl.BlockSpec((1, S5_ROW, wcols), lambda i: (i, 0, 0)),
                  pl.BlockSpec((1, two_p, S5_ROW), lambda i: (i, 0, 0)),
                  pl.BlockSpec((1, 1, two_p), lambda i: (i, 0, 0)),
                  pl.BlockSpec((1, 1, two_p), lambda i: (i, 0, 0)),
                  pl.BlockSpec((1, 1, S5_ROW), lambda i: (i, 0, 0))],
        out_specs=pl.BlockSpec((1, n, S5_ROW), lambda i: (i, 0, 0)),
        out_shape=jax.ShapeDtypeStruct((g, n, S5_ROW), BF16),
        scratch_shapes=[pltpu.VMEM((n, 2 * two_p), F32), pltpu.VMEM((n, two_p), F32)],
        compiler_params=_cparams("parallel"),
        name="s5",
    )(u_g, w_all, cst, a1, a2, dk)


def _s5_weights(lam_re, lam_im, log_dt, b_re, b_im, c_re, c_im, d_skip):
    lc, h, p = S5_CHUNK, S5_GROUP, S5_STATE
    lam = lax.complex(jnp.minimum(lam_re.astype(F32), -1e-4), lam_im.astype(F32))
    dt = jnp.exp(log_dt.astype(F32))[:, None]
    ldt = lam * dt
    a_bar = jnp.exp(ldt)
    b_bar = ((a_bar - 1.0) / lam)[..., None] * lax.complex(b_re.astype(F32), b_im.astype(F32))
    c_mat = lax.complex(c_re.astype(F32), c_im.astype(F32))
    steps = jnp.arange(lc + 1, dtype=F32)
    apow = jnp.exp(ldt[None] * steps[:, None, None].astype(ldt.dtype))
    g = lam.shape[0]
    ker = jnp.einsum('gip,kgp,gpj->gkij', c_mat, apow[:lc], b_bar).real
    lag = np.arange(lc)[None, :] - np.arange(lc)[:, None]
    toe = jnp.where((lag >= 0)[None, :, :, None, None], ker[:, np.clip(lag, 0, lc - 1)], 0.0)
    toe = toe.transpose(0, 1, 4, 2, 3).reshape(g, lc * h, lc * h)
    wst = apow[:lc][::-1].transpose(1, 0, 2)[:, :, None, :] * jnp.swapaxes(b_bar, 1, 2)[:, None, :, :]
    wst = wst.reshape(g, lc * h, p)
    w_all = jnp.concatenate([toe, wst.real, wst.imag, wst.imag, wst.real], axis=-1).astype(BF16)
    ca = c_mat[:, None, :, :] * apow[1:lc + 1].transpose(1, 0, 2)[:, :, None, :]
    ca = ca.reshape(g, lc * h, p)
    cst = jnp.concatenate([ca.real, -ca.imag], axis=-1).transpose(0, 2, 1).astype(BF16)
    al = apow[lc]
    a1 = jnp.concatenate([al.real, al.real], axis=-1)[:, None, :]
    a2 = jnp.concatenate([-al.imag, al.imag], axis=-1)[:, None, :]
    dk = jnp.tile(d_skip.astype(F32).reshape(g, 1, h), (1, lc, 1)).reshape(g, 1, lc * h)
    return w_all, cst, a1, a2, dk


def _glu_kernel(z_ref, x_ref, wv_ref, wg_ref, g_ref, x2_ref, xn_ref):
    z = z_ref[...]
    x2 = x_ref[...] + _dot(z, wv_ref[...]) * _sigmoid(_dot(z, wg_ref[...]))
    x2_ref[...] = x2
    xn_ref[...] = _rms(x2, g_ref[...]).astype(BF16)


def _glu(z, x, wv, wg, g, tm):
    t = x.shape[0]
    return pl.pallas_call(
        _glu_kernel,
        grid=(t // tm,),
        in_specs=[pl.BlockSpec((tm, D_MODEL), lambda i: (i, 0)), pl.BlockSpec((tm, D_MODEL), lambda i: (i, 0)),
                  _const_spec((D_MODEL, D_MODEL)), _const_spec((D_MODEL, D_MODEL)), _const_spec((1, D_MODEL))],
        out_specs=[pl.BlockSpec((tm, D_MODEL), lambda i: (i, 0)), pl.BlockSpec((tm, D_MODEL), lambda i: (i, 0))],
        out_shape=[jax.ShapeDtypeStruct((t, D_MODEL), F32), jax.ShapeDtypeStruct((t, D_MODEL), BF16)],
        compiler_params=_cparams("parallel"),
        name="glu",
    )(z, x, wv, wg, g)


def _router_kernel(xn_ref, wh_ref, wl_ref, pos_ref, gate_ref, posc_ref, cnt_ref, *, tb):
    xn = xn_ref[...]
    lg = (_dot_nt(wh_ref[...], xn) + _dot_nt(wl_ref[...], xn))[0:N_EXPERTS]
    eidx = lax.broadcasted_iota(jnp.int32, (N_EXPERTS, tb), 0)
    m1 = jnp.max(lg, axis=0, keepdims=True)
    i1 = jnp.min(jnp.where(lg == m1, eidx, N_EXPERTS), axis=0, keepdims=True)
    rest = jnp.where(eidx == i1, -jnp.inf, lg)
    m2 = jnp.max(rest, axis=0, keepdims=True)
    i2 = jnp.min(jnp.where(rest == m2, eidx, N_EXPERTS), axis=0, keepdims=True)
    e21 = jnp.exp(m2 - m1)
    p1 = 1.0 / (1.0 + e21)
    sel1 = eidx == i1
    sel2 = eidx == i2
    sel = sel1 | sel2
    gate_ref[...] = jnp.where(sel1, p1, jnp.where(sel2, e21 * p1, 0.0))
    onehot = jnp.concatenate([sel.astype(F32), jnp.zeros((N_EXPERTS, tb), F32)], axis=0).astype(BF16)
    srow = lax.broadcasted_iota(jnp.int32, (tb, tb), 0)
    scol = lax.broadcasted_iota(jnp.int32, (tb, tb), 1)
    rank = _dot(onehot, (srow < scol).astype(BF16))[0:N_EXPERTS]
    pos = jnp.where(sel, rank, -1.0)
    pos_ref[...] = pos
    posc_ref[...] = jnp.concatenate([pos, jnp.full((LANES - N_EXPERTS, tb), -1.0, F32)], axis=0).T
    cnt = jnp.sum(sel.astype(F32), axis=1, keepdims=True)
    cnt_ref[0] = jnp.broadcast_to(cnt, (N_EXPERTS, LANES))


def _router(xn, wr_hi, wr_lo, tb):
    t = xn.shape[0]
    nb = t // tb
    return pl.pallas_call(
        functools.partial(_router_kernel, tb=tb),
        grid=(nb,),
        in_specs=[pl.BlockSpec((tb, D_MODEL), lambda i: (i, 0)),
                  _const_spec((2 * N_EXPERTS, D_MODEL)), _const_spec((2 * N_EXPERTS, D_MODEL))],
        out_specs=[pl.BlockSpec((N_EXPERTS, tb), lambda i: (0, i)), pl.BlockSpec((N_EXPERTS, tb), lambda i: (0, i)),
                   pl.BlockSpec((tb, LANES), lambda i: (i, 0)),
                   pl.BlockSpec((1, N_EXPERTS, LANES), lambda i: (i, 0, 0))],
        out_shape=[jax.ShapeDtypeStruct((N_EXPERTS, t), F32), jax.ShapeDtypeStruct((N_EXPERTS, t), F32),
                   jax.ShapeDtypeStruct((t, LANES), F32), jax.ShapeDtypeStruct((nb, N_EXPERTS, LANES), F32)],
        compiler_params=_cparams("parallel"),
        name="router",
    )(xn, wr_hi, wr_lo)


MOE_UNIT = 128
MOE_FF_SPLIT = 2


def _moe_kernel(cnt_ref, xn_ref, x_ref, pos_ref, gate_ref, posc_ref, w1_ref, w3_ref, w2_ref, o_ref, xe_scr, ye_scr,
                *, tb):
    b = pl.program_id(0)
    e = pl.program_id(1)
    f = pl.program_id(2)
    unit = MOE_UNIT
    n_unit = (cnt_ref[b * N_EXPERTS + e] + unit - 1) // unit
    n_pair = n_unit // 2
    odd = n_unit % 2 == 1
    tail_base = n_pair * (2 * unit)

    @pl.when((e == 0) & (f == 0))
    def _():
        o_ref[...] = x_ref[...]

    prow = pos_ref[pl.ds(e, 1), :]
    grow = gate_ref[pl.ds(e, 1), :]

    def ffn_chunk(base, m):
        @pl.when(f == 0)
        def _():
            slot = lax.broadcasted_iota(jnp.int32, (m, tb), 0).astype(F32) + base.astype(F32)
            xe_scr[pl.ds(base, m), :] = _dot((prow == slot).astype(BF16), xn_ref[...]).astype(BF16)

        xe = xe_scr[pl.ds(base, m), :]
        a = _dot(xe, w1_ref[0])
        g = (a * _sigmoid(a) * _dot(xe, w3_ref[0])).astype(BF16)
        y = _dot(g, w2_ref[0])

        @pl.when(f == 0)
        def _():
            ye_scr[pl.ds(base, m), :] = y

        @pl.when(f != 0)
        def _():
            ye_scr[pl.ds(base, m), :] += y

    def ffn_pair(r, carry):
        ffn_chunk(pl.multiple_of(r * (2 * unit), 2 * unit), 2 * unit)
        return carry

    lax.fori_loop(0, n_pair, ffn_pair, 0)

    @pl.when(odd)
    def _():
        ffn_chunk(pl.multiple_of(tail_base, 2 * unit), unit)

    @pl.when(f == MOE_FF_SPLIT - 1)
    def _():
        lane = lax.broadcasted_iota(jnp.int32, (tb, LANES), 1)
        pcol = jnp.sum(jnp.where(lane == e, posc_ref[...], 0.0), axis=-1, keepdims=True)

        def scatter_chunk(base, m):
            slot_r = lax.broadcasted_iota(jnp.int32, (m, tb), 0).astype(F32) + base.astype(F32)
            gs = jnp.sum(jnp.where(prow == slot_r, grow, 0.0), axis=-1, keepdims=True)
            yg = (ye_scr[pl.ds(base, m), :] * gs).astype(BF16)
            slot_c = lax.broadcasted_iota(jnp.int32, (tb, m), 1).astype(F32) + base.astype(F32)
            o_ref[...] += _dot((pcol == slot_c).astype(BF16), yg)

        def scatter_pair(r, carry):
            scatter_chunk(pl.multiple_of(r * (2 * unit), 2 * unit), 2 * unit)
            return carry

        lax.fori_loop(0, n_pair, scatter_pair, 0)

        @pl.when(odd)
        def _():
            scatter_chunk(pl.multiple_of(tail_base, 2 * unit), unit)


def _moe(cnt, xn, x, pos, gate, posc, w1, w3, w2, tb):
    t = x.shape[0]
    nb = t // tb
    fs = D_FF_EXPERT // MOE_FF_SPLIT
    grid_spec = pltpu.PrefetchScalarGridSpec(
        num_scalar_prefetch=1,
        grid=(nb, N_EXPERTS, MOE_FF_SPLIT),
        in_specs=[pl.BlockSpec((tb, D_MODEL), lambda b, e, f, c: (b, 0)),
                  pl.BlockSpec((tb, D_MODEL), lambda b, e, f, c: (b, 0), pipeline_mode=pl.Buffered(1)),
                  pl.BlockSpec((N_EXPERTS, tb), lambda b, e, f, c: (0, b)),
                  pl.BlockSpec((N_EXPERTS, tb), lambda b, e, f, c: (0, b)),
                  pl.BlockSpec((tb, LANES), lambda b, e, f, c: (b, 0)),
                  pl.BlockSpec((1, D_MODEL, fs), lambda b, e, f, c: (e, 0, f)),
                  pl.BlockSpec((1, D_MODEL, fs), lambda b, e, f, c: (e, 0, f)),
                  pl.BlockSpec((1, fs, D_MODEL), lambda b, e, f, c: (e, f, 0))],
        out_specs=pl.BlockSpec((tb, D_MODEL), lambda b, e, f, c: (b, 0)),
        scratch_shapes=[pltpu.VMEM((tb, D_MODEL), BF16), pltpu.VMEM((tb, D_MODEL), F32)],
    )
    return pl.pallas_call(
        functools.partial(_moe_kernel, tb=tb),
        grid_spec=grid_spec,
        out_shape=jax.ShapeDtypeStruct((t, D_MODEL), F32),
        compiler_params=_cparams("parallel", "arbitrary", "arbitrary"),
        name="moe",
    )(cnt, xn, x, pos, gate, posc, w1, w3, w2)


def _rope_tables(positions):
    half = QK_ROPE // 2
    inv_freq = ROPE_THETA ** (-jnp.arange(0, QK_ROPE, 2, dtype=F32) / QK_ROPE)
    ang = positions.astype(F32).reshape(-1, 1) * inv_freq
    cos, sin = jnp.cos(ang), jnp.sin(ang)
    t = ang.shape[0]
    ones_lo = jnp.ones((t, QK_NOPE), F32)
    zeros_lo = jnp.zeros((t, QK_NOPE), F32)
    zeros_h = jnp.zeros((t, half), F32)
    tail1 = jnp.ones((t, HEAD_PAD - QK_HEAD), F32)
    tail0 = jnp.zeros((t, HEAD_PAD - QK_HEAD), F32)
    cos_t = jnp.concatenate([ones_lo, cos, cos, tail1], axis=-1)
    sin_a = jnp.concatenate([zeros_lo, -sin, zeros_h, tail0], axis=-1)
    sin_b = jnp.concatenate([zeros_lo, zeros_h, sin, tail0], axis=-1)
    return cos_t, sin_a, sin_b


def _pad_heads(w, per_head_in, keep, heads):
    k = w.shape[0]
    w = w.reshape(k, heads, per_head_in)[:, :, :keep]
    return jnp.pad(w, ((0, 0), (0, 0), (0, HEAD_PAD - keep))).reshape(k, heads * HEAD_PAD)


def _even_layer(x, rope, bsz, seq, norm1_g, w_in, conv_w, conv_b, b_igate, b_fgate, mh_norm_g, cq_norm_g, w_uq,
                ckv_norm_g, w_ukv, qn_g, kn_g, w_o, norm2_g, ffn_w1, ffn_w3, ffn_w2):
    row = lambda v: v.astype(F32).reshape(1, -1)
    mw = M_WIDTH
    o_ig, o_fg, o_og = 3 * mw, 3 * mw + M_HEADS, 3 * mw + 2 * M_HEADS
    o_cq = o_og + mw
    o_ckv, o_kr = o_cq + Q_LORA, o_cq + Q_LORA + KV_LORA
    misc_w = jnp.zeros((D_MODEL, LANES), F32)
    misc_w = misc_w.at[:, IG_LANE:IG_LANE + M_HEADS].set(w_in[:, o_ig:o_fg])
    misc_w = misc_w.at[:, FG_LANE:FG_LANE + M_HEADS].set(w_in[:, o_fg:o_og])
    misc_w = misc_w.at[:, KR_LANE:KR_LANE + QK_ROPE].set(w_in[:, o_kr:o_kr + QK_ROPE])
    w_packed = jnp.concatenate([w_in[:, :3 * mw], w_in[:, o_og:o_cq], w_in[:, o_cq:o_ckv], w_in[:, o_ckv:o_kr],
                                misc_w], axis=-1).astype(BF16)
    gate_bias = jnp.zeros((1, LANES), F32)
    gate_bias = gate_bias.at[0, IG_LANE:IG_LANE + M_HEADS].set(b_igate.astype(F32))
    gate_bias = gate_bias.at[0, FG_LANE:FG_LANE + M_HEADS].set(b_fgate.astype(F32))

    tm = min(512, x.shape[0])
    y, misc = _inproj(x, row(norm1_g), w_packed, tm)

    lc = min(256, seq)
    m_out = _mlstm(y, misc, conv_w[:, :mw].astype(F32), conv_w[:, mw:].astype(F32), row(conv_b[:mw]),
                   row(conv_b[mw:]), gate_bias, row(mh_norm_g), bsz, seq, lc)

    wuq = _pad_heads(w_uq, QK_HEAD, QK_HEAD, A_HEADS).astype(BF16)
    wuk = _pad_heads(w_ukv, QK_NOPE + V_HEAD, QK_NOPE, A_HEADS).astype(BF16)
    wuv = _pad_heads(w_ukv.reshape(KV_LORA, A_HEADS, QK_NOPE + V_HEAD)[:, :, QK_NOPE:].reshape(KV_LORA, A_WIDTH),
                     V_HEAD, V_HEAD, A_HEADS).astype(BF16)
    pad_g = lambda g: jnp.pad(g.astype(F32), (0, HEAD_PAD - QK_HEAD)).reshape(1, HEAD_PAD)
    ts = min(256, seq)
    q, k, v = _mla_prep(y, misc, rope[0], rope[1], rope[2], row(cq_norm_g), wuq, row(ckv_norm_g), wuk, wuv,
                        pad_g(qn_g), pad_g(kn_g), bsz, seq, ts)
    a_out = _attention(q, k, v, bsz, seq, ts)

    return _outproj_ffn(m_out, a_out, x, w_o[:mw].astype(BF16), w_o[mw:].astype(BF16), row(norm2_g),
                        ffn_w1.astype(BF16), ffn_w3.astype(BF16), ffn_w2.astype(BF16), tm)


def _odd_layer(x, bsz, seq, norm1_g, lam_re, lam_im, log_dt, b_re, b_im, c_re, c_im, d_skip, glu_wv, glu_wg,
               norm2_g, router_w, moe_w1, moe_w3, moe_w2):
    row = lambda v: v.astype(F32).reshape(1, -1)
    t = x.shape[0]
    tm = min(512, t)
    u = _norm(x, row(norm1_g), tm)
    nch = seq // S5_CHUNK
    u_g = u.reshape(bsz, nch, S5_CHUNK, S5_GROUPS, S5_GROUP).transpose(3, 1, 0, 2, 4)
    u_g = u_g.reshape(S5_GROUPS, nch * bsz, S5_ROW)
    w_all, cst, a1, a2, dk = _s5_weights(lam_re, lam_im, log_dt, b_re, b_im, c_re, c_im, d_skip)
    z_g = _s5(u_g, w_all, cst, a1, a2, dk, nch, bsz)
    z = z_g.reshape(S5_GROUPS, nch, bsz, S5_CHUNK, S5_GROUP).transpose(2, 1, 3, 0, 4).reshape(t, D_MODEL)
    x2, xn = _glu(z, x, glu_wv.astype(BF16), glu_wg.astype(BF16), row(norm2_g), tm)

    tb = min(1024, t)
    wr = jnp.pad(router_w.astype(F32).T, ((0, N_EXPERTS), (0, 0)))
    wr_hi = wr.astype(BF16)
    wr_lo = (wr - wr_hi.astype(F32)).astype(BF16)
    pos, gate, posc, cnt = _router(xn, wr_hi, wr_lo, tb)
    cnt = cnt[:, :, 0].astype(jnp.int32).reshape(-1)
    return _moe(cnt, xn, x2, pos, gate, posc, moe_w1.astype(BF16), moe_w3.astype(BF16), moe_w2.astype(BF16), tb)


def kernel(x, positions, l0_norm1_g, l0_w_in, l0_conv_w, l0_conv_b, l0_b_igate, l0_b_fgate, l0_mh_norm_g, l0_cq_norm_g, l0_w_uq, l0_ckv_norm_g, l0_w_ukv, l0_qn_g, l0_kn_g, l0_w_o, l0_norm2_g, l0_ffn_w1, l0_ffn_w3, l0_ffn_w2, l1_norm1_g, l1_lam_re, l1_lam_im, l1_log_dt, l1_b_re, l1_b_im, l1_c_re, l1_c_im, l1_d_skip, l1_glu_wv, l1_glu_wg, l1_norm2_g, l1_router_w, l1_moe_w1, l1_moe_w3, l1_moe_w2, l2_norm1_g, l2_w_in, l2_conv_w, l2_conv_b, l2_b_igate, l2_b_fgate, l2_mh_norm_g, l2_cq_norm_g, l2_w_uq, l2_ckv_norm_g, l2_w_ukv, l2_qn_g, l2_kn_g, l2_w_o, l2_norm2_g, l2_ffn_w1, l2_ffn_w3, l2_ffn_w2, l3_norm1_g, l3_lam_re, l3_lam_im, l3_log_dt, l3_b_re, l3_b_im, l3_c_re, l3_c_im, l3_d_skip, l3_glu_wv, l3_glu_wg, l3_norm2_g, l3_router_w, l3_moe_w1, l3_moe_w3, l3_moe_w2):
    bsz, seq, _ = x.shape
    rope = _rope_tables(positions)
    h = x.reshape(bsz * seq, D_MODEL)
    h = _even_layer(h, rope, bsz, seq, l0_norm1_g, l0_w_in, l0_conv_w, l0_conv_b, l0_b_igate, l0_b_fgate,
                    l0_mh_norm_g, l0_cq_norm_g, l0_w_uq, l0_ckv_norm_g, l0_w_ukv, l0_qn_g, l0_kn_g, l0_w_o,
                    l0_norm2_g, l0_ffn_w1, l0_ffn_w3, l0_ffn_w2)
    h = _odd_layer(h, bsz, seq, l1_norm1_g, l1_lam_re, l1_lam_im, l1_log_dt, l1_b_re, l1_b_im, l1_c_re, l1_c_im,
                   l1_d_skip, l1_glu_wv, l1_glu_wg, l1_norm2_g, l1_router_w, l1_moe_w1, l1_moe_w3, l1_moe_w2)
    h = _even_layer(h, rope, bsz, seq, l2_norm1_g, l2_w_in, l2_conv_w, l2_conv_b, l2_b_igate, l2_b_fgate,
                    l2_mh_norm_g, l2_cq_norm_g, l2_w_uq, l2_ckv_norm_g, l2_w_ukv, l2_qn_g, l2_kn_g, l2_w_o,
                    l2_norm2_g, l2_ffn_w1, l2_ffn_w3, l2_ffn_w2)
    h = _odd_layer(h, bsz, seq, l3_norm1_g, l3_lam_re, l3_lam_im, l3_log_dt, l3_b_re, l3_b_im, l3_c_re, l3_c_im,
                   l3_d_skip, l3_glu_wv, l3_glu_wg, l3_norm2_g, l3_router_w, l3_moe_w1, l3_moe_w3, l3_moe_w2)
    return h.reshape(bsz, seq, D_MODEL)
```

```python
import functools
import math

import jax
import jax.numpy as jnp
import numpy as np
from jax import lax
from jax.experimental import pallas as pl
from jax.experimental.pallas import tpu as pltpu

F32 = jnp.float32
BF16 = jnp.bfloat16

D_MODEL = 1024
NORM_EPS = 1e-6
M_HEADS = 4
M_HEAD_DIM = 128
M_WIDTH = M_HEADS * M_HEAD_DIM
CONV_WIDTH = 4
A_HEADS = 8
Q_LORA = 256
KV_LORA = 128
QK_NOPE = 64
QK_ROPE = 32
QK_HEAD = QK_NOPE + QK_ROPE
V_HEAD = 64
A_WIDTH = A_HEADS * V_HEAD
ROPE_THETA = 10000.0
MLA_CHUNK = 64
S5_GROUP = 16
S5_GROUPS = D_MODEL // S5_GROUP
S5_STATE = 64
D_FF = 2816
N_EXPERTS = 8
D_FF_EXPERT = 3584

LANES = 128
HEAD_PAD = 128
Y_COLS = 4 * M_WIDTH + Q_LORA + KV_LORA
IG_LANE = 0
FG_LANE = 8
KR_LANE = QK_NOPE

VMEM_LIMIT = 56 * 1024 * 1024
NEG_BIG = -1e30
LOG2_E = math.log2(math.e)


def _cparams(*sem):
    return pltpu.CompilerParams(dimension_semantics=sem, vmem_limit_bytes=VMEM_LIMIT)


def _rms(x, g):
    return x * lax.rsqrt(jnp.mean(x * x, axis=-1, keepdims=True) + NORM_EPS) * g


def _sigmoid(x):
    return 1.0 / (1.0 + jnp.exp(-x))


def _dot(a, b):
    return jnp.dot(a, b, preferred_element_type=F32)


def _dot_nt(a, b):
    return lax.dot_general(a, b, (((1,), (1,)), ((), ())), preferred_element_type=F32)


def _dot_tn(a, b):
    return lax.dot_general(a, b, (((0,), (0,)), ((), ())), preferred_element_type=F32)


def _const_spec(shape):
    nd = len(shape)
    return pl.BlockSpec(shape, lambda *_: (0,) * nd, pipeline_mode=pl.Buffered(1))


def _inproj_kernel(x_ref, g_ref, w_ref, y_ref, misc_ref):
    xn = _rms(x_ref[...], g_ref[...]).astype(BF16)
    y = _dot(xn, w_ref[...])
    y_ref[...] = y[:, :Y_COLS].astype(BF16)
    misc_ref[...] = y[:, Y_COLS:]


def _inproj(x, g, w_packed, tm):
    t = x.shape[0]
    return pl.pallas_call(
        _inproj_kernel,
        grid=(t // tm,),
        in_specs=[pl.BlockSpec((tm, D_MODEL), lambda i: (i, 0)),
                  _const_spec((1, D_MODEL)),
                  _const_spec((D_MODEL, Y_COLS + LANES))],
        out_specs=[pl.BlockSpec((tm, Y_COLS), lambda i: (i, 0)),
                   pl.BlockSpec((tm, LANES), lambda i: (i, 0))],
        out_shape=[jax.ShapeDtypeStruct((t, Y_COLS), BF16),
                   jax.ShapeDtypeStruct((t, LANES), F32)],
        compiler_params=_cparams("parallel"),
        name="inproj",
    )(x, g, w_packed)


def _cummax_lanes(r, width):
    lane = lax.broadcasted_iota(jnp.int32, r.shape, 1)
    s = 1
    while s < width:
        shifted = pltpu.roll(r, s, axis=1)
        r = jnp.where(lane >= s, jnp.maximum(r, shifted), r)
        s *= 2
    return r


def _mlstm_kernel(q_ref, k_ref, v_ref, og_ref, misc_ref, cwq_ref, cwk_ref, cbq_ref, cbk_ref, gb_ref, hg_ref,
                  out_ref, qpad, kpad, c_scr, n_scr, m_scr, *, lc):
    @pl.when(pl.program_id(1) == 0)
    def _():
        qpad[0:8, :] = jnp.zeros((8, M_WIDTH), F32)
        kpad[0:8, :] = jnp.zeros((8, M_WIDTH), F32)
        c_scr[...] = jnp.zeros_like(c_scr)
        n_scr[...] = jnp.zeros_like(n_scr)
        m_scr[...] = jnp.zeros_like(m_scr)

    qpad[8:8 + lc, :] = q_ref[...].astype(F32)
    kpad[8:8 + lc, :] = k_ref[...].astype(F32)

    def conv_silu(pad, cw_ref, cb_ref):
        acc = jnp.broadcast_to(cb_ref[...], (lc, M_WIDTH))
        for j in range(CONV_WIDTH):
            off = 8 - (CONV_WIDTH - 1) + j
            acc = acc + pad[off:off + lc, :] * cw_ref[j:j + 1, :]
        return acc * _sigmoid(acc)

    qc = conv_silu(qpad, cwq_ref, cbq_ref) * (M_HEAD_DIM ** -0.5)
    kc = conv_silu(kpad, cwk_ref, cbk_ref)
    qpad[0:8, :] = qpad[lc:lc + 8, :]
    kpad[0:8, :] = kpad[lc:lc + 8, :]

    gt = (misc_ref[...] + gb_ref[...]).T
    ig = gt[IG_LANE:IG_LANE + 8, :]
    fpre = gt[FG_LANE:FG_LANE + 8, :]
    lf = jnp.minimum(fpre, 0.0) - jnp.log(1.0 + jnp.exp(-jnp.abs(fpre)))
    hi = lf.astype(BF16).astype(F32)
    r1 = lf - hi
    mid = r1.astype(BF16).astype(F32)
    lo = r1 - mid
    stack = jnp.concatenate([hi, mid, lo, jnp.zeros_like(lo)], axis=0).astype(BF16)
    srow = lax.broadcasted_iota(jnp.int32, (lc, lc), 0)
    scol = lax.broadcasted_iota(jnp.int32, (lc, lc), 1)
    upper = (srow <= scol).astype(BF16)
    cs = _dot(stack, upper)
    b = cs[0:8] + cs[8:16] + cs[16:24]

    r = ig - b
    cm = _cummax_lanes(r, lc)
    mprev = m_scr[:, 0:1]
    mx = jnp.maximum(mprev, cm)
    w_inter = jnp.exp(mprev - mx)
    e_neg = jnp.exp(-(b + mx))
    total = b[:, lc - 1:lc]
    mnew = total + mx[:, lc - 1:lc]
    a_prev = jnp.exp(total + mprev - mnew)
    w_in = jnp.exp(total + r - mnew)
    m_scr[...] = jnp.broadcast_to(mnew, (8, LANES))

    cols = jnp.concatenate([-mx, w_inter, e_neg, w_in, jnp.zeros((LANES - 32, lc), F32)], axis=0).T
    tril = scol <= srow

    heads = [slice(h * M_HEAD_DIM, (h + 1) * M_HEAD_DIM) for h in range(M_HEADS)]
    qbs = [qc[:, sl].astype(BF16) for sl in heads]
    qk = [_dot_nt(qbs[h], kc[:, heads[h]].astype(BF16)) for h in range(M_HEADS)]
    qcs = [_dot(qbs[h], c_scr[h].astype(BF16)) for h in range(M_HEADS)]
    for h in range(M_HEADS):
        sl = heads[h]
        qh = qc[:, sl]
        kh = kc[:, sl]
        vh = v_ref[:, sl]
        u_col = cols[:, h:h + 1]
        wi_col = cols[:, 8 + h:9 + h]
        en_col = cols[:, 16 + h:17 + h]
        win_col = cols[:, 24 + h:25 + h]
        dmat = jnp.where(tril, u_col + r[h:h + 1, :], NEG_BIG)
        p = jnp.exp(dmat) * qk[h]
        n_h = n_scr[h:h + 1, :]
        num = wi_col * qcs[h] + _dot(p.astype(BF16), vh)
        den = wi_col * jnp.sum(qh * n_h, axis=-1, keepdims=True) + jnp.sum(p, axis=-1, keepdims=True)
        ht = num / jnp.maximum(jnp.abs(den), en_col)
        kw = kh * win_col
        ah = a_prev[h:h + 1, :]
        c_scr[h] = ah * c_scr[h] + _dot_tn(kw.astype(BF16), vh)
        n_scr[h:h + 1, :] = ah * n_h + jnp.sum(kw, axis=0, keepdims=True)
        hn = _rms(ht, hg_ref[:, sl])
        out_ref[:, sl] = (_sigmoid(og_ref[:, sl].astype(F32)) * hn).astype(BF16)


def _mlstm(y, misc, cwq, cwk, cbq, cbk, gate_bias, hg, bsz, seq, lc):
    t = bsz * seq
    nch = seq // lc
    row = lambda b, c: b * nch + c
    return pl.pallas_call(
        functools.partial(_mlstm_kernel, lc=lc),
        grid=(bsz, nch),
        in_specs=[pl.BlockSpec((lc, M_WIDTH), lambda b, c: (row(b, c), 0)),
                  pl.BlockSpec((lc, M_WIDTH), lambda b, c: (row(b, c), 1)),
                  pl.BlockSpec((lc, M_WIDTH), lambda b, c: (row(b, c), 2)),
                  pl.BlockSpec((lc, M_WIDTH), lambda b, c: (row(b, c), 3)),
                  pl.BlockSpec((lc, LANES), lambda b, c: (row(b, c), 0)),
                  _const_spec((CONV_WIDTH, M_WIDTH)), _const_spec((CONV_WIDTH, M_WIDTH)),
                  _const_spec((1, M_WIDTH)), _const_spec((1, M_WIDTH)),
                  _const_spec((1, LANES)), _const_spec((1, M_WIDTH))],
        out_specs=pl.BlockSpec((lc, M_WIDTH), lambda b, c: (row(b, c), 0)),
        out_shape=jax.ShapeDtypeStruct((t, M_WIDTH), BF16),
        scratch_shapes=[pltpu.VMEM((lc + 8, M_WIDTH), F32), pltpu.VMEM((lc + 8, M_WIDTH), F32),
                        pltpu.VMEM((M_HEADS, M_HEAD_DIM, M_HEAD_DIM), F32),
                        pltpu.VMEM((8, M_HEAD_DIM), F32), pltpu.VMEM((8, LANES), F32)],
        compiler_params=_cparams("parallel", "arbitrary"),
        name="mlstm",
    )(y, y, y, y, misc, cwq, cwk, cbq, cbk, gate_bias, hg)


def _mla_prep_kernel(cq_ref, ckv_ref, misc_ref, cos_ref, sa_ref, sb_ref, cqg_ref, wuq_ref, ckvg_ref, wuk_ref,
                     wuv_ref, qng_ref, kng_ref, q_ref, k_ref, v_ref):
    cqn = _rms(cq_ref[...].astype(F32), cqg_ref[...]).astype(BF16)
    qa = _dot(cqn, wuq_ref[...])
    ckvn = _rms(ckv_ref[...].astype(F32), ckvg_ref[...]).astype(BF16)
    ka = _dot(ckvn, wuk_ref[...])
    va = _dot(ckvn, wuv_ref[...])
    lane = lax.broadcasted_iota(jnp.int32, misc_ref.shape, 1)
    kr = jnp.where((lane >= KR_LANE) & (lane < KR_LANE + QK_ROPE), misc_ref[...], 0.0)
    cos = cos_ref[...]
    sa = sa_ref[...]
    sb = sb_ref[...]

    def norm_rope(xh, g):
        xh = xh * lax.rsqrt(jnp.sum(xh * xh, axis=-1, keepdims=True) * (1.0 / QK_HEAD) + NORM_EPS) * g
        half = QK_ROPE // 2
        return xh * cos + pltpu.roll(xh, HEAD_PAD - half, axis=1) * sa + pltpu.roll(xh, half, axis=1) * sb

    for h in range(A_HEADS):
        sl = slice(h * HEAD_PAD, (h + 1) * HEAD_PAD)
        q_ref[0, h, 0] = (norm_rope(qa[:, sl], qng_ref[...]) * (LOG2_E * QK_HEAD ** -0.5)).T.astype(BF16)
        k_ref[0, h] = norm_rope(ka[:, sl] + kr, kng_ref[...]).astype(BF16)
        v_ref[0, h, 0] = jnp.where(lane < V_HEAD, va[:, sl], 1.0).T.astype(BF16)


def _mla_prep(y, misc, cos_t, sin_a, sin_b, cqg, wuq, ckvg, wuk, wuv, qng, kng, bsz, seq, ts):
    t = bsz * seq
    nt = seq // ts
    row = lambda b, i: (b * nt + i, 0)
    cq_blk = (4 * M_WIDTH) // Q_LORA
    ckv_blk = (4 * M_WIDTH + Q_LORA) // KV_LORA
    hw = A_HEADS * HEAD_PAD
    return pl.pallas_call(
        _mla_prep_kernel,
        grid=(bsz, nt),
        in_specs=[pl.BlockSpec((ts, Q_LORA), lambda b, i: (b * nt + i, cq_blk)),
                  pl.BlockSpec((ts, KV_LORA), lambda b, i: (b * nt + i, ckv_blk)),
                  pl.BlockSpec((ts, LANES), row), pl.BlockSpec((ts, LANES), row),
                  pl.BlockSpec((ts, LANES), row), pl.BlockSpec((ts, LANES), row),
                  _const_spec((1, Q_LORA)), _const_spec((Q_LORA, hw)),
                  _const_spec((1, KV_LORA)), _const_spec((KV_LORA, hw)), _const_spec((KV_LORA, hw)),
                  _const_spec((1, HEAD_PAD)), _const_spec((1, HEAD_PAD))],
        out_specs=[pl.BlockSpec((1, A_HEADS, 1, HEAD_PAD, ts), lambda b, i: (b, 0, i, 0, 0)),
                   pl.BlockSpec((1, A_HEADS, ts, HEAD_PAD), lambda b, i: (b, 0, i, 0)),
                   pl.BlockSpec((1, A_HEADS, 1, HEAD_PAD, ts), lambda b, i: (b, 0, i, 0, 0))],
        out_shape=[jax.ShapeDtypeStruct((bsz, A_HEADS, nt, HEAD_PAD, ts), BF16),
                   jax.ShapeDtypeStruct((bsz, A_HEADS, seq, HEAD_PAD), BF16),
                   jax.ShapeDtypeStruct((bsz, A_HEADS, nt, HEAD_PAD, ts), BF16)],
        compiler_params=_cparams("parallel", "parallel"),
        name="mla_prep",
    )(y, y, misc, cos_t, sin_a, sin_b, cqg, wuq, ckvg, wuk, wuv, qng, kng)


def _attn_kernel(q_ref, k_ref, v_ref, o_ref, *scratch, tq):
    m_refs, acc_refs = scratch[:A_HEADS], scratch[A_HEADS:]
    i = pl.program_id(1)
    key = lax.broadcasted_iota(jnp.int32, (tq, tq), 0)
    qry = lax.broadcasted_iota(jnp.int32, (tq, tq), 1)
    dmask = (key // MLA_CHUNK) <= (qry // MLA_CHUNK)
    for h in range(A_HEADS):
        m_refs[h][...] = jnp.full((1, tq), NEG_BIG, F32)
        acc_refs[h][...] = jnp.zeros((HEAD_PAD, tq), F32)

    def update(kt, masked):
        off = pl.multiple_of(kt * tq, tq)
        scores = [_dot(k_ref[0, h, pl.ds(off, tq), :], q_ref[0, h, 0]) for h in range(A_HEADS)]
        for h in range(A_HEADS):
            st = scores[h]
            if masked:
                st = jnp.where(dmask, st, NEG_BIG)
            m = m_refs[h][...]
            mn = jnp.maximum(m, jnp.max(st, axis=0, keepdims=True))
            p = jnp.exp2(st - mn)
            acc_refs[h][...] = jnp.exp2(m - mn) * acc_refs[h][...] + _dot(v_ref[0, h, kt], p.astype(BF16))
            m_refs[h][...] = mn

    def body(kt, carry):
        update(kt, False)
        return carry

    lax.fori_loop(0, i, body, 0)
    update(i, True)
    for hp in range(A_HEADS // 2):
        a0 = acc_refs[2 * hp][...]
        a1 = acc_refs[2 * hp + 1][...]
        pair = jnp.concatenate([a0[:V_HEAD] / a0[V_HEAD:V_HEAD + 1], a1[:V_HEAD] / a1[V_HEAD:V_HEAD + 1]], axis=0)
        o_ref[:, hp * LANES:(hp + 1) * LANES] = pair.T.astype(BF16)


def _attention(q_t, k, v_t, bsz, seq, tq):
    nq = seq // tq
    return pl.pallas_call(
        functools.partial(_attn_kernel, tq=tq),
        grid=(bsz, nq),
        in_specs=[pl.BlockSpec((1, A_HEADS, 1, HEAD_PAD, tq), lambda b, i: (b, 0, i, 0, 0)),
                  pl.BlockSpec((1, A_HEADS, seq, HEAD_PAD), lambda b, i: (b, 0, 0, 0)),
                  pl.BlockSpec((1, A_HEADS, nq, HEAD_PAD, tq), lambda b, i: (b, 0, 0, 0, 0))],
        out_specs=pl.BlockSpec((tq, A_WIDTH), lambda b, i: (b * nq + i, 0)),
        out_shape=jax.ShapeDtypeStruct((bsz * seq, A_WIDTH), BF16),
        scratch_shapes=([pltpu.VMEM((1, tq), F32)] * A_HEADS + [pltpu.VMEM((HEAD_PAD, tq), F32)] * A_HEADS),
        compiler_params=_cparams("parallel", "arbitrary"),
        name="mla_attention",
    )(q_t, k, v_t)


def _outproj_ffn_kernel(m_ref, a_ref, x_ref, wom_ref, woa_ref, g_ref, w1_ref, w3_ref, w2_ref, o_ref):
    x1 = x_ref[...] + _dot(m_ref[...], wom_ref[...]) + _dot(a_ref[...], woa_ref[...])
    hn = _rms(x1, g_ref[...]).astype(BF16)
    a = _dot(hn, w1_ref[...])
    b = _dot(hn, w3_ref[...])
    o_ref[...] = x1 + _dot((a * _sigmoid(a) * b).astype(BF16), w2_ref[...])


def _outproj_ffn(m_out, a_out, x, wom, woa, g, w1, w3, w2, tm):
    t = x.shape[0]
    return pl.pallas_call(
        _outproj_ffn_kernel,
        grid=(t // tm,),
        in_specs=[pl.BlockSpec((tm, M_WIDTH), lambda i: (i, 0)),
                  pl.BlockSpec((tm, A_WIDTH), lambda i: (i, 0)),
                  pl.BlockSpec((tm, D_MODEL), lambda i: (i, 0)),
                  _const_spec((M_WIDTH, D_MODEL)), _const_spec((A_WIDTH, D_MODEL)), _const_spec((1, D_MODEL)),
                  _const_spec((D_MODEL, D_FF)), _const_spec((D_MODEL, D_FF)), _const_spec((D_FF, D_MODEL))],
        out_specs=pl.BlockSpec((tm, D_MODEL), lambda i: (i, 0)),
        out_shape=jax.ShapeDtypeStruct((t, D_MODEL), F32),
        compiler_params=_cparams("parallel"),
        name="outproj_ffn",
    )(m_out, a_out, x, wom, woa, g, w1, w3, w2)


S5_CHUNK = 16
S5_ROW = S5_CHUNK * S5_GROUP


GROUPS_PER_TILE = LANES // S5_GROUP
RELAYOUT_ROWS = 32


def _lane_group_masks(rows):
    grp = lax.broadcasted_iota(jnp.int32, (rows, LANES), 1) // S5_GROUP
    return [grp == g for g in range(GROUPS_PER_TILE)]


def _to_group_major(vs, masks):
    n = GROUPS_PER_TILE
    rolled = []
    for r in range(n):
        d = vs[r % n]
        for g in range(1, n):
            d = jnp.where(masks[g], vs[(g + r) % n], d)
        rolled.append(d if r == 0 else pltpu.roll(d, r * S5_GROUP, axis=1))
    outs = []
    for grp in range(n):
        o = rolled[(-grp) % n]
        for w in range(1, n):
            o = jnp.where(masks[w], rolled[(w - grp) % n], o)
        outs.append(o)
    return outs


def _to_token_major(outs, masks):
    n = GROUPS_PER_TILE
    rolled = []
    for r in range(n):
        d = outs[(-r) % n]
        for w in range(1, n):
            d = jnp.where(masks[w], outs[(w - r) % n], d)
        rolled.append(d if r == 0 else pltpu.roll(d, LANES - r * S5_GROUP, axis=1))
    vs = []
    for w in range(n):
        v = rolled[w]
        for g in range(1, n):
            v = jnp.where(masks[g], rolled[(w - g) % n], v)
        vs.append(v)
    return vs


def _relayout_steps(tile_rows):
    nrow = tile_rows // S5_CHUNK
    return [(j, k, rg) for j in range(D_MODEL // LANES) for k in range(S5_CHUNK // GROUPS_PER_TILE)
            for rg in range(nrow // RELAYOUT_ROWS)]


def _norm_group_major_kernel(x_ref, g_ref, u_ref, un_scr):
    un = _rms(x_ref[...], g_ref[...])
    for j in range(D_MODEL // LANES):
        un_scr[j] = un[:, j * LANES:(j + 1) * LANES]
    masks = _lane_group_masks(RELAYOUT_ROWS)
    n = GROUPS_PER_TILE
    for j, k, rg in _relayout_steps(x_ref.shape[0]):
        rows = slice(rg * RELAYOUT_ROWS, (rg + 1) * RELAYOUT_ROWS)
        first = rg * RELAYOUT_ROWS * S5_CHUNK + k * n
        vs = [un_scr[j, pl.ds(first + w, RELAYOUT_ROWS, stride=S5_CHUNK), :] for w in range(n)]
        outs = _to_group_major(vs, masks)
        for grp in range(n):
            u_ref[j * n + grp, rows, k * LANES:(k + 1) * LANES] = outs[grp].astype(BF16)


def _norm_group_major(x, g, tm):
    t = x.shape[0]
    nrow = tm // S5_CHUNK
    return pl.pallas_call(
        _norm_group_major_kernel,
        grid=(t // tm,),
        in_specs=[pl.BlockSpec((tm, D_MODEL), lambda i: (i, 0)), _const_spec((1, D_MODEL))],
        out_specs=pl.BlockSpec((S5_GROUPS, nrow, S5_ROW), lambda i: (0, i, 0)),
        out_shape=jax.ShapeDtypeStruct((S5_GROUPS, t // S5_CHUNK, S5_ROW), BF16),
        scratch_shapes=[pltpu.VMEM((D_MODEL // LANES, tm, LANES), F32)],
        compiler_params=_cparams("parallel"),
        name="norm_group_major",
    )(x, g)


def _gelu_tanh(x):
    return 0.5 * x * (1.0 + jnp.tanh(math.sqrt(2.0 / math.pi) * (x + 0.044715 * (x * x * x))))


def _s5_kernel(u_ref, w_ref, cst_ref, a1_ref, a2_ref, dk_ref, z_ref, vv_scr, xp_scr, *, nch, bsz):
    two_p = 2 * S5_STATE
    u = u_ref[0]
    r = _dot(u, w_ref[0])
    vv_scr[0] = r[:, S5_ROW:S5_ROW + two_p]
    vv_scr[1] = r[:, S5_ROW + two_p:]
    a1 = a1_ref[0]
    a2 = a2_ref[0]

    def body(c, carry):
        p, q = carry
        rows = pl.ds(c, bsz, stride=nch)
        xp_scr[rows, :] = p
        pn = a1 * p + a2 * q + vv_scr[0, rows, :]
        qn = a1 * q - a2 * p + vv_scr[1, rows, :]
        return pn, qn

    zero = jnp.zeros((bsz, two_p), F32)
    lax.fori_loop(0, nch, body, (zero, zero))
    y = r[:, :S5_ROW] + _dot(xp_scr[...].astype(BF16), cst_ref[0]) + dk_ref[0] * u.astype(F32)
    z_ref[0] = _gelu_tanh(y).astype(BF16)


def _s5(u_g, w_all, cst, a1, a2, dk, nch, bsz):
    g, n, _ = u_g.shape
    two_p = 2 * S5_STATE
    wcols = S5_ROW + 2 * two_p
    return pl.pallas_call(
        functools.partial(_s5_kernel, nch=nch, bsz=bsz),
        grid=(g,),
        in_specs=[pl.BlockSpec((1, n, S5_ROW), lambda i: (i, 0, 0)),
                  pl.BlockSpec((1, S5_ROW, wcols), lambda i: (i, 0, 0)),
                  pl.BlockSpec((1, two_p, S5_ROW), lambda i: (i, 0, 0)),
                  pl.BlockSpec((1, 1, two_p), lambda i: (i, 0, 0)),
                  pl.BlockSpec((1, 1, two_p), lambda i: (i, 0, 0)),
                  pl.BlockSpec((1, 1, S5_ROW), lambda i: (i, 0, 0))],
        out_specs=pl.BlockSpec((1, n, S5_ROW), lambda i: (i, 0, 0)),
        out_shape=jax.ShapeDtypeStruct((g, n, S5_ROW), BF16),
        scratch_shapes=[pltpu.VMEM((2, n, two_p), F32), pltpu.VMEM((n, two_p), F32)],
        compiler_params=_cparams("parallel"),
        name="s5",
    )(u_g, w_all, cst, a1, a2, dk)


def _s5_weights(lam_re, lam_im, log_dt, b_re, b_im, c_re, c_im, d_skip):
    lc, h, p = S5_CHUNK, S5_GROUP, S5_STATE
    lam = lax.complex(jnp.minimum(lam_re.astype(F32), -1e-4), lam_im.astype(F32))
    dt = jnp.exp(log_dt.astype(F32))[:, None]
    ldt = lam * dt
    a_bar = jnp.exp(ldt)
    b_bar = ((a_bar - 1.0) / lam)[..., None] * lax.complex(b_re.astype(F32), b_im.astype(F32))
    c_mat = lax.complex(c_re.astype(F32), c_im.astype(F32))
    steps = jnp.arange(lc + 1, dtype=F32)
    apow = jnp.exp(ldt[None] * steps[:, None, None].astype(ldt.dtype))
    g = lam.shape[0]
    ker = jnp.einsum('gip,kgp,gpj->gkij', c_mat, apow[:lc], b_bar).real
    lag = np.arange(lc)[None, :] - np.arange(lc)[:, None]
    toe = jnp.where((lag >= 0)[None, :, :, None, None], ker[:, np.clip(lag, 0, lc - 1)], 0.0)
    toe = toe.transpose(0, 1, 4, 2, 3).reshape(g, lc * h, lc * h)
    wst = apow[:lc][::-1].transpose(1, 0, 2)[:, :, None, :] * jnp.swapaxes(b_bar, 1, 2)[:, None, :, :]
    wst = wst.reshape(g, lc * h, p)
    w_all = jnp.concatenate([toe, wst.real, wst.imag, wst.imag, wst.real], axis=-1).astype(BF16)
    ca = c_mat[:, None, :, :] * apow[1:lc + 1].transpose(1, 0, 2)[:, :, None, :]
    ca = ca.reshape(g, lc * h, p)
    cst = jnp.concatenate([ca.real, -ca.imag], axis=-1).transpose(0, 2, 1).astype(BF16)
    al = apow[lc]
    a1 = jnp.concatenate([al.real, al.real], axis=-1)[:, None, :]
    a2 = jnp.concatenate([-al.imag, al.imag], axis=-1)[:, None, :]
    dk = jnp.tile(d_skip.astype(F32).reshape(g, 1, h), (1, lc, 1)).reshape(g, 1, lc * h)
    return w_all, cst, a1, a2, dk


def _glu_kernel(z_ref, x_ref, wv_ref, wg_ref, g_ref, x2_ref, xn_ref, zt_scr):
    masks = _lane_group_masks(RELAYOUT_ROWS)
    n = GROUPS_PER_TILE
    for j, k, rg in _relayout_steps(x_ref.shape[0]):
        rows = slice(rg * RELAYOUT_ROWS, (rg + 1) * RELAYOUT_ROWS)
        outs = [z_ref[j * n + grp, rows, k * LANES:(k + 1) * LANES].astype(F32) for grp in range(n)]
        vs = _to_token_major(outs, masks)
        first = rg * RELAYOUT_ROWS * S5_CHUNK + k * n
        for w in range(n):
            zt_scr[j, pl.ds(first + w, RELAYOUT_ROWS, stride=S5_CHUNK), :] = vs[w]
    z = jnp.concatenate([zt_scr[j] for j in range(D_MODEL // LANES)], axis=1).astype(BF16)
    x2 = x_ref[...] + _dot(z, wv_ref[...]) * _sigmoid(_dot(z, wg_ref[...]))
    x2_ref[...] = x2
    xn_ref[...] = _rms(x2, g_ref[...]).astype(BF16)


def _glu(z_g, x, wv, wg, g, tm):
    t = x.shape[0]
    nrow = tm // S5_CHUNK
    return pl.pallas_call(
        _glu_kernel,
        grid=(t // tm,),
        in_specs=[pl.BlockSpec((S5_GROUPS, nrow, S5_ROW), lambda i: (0, i, 0)),
                  pl.BlockSpec((tm, D_MODEL), lambda i: (i, 0)),
                  _const_spec((D_MODEL, D_MODEL)), _const_spec((D_MODEL, D_MODEL)), _const_spec((1, D_MODEL))],
        out_specs=[pl.BlockSpec((tm, D_MODEL), lambda i: (i, 0)), pl.BlockSpec((tm, D_MODEL), lambda i: (i, 0))],
        out_shape=[jax.ShapeDtypeStruct((t, D_MODEL), F32), jax.ShapeDtypeStruct((t, D_MODEL), BF16)],
        scratch_shapes=[pltpu.VMEM((D_MODEL // LANES, tm, LANES), F32)],
        compiler_params=_cparams("parallel"),
        name="glu",
    )(z_g, x, wv, wg, g)


def _router_kernel(xn_ref, wh_ref, wl_ref, pos_ref, gate_ref, posc_ref, cnt_ref, *, tb):
    xn = xn_ref[...]
    lg = (_dot_nt(wh_ref[...], xn) + _dot_nt(wl_ref[...], xn))[0:N_EXPERTS]
    eidx = lax.broadcasted_iota(jnp.int32, (N_EXPERTS, tb), 0)
    m1 = jnp.max(lg, axis=0, keepdims=True)
    i1 = jnp.min(jnp.where(lg == m1, eidx, N_EXPERTS), axis=0, keepdims=True)
    rest = jnp.where(eidx == i1, -jnp.inf, lg)
    m2 = jnp.max(rest, axis=0, keepdims=True)
    i2 = jnp.min(jnp.where(rest == m2, eidx, N_EXPERTS), axis=0, keepdims=True)
    e21 = jnp.exp(m2 - m1)
    p1 = 1.0 / (1.0 + e21)
    sel1 = eidx == i1
    sel2 = eidx == i2
    sel = sel1 | sel2
    gate_ref[...] = jnp.where(sel1, p1, jnp.where(sel2, e21 * p1, 0.0))
    onehot = jnp.concatenate([sel.astype(F32), jnp.zeros((N_EXPERTS, tb), F32)], axis=0).astype(BF16)
    srow = lax.broadcasted_iota(jnp.int32, (tb, tb), 0)
    scol = lax.broadcasted_iota(jnp.int32, (tb, tb), 1)
    rank = _dot(onehot, (srow < scol).astype(BF16))[0:N_EXPERTS]
    pos = jnp.where(sel, rank, -1.0)
    pos_ref[...] = pos
    posc_ref[...] = jnp.concatenate([pos, jnp.full((LANES - N_EXPERTS, tb), -1.0, F32)], axis=0).T
    cnt = jnp.sum(sel.astype(F32), axis=1, keepdims=True)
    cnt_ref[0] = jnp.broadcast_to(cnt, (N_EXPERTS, LANES))


def _router(xn, wr_hi, wr_lo, tb):
    t = xn.shape[0]
    nb = t // tb
    return pl.pallas_call(
        functools.partial(_router_kernel, tb=tb),
        grid=(nb,),
        in_specs=[pl.BlockSpec((tb, D_MODEL), lambda i: (i, 0)),
                  _const_spec((2 * N_EXPERTS, D_MODEL)), _const_spec((2 * N_EXPERTS, D_MODEL))],
        out_specs=[pl.BlockSpec((N_EXPERTS, tb), lambda i: (0, i)), pl.BlockSpec((N_EXPERTS, tb), lambda i: (0, i)),
                   pl.BlockSpec((tb, LANES), lambda i: (i, 0)),
                   pl.BlockSpec((1, N_EXPERTS, LANES), lambda i: (i, 0, 0))],
        out_shape=[jax.ShapeDtypeStruct((N_EXPERTS, t), F32), jax.ShapeDtypeStruct((N_EXPERTS, t), F32),
                   jax.ShapeDtypeStruct((t, LANES), F32), jax.ShapeDtypeStruct((nb, N_EXPERTS, LANES), F32)],
        compiler_params=_cparams("parallel"),
        name="router",
    )(xn, wr_hi, wr_lo)


MOE_UNIT = 128
MOE_FF_SPLIT = 2


def _moe_kernel(cnt_ref, xn_ref, x_ref, pos_ref, gate_ref, posc_ref, w1_ref, w3_ref, w2_ref, o_ref, xe_scr, ye_scr,
                *, tb):
    b = pl.program_id(0)
    e = pl.program_id(1)
    f = pl.program_id(2)
    unit = MOE_UNIT
    n_unit = (cnt_ref[b * N_EXPERTS + e] + unit - 1) // unit
    n_pair = n_unit // 2
    odd = n_unit % 2 == 1
    tail_base = n_pair * (2 * unit)

    @pl.when((e == 0) & (f == 0))
    def _():
        o_ref[...] = x_ref[...]

    prow = pos_ref[pl.ds(e, 1), :]
    grow = gate_ref[pl.ds(e, 1), :]

    def ffn_chunk(base, m):
        @pl.when(f == 0)
        def _():
            slot = lax.broadcasted_iota(jnp.int32, (m, tb), 0).astype(F32) + base.astype(F32)
            xe_scr[pl.ds(base, m), :] = _dot((prow == slot).astype(BF16), xn_ref[...]).astype(BF16)

        xe = xe_scr[pl.ds(base, m), :]
        a = _dot(xe, w1_ref[0])
        g = (a * _sigmoid(a) * _dot(xe, w3_ref[0])).astype(BF16)
        y = _dot(g, w2_ref[0])

        @pl.when(f == 0)
        def _():
            ye_scr[pl.ds(base, m), :] = y

        @pl.when(f != 0)
        def _():
            ye_scr[pl.ds(base, m), :] += y

    def ffn_pair(r, carry):
        ffn_chunk(pl.multiple_of(r * (2 * unit), 2 * unit), 2 * unit)
        return carry

    lax.fori_loop(0, n_pair, ffn_pair, 0)

    @pl.when(odd)
    def _():
        ffn_chunk(pl.multiple_of(tail_base, 2 * unit), unit)

    @pl.when(f == MOE_FF_SPLIT - 1)
    def _():
        lane = lax.broadcasted_iota(jnp.int32, (tb, LANES), 1)
        pcol = jnp.sum(jnp.where(lane == e, posc_ref[...], 0.0), axis=-1, keepdims=True)

        def scatter_chunk(base, m):
            slot_r = lax.broadcasted_iota(jnp.int32, (m, tb), 0).astype(F32) + base.astype(F32)
            gs = jnp.sum(jnp.where(prow == slot_r, grow, 0.0), axis=-1, keepdims=True)
            yg = (ye_scr[pl.ds(base, m), :] * gs).astype(BF16)
            slot_c = lax.broadcasted_iota(jnp.int32, (tb, m), 1).astype(F32) + base.astype(F32)
            o_ref[...] += _dot((pcol == slot_c).astype(BF16), yg)

        def scatter_pair(r, carry):
            scatter_chunk(pl.multiple_of(r * (2 * unit), 2 * unit), 2 * unit)
            return carry

        lax.fori_loop(0, n_pair, scatter_pair, 0)

        @pl.when(odd)
        def _():
            scatter_chunk(pl.multiple_of(tail_base, 2 * unit), unit)


def _moe(cnt, xn, x, pos, gate, posc, w1, w3, w2, tb):
    t = x.shape[0]
    nb = t // tb
    fs = D_FF_EXPERT // MOE_FF_SPLIT
    grid_spec = pltpu.PrefetchScalarGridSpec(
        num_scalar_prefetch=1,
        grid=(nb, N_EXPERTS, MOE_FF_SPLIT),
        in_specs=[pl.BlockSpec((tb, D_MODEL), lambda b, e, f, c: (b, 0)),
                  pl.BlockSpec((tb, D_MODEL), lambda b, e, f, c: (b, 0), pipeline_mode=pl.Buffered(1)),
                  pl.BlockSpec((N_EXPERTS, tb), lambda b, e, f, c: (0, b)),
                  pl.BlockSpec((N_EXPERTS, tb), lambda b, e, f, c: (0, b)),
                  pl.BlockSpec((tb, LANES), lambda b, e, f, c: (b, 0)),
                  pl.BlockSpec((1, D_MODEL, fs), lambda b, e, f, c: (e, 0, f)),
                  pl.BlockSpec((1, D_MODEL, fs), lambda b, e, f, c: (e, 0, f)),
                  pl.BlockSpec((1, fs, D_MODEL), lambda b, e, f, c: (e, f, 0))],
        out_specs=pl.BlockSpec((tb, D_MODEL), lambda b, e, f, c: (b, 0)),
        scratch_shapes=[pltpu.VMEM((tb, D_MODEL), BF16), pltpu.VMEM((tb, D_MODEL), F32)],
    )
    return pl.pallas_call(
        functools.partial(_moe_kernel, tb=tb),
        grid_spec=grid_spec,
        out_shape=jax.ShapeDtypeStruct((t, D_MODEL), F32),
        compiler_params=_cparams("parallel", "arbitrary", "arbitrary"),
        name="moe",
    )(cnt, xn, x, pos, gate, posc, w1, w3, w2)


def _rope_tables(positions):
    half = QK_ROPE // 2
    inv_freq = ROPE_THETA ** (-jnp.arange(0, QK_ROPE, 2, dtype=F32) / QK_ROPE)
    ang = positions.astype(F32).reshape(-1, 1) * inv_freq
    cos, sin = jnp.cos(ang), jnp.sin(ang)
    t = ang.shape[0]
    ones_lo = jnp.ones((t, QK_NOPE), F32)
    zeros_lo = jnp.zeros((t, QK_NOPE), F32)
    zeros_h = jnp.zeros((t, half), F32)
    tail1 = jnp.ones((t, HEAD_PAD - QK_HEAD), F32)
    tail0 = jnp.zeros((t, HEAD_PAD - QK_HEAD), F32)
    cos_t = jnp.concatenate([ones_lo, cos, cos, tail1], axis=-1)
    sin_a = jnp.concatenate([zeros_lo, -sin, zeros_h, tail0], axis=-1)
    sin_b = jnp.concatenate([zeros_lo, zeros_h, sin, tail0], axis=-1)
    return cos_t, sin_a, sin_b


def _pad_heads(w, per_head_in, keep, heads):
    k = w.shape[0]
    w = w.reshape(k, heads, per_head_in)[:, :, :keep]
    return jnp.pad(w, ((0, 0), (0, 0), (0, HEAD_PAD - keep))).reshape(k, heads * HEAD_PAD)


def _even_layer(x, rope, bsz, seq, norm1_g, w_in, conv_w, conv_b, b_igate, b_fgate, mh_norm_g, cq_norm_g, w_uq,
                ckv_norm_g, w_ukv, qn_g, kn_g, w_o, norm2_g, ffn_w1, ffn_w3, ffn_w2):
    row = lambda v: v.astype(F32).reshape(1, -1)
    mw = M_WIDTH
    o_ig, o_fg, o_og = 3 * mw, 3 * mw + M_HEADS, 3 * mw + 2 * M_HEADS
    o_cq = o_og + mw
    o_ckv, o_kr = o_cq + Q_LORA, o_cq + Q_LORA + KV_LORA
    misc_w = jnp.zeros((D_MODEL, LANES), F32)
    misc_w = misc_w.at[:, IG_LANE:IG_LANE + M_HEADS].set(w_in[:, o_ig:o_fg])
    misc_w = misc_w.at[:, FG_LANE:FG_LANE + M_HEADS].set(w_in[:, o_fg:o_og])
    misc_w = misc_w.at[:, KR_LANE:KR_LANE + QK_ROPE].set(w_in[:, o_kr:o_kr + QK_ROPE])
    w_packed = jnp.concatenate([w_in[:, :3 * mw], w_in[:, o_og:o_cq], w_in[:, o_cq:o_ckv], w_in[:, o_ckv:o_kr],
                                misc_w], axis=-1).astype(BF16)
    gate_bias = jnp.zeros((1, LANES), F32)
    gate_bias = gate_bias.at[0, IG_LANE:IG_LANE + M_HEADS].set(b_igate.astype(F32))
    gate_bias = gate_bias.at[0, FG_LANE:FG_LANE + M_HEADS].set(b_fgate.astype(F32))

    tm = min(512, x.shape[0])
    y, misc = _inproj(x, row(norm1_g), w_packed, tm)

    lc = min(256, seq)
    m_out = _mlstm(y, misc, conv_w[:, :mw].astype(F32), conv_w[:, mw:].astype(F32), row(conv_b[:mw]),
                   row(conv_b[mw:]), gate_bias, row(mh_norm_g), bsz, seq, lc)

    wuq = _pad_heads(w_uq, QK_HEAD, QK_HEAD, A_HEADS).astype(BF16)
    wuk = _pad_heads(w_ukv, QK_NOPE + V_HEAD, QK_NOPE, A_HEADS).astype(BF16)
    wuv = _pad_heads(w_ukv.reshape(KV_LORA, A_HEADS, QK_NOPE + V_HEAD)[:, :, QK_NOPE:].reshape(KV_LORA, A_WIDTH),
                     V_HEAD, V_HEAD, A_HEADS).astype(BF16)
    pad_g = lambda g: jnp.pad(g.astype(F32), (0, HEAD_PAD - QK_HEAD)).reshape(1, HEAD_PAD)
    ts = min(256, seq)
    q, k, v = _mla_prep(y, misc, rope[0], rope[1], rope[2], row(cq_norm_g), wuq, row(ckv_norm_g), wuk, wuv,
                        pad_g(qn_g), pad_g(kn_g), bsz, seq, ts)
    a_out = _attention(q, k, v, bsz, seq, ts)

    return _outproj_ffn(m_out, a_out, x, w_o[:mw].astype(BF16), w_o[mw:].astype(BF16), row(norm2_g),
                        ffn_w1.astype(BF16), ffn_w3.astype(BF16), ffn_w2.astype(BF16), tm)


def _odd_layer(x, bsz, seq, norm1_g, lam_re, lam_im, log_dt, b_re, b_im, c_re, c_im, d_skip, glu_wv, glu_wg,
               norm2_g, router_w, moe_w1, moe_w3, moe_w2):
    row = lambda v: v.astype(F32).reshape(1, -1)
    t = x.shape[0]
    tm = min(512, t)
    nch = seq // S5_CHUNK
    tr = min(1024, seq)
    u_g = _norm_group_major(x, row(norm1_g), tr)
    w_all, cst, a1, a2, dk = _s5_weights(lam_re, lam_im, log_dt, b_re, b_im, c_re, c_im, d_skip)
    z_g = _s5(u_g, w_all, cst, a1, a2, dk, nch, bsz)
    x2, xn = _glu(z_g, x, glu_wv.astype(BF16), glu_wg.astype(BF16), row(norm2_g), tr)

    tb = min(1024, t)
    wr = jnp.pad(router_w.astype(F32).T, ((0, N_EXPERTS), (0, 0)))
    wr_hi = wr.astype(BF16)
    wr_lo = (wr - wr_hi.astype(F32)).astype(BF16)
    pos, gate, posc, cnt = _router(xn, wr_hi, wr_lo, tb)
    cnt = cnt[:, :, 0].astype(jnp.int32).reshape(-1)
    return _moe(cnt, xn, x2, pos, gate, posc, moe_w1.astype(BF16), moe_w3.astype(BF16), moe_w2.astype(BF16), tb)


def kernel(x, positions, l0_norm1_g, l0_w_in, l0_conv_w, l0_conv_b, l0_b_igate, l0_b_fgate, l0_mh_norm_g, l0_cq_norm_g, l0_w_uq, l0_ckv_norm_g, l0_w_ukv, l0_qn_g, l0_kn_g, l0_w_o, l0_norm2_g, l0_ffn_w1, l0_ffn_w3, l0_ffn_w2, l1_norm1_g, l1_lam_re, l1_lam_im, l1_log_dt, l1_b_re, l1_b_im, l1_c_re, l1_c_im, l1_d_skip, l1_glu_wv, l1_glu_wg, l1_norm2_g, l1_router_w, l1_moe_w1, l1_moe_w3, l1_moe_w2, l2_norm1_g, l2_w_in, l2_conv_w, l2_conv_b, l2_b_igate, l2_b_fgate, l2_mh_norm_g, l2_cq_norm_g, l2_w_uq, l2_ckv_norm_g, l2_w_ukv, l2_qn_g, l2_kn_g, l2_w_o, l2_norm2_g, l2_ffn_w1, l2_ffn_w3, l2_ffn_w2, l3_norm1_g, l3_lam_re, l3_lam_im, l3_log_dt, l3_b_re, l3_b_im, l3_c_re, l3_c_im, l3_d_skip, l3_glu_wv, l3_glu_wg, l3_norm2_g, l3_router_w, l3_moe_w1, l3_moe_w3, l3_moe_w2):
    bsz, seq, _ = x.shape
    rope = _rope_tables(positions)
    h = x.reshape(bsz * seq, D_MODEL)
    h = _even_layer(h, rope, bsz, seq, l0_norm1_g, l0_w_in, l0_conv_w, l0_conv_b, l0_b_igate, l0_b_fgate,
                    l0_mh_norm_g, l0_cq_norm_g, l0_w_uq, l0_ckv_norm_g, l0_w_ukv, l0_qn_g, l0_kn_g, l0_w_o,
                    l0_norm2_g, l0_ffn_w1, l0_ffn_w3, l0_ffn_w2)
    h = _odd_layer(h, bsz, seq, l1_norm1_g, l1_lam_re, l1_lam_im, l1_log_dt, l1_b_re, l1_b_im, l1_c_re, l1_c_im,
                   l1_d_skip, l1_glu_wv, l1_glu_wg, l1_norm2_g, l1_router_w, l1_moe_w1, l1_moe_w3, l1_moe_w2)
    h = _even_layer(h, rope, bsz, seq, l2_norm1_g, l2_w_in, l2_conv_w, l2_conv_b, l2_b_igate, l2_b_fgate,
                    l2_mh_norm_g, l2_cq_norm_g, l2_w_uq, l2_ckv_norm_g, l2_w_ukv, l2_qn_g, l2_kn_g, l2_w_o,
                    l2_norm2_g, l2_ffn_w1, l2_ffn_w3, l2_ffn_w2)
    h = _odd_layer(h, bsz, seq, l3_norm1_g, l3_lam_re, l3_lam_im, l3_log_dt, l3_b_re, l3_b_im, l3_c_re, l3_c_im,
                   l3_d_skip, l3_glu_wv, l3_glu_wg, l3_norm2_g, l3_router_w, l3_moe_w1, l3_moe_w3, l3_moe_w2)
    return h.reshape(bsz, seq, D_MODEL)
```

```python
import functools
import math

import jax
import jax.numpy as jnp
import numpy as np
from jax import lax
from jax.experimental import pallas as pl
from jax.experimental.pallas import tpu as pltpu

F32 = jnp.float32
BF16 = jnp.bfloat16

D_MODEL = 1024
NORM_EPS = 1e-6
M_HEADS = 4
M_HEAD_DIM = 128
M_WIDTH = M_HEADS * M_HEAD_DIM
CONV_WIDTH = 4
A_HEADS = 8
Q_LORA = 256
KV_LORA = 128
QK_NOPE = 64
QK_ROPE = 32
QK_HEAD = QK_NOPE + QK_ROPE
V_HEAD = 64
A_WIDTH = A_HEADS * V_HEAD
ROPE_THETA = 10000.0
MLA_CHUNK = 64
S5_GROUP = 16
S5_GROUPS = D_MODEL // S5_GROUP
S5_STATE = 64
D_FF = 2816
N_EXPERTS = 8
D_FF_EXPERT = 3584

LANES = 128
HEAD_PAD = 128
Y_COLS = 4 * M_WIDTH + Q_LORA + KV_LORA
IG_LANE = 0
FG_LANE = 8
KR_LANE = QK_NOPE

VMEM_LIMIT = 56 * 1024 * 1024
NEG_BIG = -1e30
LOG2_E = math.log2(math.e)


def _cparams(*sem):
    return pltpu.CompilerParams(dimension_semantics=sem, vmem_limit_bytes=VMEM_LIMIT)


def _rms(x, g):
    return x * lax.rsqrt(jnp.mean(x * x, axis=-1, keepdims=True) + NORM_EPS) * g


def _sigmoid(x):
    return 1.0 / (1.0 + jnp.exp(-x))


def _dot(a, b):
    return jnp.dot(a, b, preferred_element_type=F32)


def _dot_nt(a, b):
    return lax.dot_general(a, b, (((1,), (1,)), ((), ())), preferred_element_type=F32)


def _dot_tn(a, b):
    return lax.dot_general(a, b, (((0,), (0,)), ((), ())), preferred_element_type=F32)


def _const_spec(shape):
    nd = len(shape)
    return pl.BlockSpec(shape, lambda *_: (0,) * nd, pipeline_mode=pl.Buffered(1))


def _inproj_kernel(x_ref, g_ref, w_ref, y_ref, misc_ref):
    xn = _rms(x_ref[...], g_ref[...]).astype(BF16)
    y = _dot(xn, w_ref[...])
    y_ref[...] = y[:, :Y_COLS].astype(BF16)
    misc_ref[...] = y[:, Y_COLS:]


def _inproj(x, g, w_packed, tm):
    t = x.shape[0]
    return pl.pallas_call(
        _inproj_kernel,
        grid=(t // tm,),
        in_specs=[pl.BlockSpec((tm, D_MODEL), lambda i: (i, 0)),
                  _const_spec((1, D_MODEL)),
                  _const_spec((D_MODEL, Y_COLS + LANES))],
        out_specs=[pl.BlockSpec((tm, Y_COLS), lambda i: (i, 0)),
                   pl.BlockSpec((tm, LANES), lambda i: (i, 0))],
        out_shape=[jax.ShapeDtypeStruct((t, Y_COLS), BF16),
                   jax.ShapeDtypeStruct((t, LANES), F32)],
        compiler_params=_cparams("parallel"),
        name="inproj",
    )(x, g, w_packed)


def _cummax_lanes(r, width):
    lane = lax.broadcasted_iota(jnp.int32, r.shape, 1)
    s = 1
    while s < width:
        shifted = pltpu.roll(r, s, axis=1)
        r = jnp.where(lane >= s, jnp.maximum(r, shifted), r)
        s *= 2
    return r


def _mlstm_kernel(q_ref, k_ref, v_ref, og_ref, misc_ref, cwq_ref, cwk_ref, cbq_ref, cbk_ref, gb_ref, hg_ref,
                  out_ref, qpad, kpad, c_scr, n_scr, m_scr, *, lc):
    @pl.when(pl.program_id(1) == 0)
    def _():
        qpad[0:8, :] = jnp.zeros((8, M_WIDTH), F32)
        kpad[0:8, :] = jnp.zeros((8, M_WIDTH), F32)
        c_scr[...] = jnp.zeros_like(c_scr)
        n_scr[...] = jnp.zeros_like(n_scr)
        m_scr[...] = jnp.zeros_like(m_scr)

    qpad[8:8 + lc, :] = q_ref[...].astype(F32)
    kpad[8:8 + lc, :] = k_ref[...].astype(F32)

    def conv_silu(pad, cw_ref, cb_ref):
        acc = jnp.broadcast_to(cb_ref[...], (lc, M_WIDTH))
        for j in range(CONV_WIDTH):
            off = 8 - (CONV_WIDTH - 1) + j
            acc = acc + pad[off:off + lc, :] * cw_ref[j:j + 1, :]
        return acc * _sigmoid(acc)

    qc = conv_silu(qpad, cwq_ref, cbq_ref) * (M_HEAD_DIM ** -0.5)
    kc = conv_silu(kpad, cwk_ref, cbk_ref)
    qpad[0:8, :] = qpad[lc:lc + 8, :]
    kpad[0:8, :] = kpad[lc:lc + 8, :]

    gt = (misc_ref[...] + gb_ref[...]).T
    ig = gt[IG_LANE:IG_LANE + 8, :]
    fpre = gt[FG_LANE:FG_LANE + 8, :]
    lf = jnp.minimum(fpre, 0.0) - jnp.log(1.0 + jnp.exp(-jnp.abs(fpre)))
    hi = lf.astype(BF16).astype(F32)
    r1 = lf - hi
    mid = r1.astype(BF16).astype(F32)
    lo = r1 - mid
    stack = jnp.concatenate([hi, mid, lo, jnp.zeros_like(lo)], axis=0).astype(BF16)
    srow = lax.broadcasted_iota(jnp.int32, (lc, lc), 0)
    scol = lax.broadcasted_iota(jnp.int32, (lc, lc), 1)
    upper = (srow <= scol).astype(BF16)
    cs = _dot(stack, upper)
    b = cs[0:8] + cs[8:16] + cs[16:24]

    r = ig - b
    cm = _cummax_lanes(r, lc)
    mprev = m_scr[:, 0:1]
    mx = jnp.maximum(mprev, cm)
    w_inter = jnp.exp(mprev - mx)
    e_neg = jnp.exp(-(b + mx))
    total = b[:, lc - 1:lc]
    mnew = total + mx[:, lc - 1:lc]
    a_prev = jnp.exp(total + mprev - mnew)
    w_in = jnp.exp(total + r - mnew)
    m_scr[...] = jnp.broadcast_to(mnew, (8, LANES))

    cols = jnp.concatenate([-mx, w_inter, e_neg, w_in, jnp.zeros((LANES - 32, lc), F32)], axis=0).T
    tril = scol <= srow

    heads = [slice(h * M_HEAD_DIM, (h + 1) * M_HEAD_DIM) for h in range(M_HEADS)]
    qbs = [qc[:, sl].astype(BF16) for sl in heads]
    qk = [_dot_nt(qbs[h], kc[:, heads[h]].astype(BF16)) for h in range(M_HEADS)]
    qcs = [_dot(qbs[h], c_scr[h].astype(BF16)) for h in range(M_HEADS)]
    for h in range(M_HEADS):
        sl = heads[h]
        qh = qc[:, sl]
        kh = kc[:, sl]
        vh = v_ref[:, sl]
        u_col = cols[:, h:h + 1]
        wi_col = cols[:, 8 + h:9 + h]
        en_col = cols[:, 16 + h:17 + h]
        win_col = cols[:, 24 + h:25 + h]
        dmat = jnp.where(tril, u_col + r[h:h + 1, :], NEG_BIG)
        p = jnp.exp(dmat) * qk[h]
        n_h = n_scr[h:h + 1, :]
        num = wi_col * qcs[h] + _dot(p.astype(BF16), vh)
        den = wi_col * jnp.sum(qh * n_h, axis=-1, keepdims=True) + jnp.sum(p, axis=-1, keepdims=True)
        ht = num / jnp.maximum(jnp.abs(den), en_col)
        kw = kh * win_col
        ah = a_prev[h:h + 1, :]
        c_scr[h] = ah * c_scr[h] + _dot_tn(kw.astype(BF16), vh)
        n_scr[h:h + 1, :] = ah * n_h + jnp.sum(kw, axis=0, keepdims=True)
        hn = _rms(ht, hg_ref[:, sl])
        out_ref[:, sl] = (_sigmoid(og_ref[:, sl].astype(F32)) * hn).astype(BF16)


def _mlstm(y, misc, cwq, cwk, cbq, cbk, gate_bias, hg, bsz, seq, lc):
    t = bsz * seq
    nch = seq // lc
    row = lambda b, c: b * nch + c
    return pl.pallas_call(
        functools.partial(_mlstm_kernel, lc=lc),
        grid=(bsz, nch),
        in_specs=[pl.BlockSpec((lc, M_WIDTH), lambda b, c: (row(b, c), 0)),
                  pl.BlockSpec((lc, M_WIDTH), lambda b, c: (row(b, c), 1)),
                  pl.BlockSpec((lc, M_WIDTH), lambda b, c: (row(b, c), 2)),
                  pl.BlockSpec((lc, M_WIDTH), lambda b, c: (row(b, c), 3)),
                  pl.BlockSpec((lc, LANES), lambda b, c: (row(b, c), 0)),
                  _const_spec((CONV_WIDTH, M_WIDTH)), _const_spec((CONV_WIDTH, M_WIDTH)),
                  _const_spec((1, M_WIDTH)), _const_spec((1, M_WIDTH)),
                  _const_spec((1, LANES)), _const_spec((1, M_WIDTH))],
        out_specs=pl.BlockSpec((lc, M_WIDTH), lambda b, c: (row(b, c), 0)),
        out_shape=jax.ShapeDtypeStruct((t, M_WIDTH), BF16),
        scratch_shapes=[pltpu.VMEM((lc + 8, M_WIDTH), F32), pltpu.VMEM((lc + 8, M_WIDTH), F32),
                        pltpu.VMEM((M_HEADS, M_HEAD_DIM, M_HEAD_DIM), F32),
                        pltpu.VMEM((8, M_HEAD_DIM), F32), pltpu.VMEM((8, LANES), F32)],
        compiler_params=_cparams("parallel", "arbitrary"),
        name="mlstm",
    )(y, y, y, y, misc, cwq, cwk, cbq, cbk, gate_bias, hg)


def _mla_prep_kernel(cq_ref, ckv_ref, misc_ref, cos_ref, sa_ref, sb_ref, cqg_ref, wuq_ref, ckvg_ref, wuk_ref,
                     wuv_ref, qng_ref, kng_ref, q_ref, k_ref, v_ref):
    cqn = _rms(cq_ref[...].astype(F32), cqg_ref[...]).astype(BF16)
    qa = _dot(cqn, wuq_ref[...])
    ckvn = _rms(ckv_ref[...].astype(F32), ckvg_ref[...]).astype(BF16)
    ka = _dot(ckvn, wuk_ref[...])
    va = _dot(ckvn, wuv_ref[...])
    lane = lax.broadcasted_iota(jnp.int32, misc_ref.shape, 1)
    kr = jnp.where((lane >= KR_LANE) & (lane < KR_LANE + QK_ROPE), misc_ref[...], 0.0)
    cos = cos_ref[...]
    sa = sa_ref[...]
    sb = sb_ref[...]

    def norm_rope(xh, g):
        xh = xh * lax.rsqrt(jnp.sum(xh * xh, axis=-1, keepdims=True) * (1.0 / QK_HEAD) + NORM_EPS) * g
        half = QK_ROPE // 2
        return xh * cos + pltpu.roll(xh, HEAD_PAD - half, axis=1) * sa + pltpu.roll(xh, half, axis=1) * sb

    for h in range(A_HEADS):
        sl = slice(h * HEAD_PAD, (h + 1) * HEAD_PAD)
        q_ref[0, h, 0] = (norm_rope(qa[:, sl], qng_ref[...]) * (LOG2_E * QK_HEAD ** -0.5)).T.astype(BF16)
        k_ref[0, h] = norm_rope(ka[:, sl] + kr, kng_ref[...]).astype(BF16)
        v_ref[0, h, 0] = jnp.where(lane < V_HEAD, va[:, sl], 1.0).T.astype(BF16)


def _mla_prep(y, misc, cos_t, sin_a, sin_b, cqg, wuq, ckvg, wuk, wuv, qng, kng, bsz, seq, ts):
    t = bsz * seq
    nt = seq // ts
    row = lambda b, i: (b * nt + i, 0)
    cq_blk = (4 * M_WIDTH) // Q_LORA
    ckv_blk = (4 * M_WIDTH + Q_LORA) // KV_LORA
    hw = A_HEADS * HEAD_PAD
    return pl.pallas_call(
        _mla_prep_kernel,
        grid=(bsz, nt),
        in_specs=[pl.BlockSpec((ts, Q_LORA), lambda b, i: (b * nt + i, cq_blk)),
                  pl.BlockSpec((ts, KV_LORA), lambda b, i: (b * nt + i, ckv_blk)),
                  pl.BlockSpec((ts, LANES), row), pl.BlockSpec((ts, LANES), row),
                  pl.BlockSpec((ts, LANES), row), pl.BlockSpec((ts, LANES), row),
                  _const_spec((1, Q_LORA)), _const_spec((Q_LORA, hw)),
                  _const_spec((1, KV_LORA)), _const_spec((KV_LORA, hw)), _const_spec((KV_LORA, hw)),
                  _const_spec((1, HEAD_PAD)), _const_spec((1, HEAD_PAD))],
        out_specs=[pl.BlockSpec((1, A_HEADS, 1, HEAD_PAD, ts), lambda b, i: (b, 0, i, 0, 0)),
                   pl.BlockSpec((1, A_HEADS, ts, HEAD_PAD), lambda b, i: (b, 0, i, 0)),
                   pl.BlockSpec((1, A_HEADS, 1, HEAD_PAD, ts), lambda b, i: (b, 0, i, 0, 0))],
        out_shape=[jax.ShapeDtypeStruct((bsz, A_HEADS, nt, HEAD_PAD, ts), BF16),
                   jax.ShapeDtypeStruct((bsz, A_HEADS, seq, HEAD_PAD), BF16),
                   jax.ShapeDtypeStruct((bsz, A_HEADS, nt, HEAD_PAD, ts), BF16)],
        compiler_params=_cparams("parallel", "parallel"),
        name="mla_prep",
    )(y, y, misc, cos_t, sin_a, sin_b, cqg, wuq, ckvg, wuk, wuv, qng, kng)


def _attn_kernel(q_ref, k_ref, v_ref, o_ref, *scratch, tq):
    m_refs, acc_refs = scratch[:A_HEADS], scratch[A_HEADS:]
    i = pl.program_id(1)
    key = lax.broadcasted_iota(jnp.int32, (tq, tq), 0)
    qry = lax.broadcasted_iota(jnp.int32, (tq, tq), 1)
    dmask = (key // MLA_CHUNK) <= (qry // MLA_CHUNK)
    for h in range(A_HEADS):
        m_refs[h][...] = jnp.full((1, tq), NEG_BIG, F32)
        acc_refs[h][...] = jnp.zeros((HEAD_PAD, tq), F32)

    def update(kt, masked):
        off = pl.multiple_of(kt * tq, tq)
        scores = [_dot(k_ref[0, h, pl.ds(off, tq), :], q_ref[0, h, 0]) for h in range(A_HEADS)]
        for h in range(A_HEADS):
            st = scores[h]
            if masked:
                st = jnp.where(dmask, st, NEG_BIG)
            m = m_refs[h][...]
            mn = jnp.maximum(m, jnp.max(st, axis=0, keepdims=True))
            p = jnp.exp2(st - mn)
            acc_refs[h][...] = jnp.exp2(m - mn) * acc_refs[h][...] + _dot(v_ref[0, h, kt], p.astype(BF16))
            m_refs[h][...] = mn

    def body(kt, carry):
        update(kt, False)
        return carry

    lax.fori_loop(0, i, body, 0)
    update(i, True)
    for hp in range(A_HEADS // 2):
        a0 = acc_refs[2 * hp][...]
        a1 = acc_refs[2 * hp + 1][...]
        pair = jnp.concatenate([a0[:V_HEAD] / a0[V_HEAD:V_HEAD + 1], a1[:V_HEAD] / a1[V_HEAD:V_HEAD + 1]], axis=0)
        o_ref[:, hp * LANES:(hp + 1) * LANES] = pair.T.astype(BF16)


def _attention(q_t, k, v_t, bsz, seq, tq):
    nq = seq // tq
    return pl.pallas_call(
        functools.partial(_attn_kernel, tq=tq),
        grid=(bsz, nq),
        in_specs=[pl.BlockSpec((1, A_HEADS, 1, HEAD_PAD, tq), lambda b, i: (b, 0, i, 0, 0)),
                  pl.BlockSpec((1, A_HEADS, seq, HEAD_PAD), lambda b, i: (b, 0, 0, 0)),
                  pl.BlockSpec((1, A_HEADS, nq, HEAD_PAD, tq), lambda b, i: (b, 0, 0, 0, 0))],
        out_specs=pl.BlockSpec((tq, A_WIDTH), lambda b, i: (b * nq + i, 0)),
        out_shape=jax.ShapeDtypeStruct((bsz * seq, A_WIDTH), BF16),
        scratch_shapes=([pltpu.VMEM((1, tq), F32)] * A_HEADS + [pltpu.VMEM((HEAD_PAD, tq), F32)] * A_HEADS),
        compiler_params=_cparams("parallel", "arbitrary"),
        name="mla_attention",
    )(q_t, k, v_t)


def _outproj_ffn_kernel(m_ref, a_ref, x_ref, wom_ref, woa_ref, g_ref, w1_ref, w3_ref, w2_ref, o_ref):
    x1 = x_ref[...] + _dot(m_ref[...], wom_ref[...]) + _dot(a_ref[...], woa_ref[...])
    hn = _rms(x1, g_ref[...]).astype(BF16)
    a = _dot(hn, w1_ref[...])
    b = _dot(hn, w3_ref[...])
    o_ref[...] = x1 + _dot((a * _sigmoid(a) * b).astype(BF16), w2_ref[...])


def _outproj_ffn(m_out, a_out, x, wom, woa, g, w1, w3, w2, tm):
    t = x.shape[0]
    return pl.pallas_call(
        _outproj_ffn_kernel,
        grid=(t // tm,),
        in_specs=[pl.BlockSpec((tm, M_WIDTH), lambda i: (i, 0)),
                  pl.BlockSpec((tm, A_WIDTH), lambda i: (i, 0)),
                  pl.BlockSpec((tm, D_MODEL), lambda i: (i, 0)),
                  _const_spec((M_WIDTH, D_MODEL)), _const_spec((A_WIDTH, D_MODEL)), _const_spec((1, D_MODEL)),
                  _const_spec((D_MODEL, D_FF)), _const_spec((D_MODEL, D_FF)), _const_spec((D_FF, D_MODEL))],
        out_specs=pl.BlockSpec((tm, D_MODEL), lambda i: (i, 0)),
        out_shape=jax.ShapeDtypeStruct((t, D_MODEL), F32),
        compiler_params=_cparams("parallel"),
        name="outproj_ffn",
    )(m_out, a_out, x, wom, woa, g, w1, w3, w2)


S5_CHUNK = 16
S5_ROW = S5_CHUNK * S5_GROUP


GROUPS_PER_TILE = LANES // S5_GROUP
RELAYOUT_ROWS = 32


def _lane_group_masks(rows):
    grp = lax.broadcasted_iota(jnp.int32, (rows, LANES), 1) // S5_GROUP
    return [grp == g for g in range(GROUPS_PER_TILE)]


def _to_group_major(vs, masks):
    n = GROUPS_PER_TILE
    rolled = []
    for r in range(n):
        d = vs[r % n]
        for g in range(1, n):
            d = jnp.where(masks[g], vs[(g + r) % n], d)
        rolled.append(d if r == 0 else pltpu.roll(d, r * S5_GROUP, axis=1))
    outs = []
    for grp in range(n):
        o = rolled[(-grp) % n]
        for w in range(1, n):
            o = jnp.where(masks[w], rolled[(w - grp) % n], o)
        outs.append(o)
    return outs


def _to_token_major(outs, masks):
    n = GROUPS_PER_TILE
    rolled = []
    for r in range(n):
        d = outs[(-r) % n]
        for w in range(1, n):
            d = jnp.where(masks[w], outs[(w - r) % n], d)
        rolled.append(d if r == 0 else pltpu.roll(d, LANES - r * S5_GROUP, axis=1))
    vs = []
    for w in range(n):
        v = rolled[w]
        for g in range(1, n):
            v = jnp.where(masks[g], rolled[(w - g) % n], v)
        vs.append(v)
    return vs


def _relayout_steps(tile_rows):
    nrow = tile_rows // S5_CHUNK
    return [(j, k, rg) for j in range(D_MODEL // LANES) for k in range(S5_CHUNK // GROUPS_PER_TILE)
            for rg in range(nrow // RELAYOUT_ROWS)]


def _norm_group_major_kernel(x_ref, g_ref, u_ref, un_scr):
    un = _rms(x_ref[...], g_ref[...])
    for j in range(D_MODEL // LANES):
        un_scr[j] = un[:, j * LANES:(j + 1) * LANES]
    masks = _lane_group_masks(RELAYOUT_ROWS)
    n = GROUPS_PER_TILE
    for j, k, rg in _relayout_steps(x_ref.shape[0]):
        rows = slice(rg * RELAYOUT_ROWS, (rg + 1) * RELAYOUT_ROWS)
        first = rg * RELAYOUT_ROWS * S5_CHUNK + k * n
        vs = [un_scr[j, pl.ds(first + w, RELAYOUT_ROWS, stride=S5_CHUNK), :] for w in range(n)]
        outs = _to_group_major(vs, masks)
        for grp in range(n):
            u_ref[j * n + grp, rows, k * LANES:(k + 1) * LANES] = outs[grp].astype(BF16)


def _norm_group_major(x, g, tm):
    t = x.shape[0]
    nrow = tm // S5_CHUNK
    return pl.pallas_call(
        _norm_group_major_kernel,
        grid=(t // tm,),
        in_specs=[pl.BlockSpec((tm, D_MODEL), lambda i: (i, 0)), _const_spec((1, D_MODEL))],
        out_specs=pl.BlockSpec((S5_GROUPS, nrow, S5_ROW), lambda i: (0, i, 0)),
        out_shape=jax.ShapeDtypeStruct((S5_GROUPS, t // S5_CHUNK, S5_ROW), BF16),
        scratch_shapes=[pltpu.VMEM((D_MODEL // LANES, tm, LANES), F32)],
        compiler_params=_cparams("parallel"),
        name="norm_group_major",
    )(x, g)


def _gelu_tanh(x):
    return 0.5 * x * (1.0 + jnp.tanh(math.sqrt(2.0 / math.pi) * (x + 0.044715 * (x * x * x))))


def _s5_kernel(u_ref, w_ref, cst_ref, a1_ref, a2_ref, dk_ref, z_ref, vv_scr, xp_scr, *, nch, bsz):
    two_p = 2 * S5_STATE
    u = u_ref[0]
    r = _dot(u, w_ref[0])
    vv_scr[0] = r[:, S5_ROW:S5_ROW + two_p]
    vv_scr[1] = r[:, S5_ROW + two_p:]
    a1 = a1_ref[0]
    a2 = a2_ref[0]

    def body(c, carry):
        p, q = carry
        rows = pl.ds(c, bsz, stride=nch)
        xp_scr[rows, :] = p
        pn = a1 * p + a2 * q + vv_scr[0, rows, :]
        qn = a1 * q - a2 * p + vv_scr[1, rows, :]
        return pn, qn

    zero = jnp.zeros((bsz, two_p), F32)
    lax.fori_loop(0, nch, body, (zero, zero), unroll=4)
    y = r[:, :S5_ROW] + _dot(xp_scr[...].astype(BF16), cst_ref[0]) + dk_ref[0] * u.astype(F32)
    z_ref[0] = _gelu_tanh(y).astype(BF16)


def _s5(u_g, w_all, cst, a1, a2, dk, nch, bsz):
    g, n, _ = u_g.shape
    two_p = 2 * S5_STATE
    wcols = S5_ROW + 2 * two_p
    return pl.pallas_call(
        functools.partial(_s5_kernel, nch=nch, bsz=bsz),
        grid=(g,),
        in_specs=[pl.BlockSpec((1, n, S5_ROW), lambda i: (i, 0, 0)),
                  pl.BlockSpec((1, S5_ROW, wcols), lambda i: (i, 0, 0)),
                  pl.BlockSpec((1, two_p, S5_ROW), lambda i: (i, 0, 0)),
                  pl.BlockSpec((1, 1, two_p), lambda i: (i, 0, 0)),
                  pl.BlockSpec((1, 1, two_p), lambda i: (i, 0, 0)),
                  pl.BlockSpec((1, 1, S5_ROW), lambda i: (i, 0, 0))],
        out_specs=pl.BlockSpec((1, n, S5_ROW), lambda i: (i, 0, 0)),
        out_shape=jax.ShapeDtypeStruct((g, n, S5_ROW), BF16),
        scratch_shapes=[pltpu.VMEM((2, n, two_p), F32), pltpu.VMEM((n, two_p), F32)],
        compiler_params=_cparams("parallel"),
        name="s5",
    )(u_g, w_all, cst, a1, a2, dk)


def _s5_weights(lam_re, lam_im, log_dt, b_re, b_im, c_re, c_im, d_skip):
    lc, h, p = S5_CHUNK, S5_GROUP, S5_STATE
    lam = lax.complex(jnp.minimum(lam_re.astype(F32), -1e-4), lam_im.astype(F32))
    dt = jnp.exp(log_dt.astype(F32))[:, None]
    ldt = lam * dt
    a_bar = jnp.exp(ldt)
    b_bar = ((a_bar - 1.0) / lam)[..., None] * lax.complex(b_re.astype(F32), b_im.astype(F32))
    c_mat = lax.complex(c_re.astype(F32), c_im.astype(F32))
    steps = jnp.arange(lc + 1, dtype=F32)
    apow = jnp.exp(ldt[None] * steps[:, None, None].astype(ldt.dtype))
    g = lam.shape[0]
    ker = jnp.einsum('gip,kgp,gpj->gkij', c_mat, apow[:lc], b_bar).real
    lag = np.arange(lc)[None, :] - np.arange(lc)[:, None]
    toe = jnp.where((lag >= 0)[None, :, :, None, None], ker[:, np.clip(lag, 0, lc - 1)], 0.0)
    toe = toe.transpose(0, 1, 4, 2, 3).reshape(g, lc * h, lc * h)
    wst = apow[:lc][::-1].transpose(1, 0, 2)[:, :, None, :] * jnp.swapaxes(b_bar, 1, 2)[:, None, :, :]
    wst = wst.reshape(g, lc * h, p)
    w_all = jnp.concatenate([toe, wst.real, wst.imag, wst.imag, wst.real], axis=-1).astype(BF16)
    ca = c_mat[:, None, :, :] * apow[1:lc + 1].transpose(1, 0, 2)[:, :, None, :]
    ca = ca.reshape(g, lc * h, p)
    cst = jnp.concatenate([ca.real, -ca.imag], axis=-1).transpose(0, 2, 1).astype(BF16)
    al = apow[lc]
    a1 = jnp.concatenate([al.real, al.real], axis=-1)[:, None, :]
    a2 = jnp.concatenate([-al.imag, al.imag], axis=-1)[:, None, :]
    dk = jnp.tile(d_skip.astype(F32).reshape(g, 1, h), (1, lc, 1)).reshape(g, 1, lc * h)
    return w_all, cst, a1, a2, dk


def _glu_kernel(z_ref, x_ref, wv_ref, wg_ref, g_ref, x2_ref, xn_ref, zt_scr):
    masks = _lane_group_masks(RELAYOUT_ROWS)
    n = GROUPS_PER_TILE
    for j, k, rg in _relayout_steps(x_ref.shape[0]):
        rows = slice(rg * RELAYOUT_ROWS, (rg + 1) * RELAYOUT_ROWS)
        outs = [z_ref[j * n + grp, rows, k * LANES:(k + 1) * LANES].astype(F32) for grp in range(n)]
        vs = _to_token_major(outs, masks)
        first = rg * RELAYOUT_ROWS * S5_CHUNK + k * n
        for w in range(n):
            zt_scr[j, pl.ds(first + w, RELAYOUT_ROWS, stride=S5_CHUNK), :] = vs[w]
    z = jnp.concatenate([zt_scr[j] for j in range(D_MODEL // LANES)], axis=1).astype(BF16)
    x2 = x_ref[...] + _dot(z, wv_ref[...]) * _sigmoid(_dot(z, wg_ref[...]))
    x2_ref[...] = x2
    xn_ref[...] = _rms(x2, g_ref[...]).astype(BF16)


def _glu(z_g, x, wv, wg, g, tm):
    t = x.shape[0]
    nrow = tm // S5_CHUNK
    return pl.pallas_call(
        _glu_kernel,
        grid=(t // tm,),
        in_specs=[pl.BlockSpec((S5_GROUPS, nrow, S5_ROW), lambda i: (0, i, 0)),
                  pl.BlockSpec((tm, D_MODEL), lambda i: (i, 0)),
                  _const_spec((D_MODEL, D_MODEL)), _const_spec((D_MODEL, D_MODEL)), _const_spec((1, D_MODEL))],
        out_specs=[pl.BlockSpec((tm, D_MODEL), lambda i: (i, 0)), pl.BlockSpec((tm, D_MODEL), lambda i: (i, 0))],
        out_shape=[jax.ShapeDtypeStruct((t, D_MODEL), F32), jax.ShapeDtypeStruct((t, D_MODEL), BF16)],
        scratch_shapes=[pltpu.VMEM((D_MODEL // LANES, tm, LANES), F32)],
        compiler_params=_cparams("parallel"),
        name="glu",
    )(z_g, x, wv, wg, g)


def _router_kernel(xn_ref, wh_ref, wl_ref, pos_ref, gate_ref, posc_ref, cnt_ref, *, tb):
    xn = xn_ref[...]
    lg = (_dot_nt(wh_ref[...], xn) + _dot_nt(wl_ref[...], xn))[0:N_EXPERTS]
    eidx = lax.broadcasted_iota(jnp.int32, (N_EXPERTS, tb), 0)
    m1 = jnp.max(lg, axis=0, keepdims=True)
    i1 = jnp.min(jnp.where(lg == m1, eidx, N_EXPERTS), axis=0, keepdims=True)
    rest = jnp.where(eidx == i1, -jnp.inf, lg)
    m2 = jnp.max(rest, axis=0, keepdims=True)
    i2 = jnp.min(jnp.where(rest == m2, eidx, N_EXPERTS), axis=0, keepdims=True)
    e21 = jnp.exp(m2 - m1)
    p1 = 1.0 / (1.0 + e21)
    sel1 = eidx == i1
    sel2 = eidx == i2
    sel = sel1 | sel2
    gate_ref[...] = jnp.where(sel1, p1, jnp.where(sel2, e21 * p1, 0.0))
    onehot = jnp.concatenate([sel.astype(F32), jnp.zeros((N_EXPERTS, tb), F32)], axis=0).astype(BF16)
    srow = lax.broadcasted_iota(jnp.int32, (tb, tb), 0)
    scol = lax.broadcasted_iota(jnp.int32, (tb, tb), 1)
    rank = _dot(onehot, (srow < scol).astype(BF16))[0:N_EXPERTS]
    pos = jnp.where(sel, rank, -1.0)
    pos_ref[...] = pos
    posc_ref[...] = jnp.concatenate([pos, jnp.full((LANES - N_EXPERTS, tb), -1.0, F32)], axis=0).T
    cnt = jnp.sum(sel.astype(F32), axis=1, keepdims=True)
    cnt_ref[0] = jnp.broadcast_to(cnt, (N_EXPERTS, LANES))


def _router(xn, wr_hi, wr_lo, tb):
    t = xn.shape[0]
    nb = t // tb
    return pl.pallas_call(
        functools.partial(_router_kernel, tb=tb),
        grid=(nb,),
        in_specs=[pl.BlockSpec((tb, D_MODEL), lambda i: (i, 0)),
                  _const_spec((2 * N_EXPERTS, D_MODEL)), _const_spec((2 * N_EXPERTS, D_MODEL))],
        out_specs=[pl.BlockSpec((N_EXPERTS, tb), lambda i: (0, i)), pl.BlockSpec((N_EXPERTS, tb), lambda i: (0, i)),
                   pl.BlockSpec((tb, LANES), lambda i: (i, 0)),
                   pl.BlockSpec((1, N_EXPERTS, LANES), lambda i: (i, 0, 0))],
        out_shape=[jax.ShapeDtypeStruct((N_EXPERTS, t), F32), jax.ShapeDtypeStruct((N_EXPERTS, t), F32),
                   jax.ShapeDtypeStruct((t, LANES), F32), jax.ShapeDtypeStruct((nb, N_EXPERTS, LANES), F32)],
        compiler_params=_cparams("parallel"),
        name="router",
    )(xn, wr_hi, wr_lo)


MOE_UNIT = 128
MOE_ALIGN = 16
MOE_TM = 512
MOE_FF_SPLIT = 2


def _moe_plan(cnt, n_tokens):
    nb = cnt.shape[0]
    seg = (cnt + MOE_ALIGN - 1) // MOE_ALIGN * MOE_ALIGN
    used = jnp.sum(seg, axis=0)
    gsize = (used + MOE_UNIT + MOE_TM - 1) // MOE_TM * MOE_TM
    gstart = jnp.cumsum(gsize) - gsize
    dest = gstart[None, :] + jnp.cumsum(seg, axis=0) - seg
    max_rows = 2 * n_tokens + N_EXPERTS * (nb * (MOE_ALIGN - 1) + MOE_UNIT + MOE_TM - 1)
    n_tiles = -(-max_rows // MOE_TM)
    tile_start = jnp.arange(n_tiles, dtype=jnp.int32) * MOE_TM
    tile_e = jnp.minimum(jnp.sum(tile_start[:, None] >= (gstart + gsize)[None, :], axis=1), N_EXPERTS - 1)
    tile_valid = tile_start < (gstart + used)[tile_e]
    return dest.reshape(-1).astype(jnp.int32), tile_e.astype(jnp.int32), tile_valid.astype(jnp.int32), n_tiles


def _moe_pack_kernel(dest_ref, cnt_ref, xn_ref, pos_ref, xs_in_ref, xs_ref, stage, sem, *, tb):
    del xs_in_ref
    b = pl.program_id(0)
    unit = MOE_UNIT
    slot = lax.broadcasted_iota(jnp.int32, (unit, tb), 0).astype(F32)

    def n_units(e):
        return (cnt_ref[b * N_EXPERTS + e] + unit - 1) // unit

    def chunk_copy(e, r):
        dst = pl.multiple_of(dest_ref[b * N_EXPERTS + e] + r * unit, MOE_ALIGN)
        return pltpu.make_async_copy(stage.at[e % 2, pl.ds(pl.multiple_of(r * unit, unit), unit)],
                                     xs_ref.at[pl.ds(dst, unit)], sem.at[e % 2])

    def wait_expert(e):
        def body(r, carry):
            chunk_copy(e, r).wait()
            return carry
        lax.fori_loop(0, n_units(e), body, 0)

    for e in range(N_EXPERTS):
        if e >= 2:
            wait_expert(e - 2)
        prow = pos_ref[e:e + 1, :]

        def body(r, carry, e=e, prow=prow):
            base = pl.multiple_of(r * unit, unit)
            onehot = (prow - base.astype(F32) == slot).astype(BF16)
            stage[e % 2, pl.ds(base, unit), :] = _dot(onehot, xn_ref[...]).astype(BF16)
            chunk_copy(e, r).start()
            return carry

        lax.fori_loop(0, n_units(e), body, 0)
    wait_expert(N_EXPERTS - 2)
    wait_expert(N_EXPERTS - 1)


def _moe_pack(dest, cnt, xn, pos, n_rows, tb):
    t = xn.shape[0]
    grid_spec = pltpu.PrefetchScalarGridSpec(
        num_scalar_prefetch=2,
        grid=(t // tb,),
        in_specs=[pl.BlockSpec((tb, D_MODEL), lambda b, d, c: (b, 0)),
                  pl.BlockSpec((N_EXPERTS, tb), lambda b, d, c: (0, b)),
                  pl.BlockSpec(memory_space=pl.ANY)],
        out_specs=pl.BlockSpec(memory_space=pl.ANY),
        scratch_shapes=[pltpu.VMEM((2, tb, D_MODEL), BF16), pltpu.SemaphoreType.DMA((2,))],
    )
    return pl.pallas_call(
        functools.partial(_moe_pack_kernel, tb=tb),
        grid_spec=grid_spec,
        out_shape=jax.ShapeDtypeStruct((n_rows, D_MODEL), BF16),
        input_output_aliases={4: 0},
        compiler_params=_cparams("arbitrary"),
        name="moe_pack",
    )(dest, cnt, xn, pos, jnp.zeros((n_rows, D_MODEL), BF16))


def _moe_ffn_kernel(te_ref, tv_ref, xs_ref, w1_ref, w3_ref, w2_ref, y_ref, acc_ref):
    del te_ref
    i = pl.program_id(0)
    f = pl.program_id(1)
    valid = tv_ref[i] != 0

    @pl.when(valid)
    def _():
        xe = xs_ref[...]
        a = _dot(xe, w1_ref[0])
        g = (a * _sigmoid(a) * _dot(xe, w3_ref[0])).astype(BF16)
        y = _dot(g, w2_ref[0])

        @pl.when(f == 0)
        def _():
            acc_ref[...] = y

        @pl.when(f != 0)
        def _():
            acc_ref[...] += y

    @pl.when(f == MOE_FF_SPLIT - 1)
    def _():
        @pl.when(valid)
        def _():
            y_ref[...] = acc_ref[...].astype(BF16)

        @pl.when(jnp.logical_not(valid))
        def _():
            y_ref[...] = jnp.zeros(y_ref.shape, BF16)


def _moe_ffn(tile_e, tile_valid, xs, w1, w3, w2, n_tiles):
    fs = D_FF_EXPERT // MOE_FF_SPLIT
    grid_spec = pltpu.PrefetchScalarGridSpec(
        num_scalar_prefetch=2,
        grid=(n_tiles, MOE_FF_SPLIT),
        in_specs=[pl.BlockSpec((MOE_TM, D_MODEL), lambda i, f, te, tv: (i, 0)),
                  pl.BlockSpec((1, D_MODEL, fs), lambda i, f, te, tv: (te[i], 0, f)),
                  pl.BlockSpec((1, D_MODEL, fs), lambda i, f, te, tv: (te[i], 0, f)),
                  pl.BlockSpec((1, fs, D_MODEL), lambda i, f, te, tv: (te[i], f, 0))],
        out_specs=pl.BlockSpec((MOE_TM, D_MODEL), lambda i, f, te, tv: (i, 0)),
        scratch_shapes=[pltpu.VMEM((MOE_TM, D_MODEL), F32)],
    )
    return pl.pallas_call(
        _moe_ffn_kernel,
        grid_spec=grid_spec,
        out_shape=jax.ShapeDtypeStruct((n_tiles * MOE_TM, D_MODEL), BF16),
        compiler_params=_cparams("parallel", "arbitrary"),
        name="moe_ffn",
    )(tile_e, tile_valid, xs, w1, w3, w2)


def _moe_combine_kernel(dest_ref, cnt_ref, x_ref, pos_ref, gate_ref, posc_ref, ys_ref, o_ref, ybuf, sem, *, tb):
    b = pl.program_id(0)
    unit = MOE_UNIT

    def n_units(e):
        return (cnt_ref[b * N_EXPERTS + e] + unit - 1) // unit

    def chunk_copy(e, r):
        src = pl.multiple_of(dest_ref[b * N_EXPERTS + e] + r * unit, MOE_ALIGN)
        return pltpu.make_async_copy(ys_ref.at[pl.ds(src, unit)],
                                     ybuf.at[e % 2, pl.ds(pl.multiple_of(r * unit, unit), unit)], sem.at[e % 2])

    def start_expert(e):
        def body(r, carry):
            chunk_copy(e, r).start()
            return carry
        lax.fori_loop(0, n_units(e), body, 0)

    def wait_expert(e):
        def body(r, carry):
            chunk_copy(e, r).wait()
            return carry
        lax.fori_loop(0, n_units(e), body, 0)

    start_expert(0)
    o_ref[...] = x_ref[...]
    lane = lax.broadcasted_iota(jnp.int32, (tb, LANES), 1)
    for e in range(N_EXPERTS):
        if e + 1 < N_EXPERTS:
            start_expert(e + 1)
        wait_expert(e)
        prow = pos_ref[e:e + 1, :]
        grow = gate_ref[e:e + 1, :]
        n_unit = n_units(e)
        n_pair = n_unit // 2

        def scatter_chunk(base, m, e=e, prow=prow, grow=grow):
            hit = prow == lax.broadcasted_iota(jnp.int32, (m, tb), 0).astype(F32) + base.astype(F32)
            gs = jnp.sum(jnp.where(hit, grow, 0.0), axis=-1, keepdims=True)
            yg = (ybuf[e % 2, pl.ds(base, m), :].astype(F32) * gs).astype(BF16)
            o_ref[...] += _dot_tn(hit.astype(BF16), yg)

        def scatter_pair(r, carry, scatter_chunk=scatter_chunk):
            scatter_chunk(pl.multiple_of(r * (2 * unit), 2 * unit), 2 * unit)
            return carry

        lax.fori_loop(0, n_pair, scatter_pair, 0)

        @pl.when(n_unit % 2 == 1)
        def _(scatter_chunk=scatter_chunk, n_pair=n_pair):
            scatter_chunk(pl.multiple_of(n_pair * (2 * unit), 2 * unit), unit)


def _moe_combine(dest, cnt, x, pos, gate, posc, ys, tb):
    t = x.shape[0]
    grid_spec = pltpu.PrefetchScalarGridSpec(
        num_scalar_prefetch=2,
        grid=(t // tb,),
        in_specs=[pl.BlockSpec((tb, D_MODEL), lambda b, d, c: (b, 0)),
                  pl.BlockSpec((N_EXPERTS, tb), lambda b, d, c: (0, b)),
                  pl.BlockSpec((N_EXPERTS, tb), lambda b, d, c: (0, b)),
                  pl.BlockSpec((tb, LANES), lambda b, d, c: (b, 0)),
                  pl.BlockSpec(memory_space=pl.ANY)],
        out_specs=pl.BlockSpec((tb, D_MODEL), lambda b, d, c: (b, 0)),
        scratch_shapes=[pltpu.VMEM((2, tb, D_MODEL), BF16), pltpu.SemaphoreType.DMA((2,))],
    )
    return pl.pallas_call(
        functools.partial(_moe_combine_kernel, tb=tb),
        grid_spec=grid_spec,
        out_shape=jax.ShapeDtypeStruct((t, D_MODEL), F32),
        compiler_params=_cparams("parallel"),
        name="moe_combine",
    )(dest, cnt, x, pos, gate, posc, ys)


def _moe(cnt, xn, x, pos, gate, posc, w1, w3, w2, tb):
    t = x.shape[0]
    dest, tile_e, tile_valid, n_tiles = _moe_plan(cnt, t)
    cnt = cnt.reshape(-1)
    xs = _moe_pack(dest, cnt, xn, pos, n_tiles * MOE_TM, tb)
    ys = _moe_ffn(tile_e, tile_valid, xs, w1, w3, w2, n_tiles)
    return _moe_combine(dest, cnt, x, pos, gate, posc, ys, tb)


def _rope_tables(positions):
    half = QK_ROPE // 2
    inv_freq = ROPE_THETA ** (-jnp.arange(0, QK_ROPE, 2, dtype=F32) / QK_ROPE)
    ang = positions.astype(F32).reshape(-1, 1) * inv_freq
    cos, sin = jnp.cos(ang), jnp.sin(ang)
    t = ang.shape[0]
    ones_lo = jnp.ones((t, QK_NOPE), F32)
    zeros_lo = jnp.zeros((t, QK_NOPE), F32)
    zeros_h = jnp.zeros((t, half), F32)
    tail1 = jnp.ones((t, HEAD_PAD - QK_HEAD), F32)
    tail0 = jnp.zeros((t, HEAD_PAD - QK_HEAD), F32)
    cos_t = jnp.concatenate([ones_lo, cos, cos, tail1], axis=-1)
    sin_a = jnp.concatenate([zeros_lo, -sin, zeros_h, tail0], axis=-1)
    sin_b = jnp.concatenate([zeros_lo, zeros_h, sin, tail0], axis=-1)
    return cos_t, sin_a, sin_b


def _pad_heads(w, per_head_in, keep, heads):
    k = w.shape[0]
    w = w.reshape(k, heads, per_head_in)[:, :, :keep]
    return jnp.pad(w, ((0, 0), (0, 0), (0, HEAD_PAD - keep))).reshape(k, heads * HEAD_PAD)


def _even_layer(x, rope, bsz, seq, norm1_g, w_in, conv_w, conv_b, b_igate, b_fgate, mh_norm_g, cq_norm_g, w_uq,
                ckv_norm_g, w_ukv, qn_g, kn_g, w_o, norm2_g, ffn_w1, ffn_w3, ffn_w2):
    row = lambda v: v.astype(F32).reshape(1, -1)
    mw = M_WIDTH
    o_ig, o_fg, o_og = 3 * mw, 3 * mw + M_HEADS, 3 * mw + 2 * M_HEADS
    o_cq = o_og + mw
    o_ckv, o_kr = o_cq + Q_LORA, o_cq + Q_LORA + KV_LORA
    misc_w = jnp.zeros((D_MODEL, LANES), F32)
    misc_w = misc_w.at[:, IG_LANE:IG_LANE + M_HEADS].set(w_in[:, o_ig:o_fg])
    misc_w = misc_w.at[:, FG_LANE:FG_LANE + M_HEADS].set(w_in[:, o_fg:o_og])
    misc_w = misc_w.at[:, KR_LANE:KR_LANE + QK_ROPE].set(w_in[:, o_kr:o_kr + QK_ROPE])
    w_packed = jnp.concatenate([w_in[:, :3 * mw], w_in[:, o_og:o_cq], w_in[:, o_cq:o_ckv], w_in[:, o_ckv:o_kr],
                                misc_w], axis=-1).astype(BF16)
    gate_bias = jnp.zeros((1, LANES), F32)
    gate_bias = gate_bias.at[0, IG_LANE:IG_LANE + M_HEADS].set(b_igate.astype(F32))
    gate_bias = gate_bias.at[0, FG_LANE:FG_LANE + M_HEADS].set(b_fgate.astype(F32))

    tm = min(512, x.shape[0])
    y, misc = _inproj(x, row(norm1_g), w_packed, tm)

    lc = min(256, seq)
    m_out = _mlstm(y, misc, conv_w[:, :mw].astype(F32), conv_w[:, mw:].astype(F32), row(conv_b[:mw]),
                   row(conv_b[mw:]), gate_bias, row(mh_norm_g), bsz, seq, lc)

    wuq = _pad_heads(w_uq, QK_HEAD, QK_HEAD, A_HEADS).astype(BF16)
    wuk = _pad_heads(w_ukv, QK_NOPE + V_HEAD, QK_NOPE, A_HEADS).astype(BF16)
    wuv = _pad_heads(w_ukv.reshape(KV_LORA, A_HEADS, QK_NOPE + V_HEAD)[:, :, QK_NOPE:].reshape(KV_LORA, A_WIDTH),
                     V_HEAD, V_HEAD, A_HEADS).astype(BF16)
    pad_g = lambda g: jnp.pad(g.astype(F32), (0, HEAD_PAD - QK_HEAD)).reshape(1, HEAD_PAD)
    ts = min(256, seq)
    q, k, v = _mla_prep(y, misc, rope[0], rope[1], rope[2], row(cq_norm_g), wuq, row(ckv_norm_g), wuk, wuv,
                        pad_g(qn_g), pad_g(kn_g), bsz, seq, ts)
    a_out = _attention(q, k, v, bsz, seq, ts)

    return _outproj_ffn(m_out, a_out, x, w_o[:mw].astype(BF16), w_o[mw:].astype(BF16), row(norm2_g),
                        ffn_w1.astype(BF16), ffn_w3.astype(BF16), ffn_w2.astype(BF16), tm)


def _odd_layer(x, bsz, seq, norm1_g, lam_re, lam_im, log_dt, b_re, b_im, c_re, c_im, d_skip, glu_wv, glu_wg,
               norm2_g, router_w, moe_w1, moe_w3, moe_w2):
    row = lambda v: v.astype(F32).reshape(1, -1)
    t = x.shape[0]
    tm = min(512, t)
    nch = seq // S5_CHUNK
    tr = min(1024, seq)
    u_g = _norm_group_major(x, row(norm1_g), tr)
    w_all, cst, a1, a2, dk = _s5_weights(lam_re, lam_im, log_dt, b_re, b_im, c_re, c_im, d_skip)
    z_g = _s5(u_g, w_all, cst, a1, a2, dk, nch, bsz)
    x2, xn = _glu(z_g, x, glu_wv.astype(BF16), glu_wg.astype(BF16), row(norm2_g), tr)

    tb = min(1024, t)
    wr = jnp.pad(router_w.astype(F32).T, ((0, N_EXPERTS), (0, 0)))
    wr_hi = wr.astype(BF16)
    wr_lo = (wr - wr_hi.astype(F32)).astype(BF16)
    pos, gate, posc, cnt = _router(xn, wr_hi, wr_lo, tb)
    cnt = cnt[:, :, 0].astype(jnp.int32)
    return _moe(cnt, xn, x2, pos, gate, posc, moe_w1.astype(BF16), moe_w3.astype(BF16), moe_w2.astype(BF16), tb)


def kernel(x, positions, l0_norm1_g, l0_w_in, l0_conv_w, l0_conv_b, l0_b_igate, l0_b_fgate, l0_mh_norm_g, l0_cq_norm_g, l0_w_uq, l0_ckv_norm_g, l0_w_ukv, l0_qn_g, l0_kn_g, l0_w_o, l0_norm2_g, l0_ffn_w1, l0_ffn_w3, l0_ffn_w2, l1_norm1_g, l1_lam_re, l1_lam_im, l1_log_dt, l1_b_re, l1_b_im, l1_c_re, l1_c_im, l1_d_skip, l1_glu_wv, l1_glu_wg, l1_norm2_g, l1_router_w, l1_moe_w1, l1_moe_w3, l1_moe_w2, l2_norm1_g, l2_w_in, l2_conv_w, l2_conv_b, l2_b_igate, l2_b_fgate, l2_mh_norm_g, l2_cq_norm_g, l2_w_uq, l2_ckv_norm_g, l2_w_ukv, l2_qn_g, l2_kn_g, l2_w_o, l2_norm2_g, l2_ffn_w1, l2_ffn_w3, l2_ffn_w2, l3_norm1_g, l3_lam_re, l3_lam_im, l3_log_dt, l3_b_re, l3_b_im, l3_c_re, l3_c_im, l3_d_skip, l3_glu_wv, l3_glu_wg, l3_norm2_g, l3_router_w, l3_moe_w1, l3_moe_w3, l3_moe_w2):
    bsz, seq, _ = x.shape
    rope = _rope_tables(positions)
    h = x.reshape(bsz * seq, D_MODEL)
    h = _even_layer(h, rope, bsz, seq, l0_norm1_g, l0_w_in, l0_conv_w, l0_conv_b, l0_b_igate, l0_b_fgate,
                    l0_mh_norm_g, l0_cq_norm_g, l0_w_uq, l0_ckv_norm_g, l0_w_ukv, l0_qn_g, l0_kn_g, l0_w_o,
                    l0_norm2_g, l0_ffn_w1, l0_ffn_w3, l0_ffn_w2)
    h = _odd_layer(h, bsz, seq, l1_norm1_g, l1_lam_re, l1_lam_im, l1_log_dt, l1_b_re, l1_b_im, l1_c_re, l1_c_im,
                   l1_d_skip, l1_glu_wv, l1_glu_wg, l1_norm2_g, l1_router_w, l1_moe_w1, l1_moe_w3, l1_moe_w2)
    h = _even_layer(h, rope, bsz, seq, l2_norm1_g, l2_w_in, l2_conv_w, l2_conv_b, l2_b_igate, l2_b_fgate,
                    l2_mh_norm_g, l2_cq_norm_g, l2_w_uq, l2_ckv_norm_g, l2_w_ukv, l2_qn_g, l2_kn_g, l2_w_o,
                    l2_norm2_g, l2_ffn_w1, l2_ffn_w3, l2_ffn_w2)
    h = _odd_layer(h, bsz, seq, l3_norm1_g, l3_lam_re, l3_lam_im, l3_log_dt, l3_b_re, l3_b_im, l3_c_re, l3_c_im,
                   l3_d_skip, l3_glu_wv, l3_glu_wg, l3_norm2_g, l3_router_w, l3_moe_w1, l3_moe_w3, l3_moe_w2)
    return h.reshape(bsz, seq, D_MODEL)
```

```python
import functools
import math

import jax
import jax.numpy as jnp
import numpy as np
from jax import lax
from jax.experimental import pallas as pl
from jax.experimental.pallas import tpu as pltpu

F32 = jnp.float32
BF16 = jnp.bfloat16

D_MODEL = 1024
NORM_EPS = 1e-6
M_HEADS = 4
M_HEAD_DIM = 128
M_WIDTH = M_HEADS * M_HEAD_DIM
CONV_WIDTH = 4
A_HEADS = 8
Q_LORA = 256
KV_LORA = 128
QK_NOPE = 64
QK_ROPE = 32
QK_HEAD = QK_NOPE + QK_ROPE
V_HEAD = 64
A_WIDTH = A_HEADS * V_HEAD
ROPE_THETA = 10000.0
MLA_CHUNK = 64
S5_GROUP = 16
S5_GROUPS = D_MODEL // S5_GROUP
S5_STATE = 64
D_FF = 2816
N_EXPERTS = 8
D_FF_EXPERT = 3584

LANES = 128
HEAD_PAD = 128
Y_COLS = 4 * M_WIDTH + Q_LORA + KV_LORA
IG_LANE = 0
FG_LANE = 8
KR_LANE = QK_NOPE

VMEM_LIMIT = 56 * 1024 * 1024
NEG_BIG = -1e30
LOG2_E = math.log2(math.e)


def _cparams(*sem):
    return pltpu.CompilerParams(dimension_semantics=sem, vmem_limit_bytes=VMEM_LIMIT)


def _rms(x, g):
    return x * lax.rsqrt(jnp.mean(x * x, axis=-1, keepdims=True) + NORM_EPS) * g


def _sigmoid(x):
    return 1.0 / (1.0 + jnp.exp(-x))


def _dot(a, b):
    return jnp.dot(a, b, preferred_element_type=F32)


def _dot_nt(a, b):
    return lax.dot_general(a, b, (((1,), (1,)), ((), ())), preferred_element_type=F32)


def _dot_tn(a, b):
    return lax.dot_general(a, b, (((0,), (0,)), ((), ())), preferred_element_type=F32)


def _const_spec(shape):
    nd = len(shape)
    return pl.BlockSpec(shape, lambda *_: (0,) * nd, pipeline_mode=pl.Buffered(1))


def _inproj_kernel(x_ref, g_ref, w_ref, y_ref, misc_ref):
    xn = _rms(x_ref[...], g_ref[...]).astype(BF16)
    y = _dot(xn, w_ref[...])
    y_ref[...] = y[:, :Y_COLS].astype(BF16)
    misc_ref[...] = y[:, Y_COLS:]


def _inproj(x, g, w_packed, tm):
    t = x.shape[0]
    return pl.pallas_call(
        _inproj_kernel,
        grid=(t // tm,),
        in_specs=[pl.BlockSpec((tm, D_MODEL), lambda i: (i, 0)),
                  _const_spec((1, D_MODEL)),
                  _const_spec((D_MODEL, Y_COLS + LANES))],
        out_specs=[pl.BlockSpec((tm, Y_COLS), lambda i: (i, 0)),
                   pl.BlockSpec((tm, LANES), lambda i: (i, 0))],
        out_shape=[jax.ShapeDtypeStruct((t, Y_COLS), BF16),
                   jax.ShapeDtypeStruct((t, LANES), F32)],
        compiler_params=_cparams("parallel"),
        name="inproj",
    )(x, g, w_packed)


def _cummax_lanes(r, width):
    lane = lax.broadcasted_iota(jnp.int32, r.shape, 1)
    s = 1
    while s < width:
        shifted = pltpu.roll(r, s, axis=1)
        r = jnp.where(lane >= s, jnp.maximum(r, shifted), r)
        s *= 2
    return r


def _mlstm_kernel(q_ref, k_ref, v_ref, og_ref, misc_ref, cwq_ref, cwk_ref, cbq_ref, cbk_ref, gb_ref, hg_ref,
                  out_ref, qpad, kpad, c_scr, n_scr, m_scr, *, lc):
    @pl.when(pl.program_id(1) == 0)
    def _():
        qpad[0:8, :] = jnp.zeros((8, M_WIDTH), F32)
        kpad[0:8, :] = jnp.zeros((8, M_WIDTH), F32)
        c_scr[...] = jnp.zeros_like(c_scr)
        n_scr[...] = jnp.zeros_like(n_scr)
        m_scr[...] = jnp.zeros_like(m_scr)

    qpad[8:8 + lc, :] = q_ref[...].astype(F32)
    kpad[8:8 + lc, :] = k_ref[...].astype(F32)

    def conv_silu(pad, cw_ref, cb_ref):
        acc = jnp.broadcast_to(cb_ref[...], (lc, M_WIDTH))
        for j in range(CONV_WIDTH):
            off = 8 - (CONV_WIDTH - 1) + j
            acc = acc + pad[off:off + lc, :] * cw_ref[j:j + 1, :]
        return acc * _sigmoid(acc)

    qc = conv_silu(qpad, cwq_ref, cbq_ref) * (M_HEAD_DIM ** -0.5)
    kc = conv_silu(kpad, cwk_ref, cbk_ref)
    qpad[0:8, :] = qpad[lc:lc + 8, :]
    kpad[0:8, :] = kpad[lc:lc + 8, :]

    gt = (misc_ref[...] + gb_ref[...]).T
    ig = gt[IG_LANE:IG_LANE + 8, :]
    fpre = gt[FG_LANE:FG_LANE + 8, :]
    lf = jnp.minimum(fpre, 0.0) - jnp.log(1.0 + jnp.exp(-jnp.abs(fpre)))
    hi = lf.astype(BF16).astype(F32)
    r1 = lf - hi
    mid = r1.astype(BF16).astype(F32)
    lo = r1 - mid
    stack = jnp.concatenate([hi, mid, lo, jnp.zeros_like(lo)], axis=0).astype(BF16)
    srow = lax.broadcasted_iota(jnp.int32, (lc, lc), 0)
    scol = lax.broadcasted_iota(jnp.int32, (lc, lc), 1)
    upper = (srow <= scol).astype(BF16)
    cs = _dot(stack, upper)
    b = cs[0:8] + cs[8:16] + cs[16:24]

    r = ig - b
    cm = _cummax_lanes(r, lc)
    mprev = m_scr[:, 0:1]
    mx = jnp.maximum(mprev, cm)
    w_inter = jnp.exp(mprev - mx)
    e_neg = jnp.exp(-(b + mx))
    total = b[:, lc - 1:lc]
    mnew = total + mx[:, lc - 1:lc]
    a_prev = jnp.exp(total + mprev - mnew)
    w_in = jnp.exp(total + r - mnew)
    m_scr[...] = jnp.broadcast_to(mnew, (8, LANES))

    cols = jnp.concatenate([-mx, w_inter, e_neg, w_in, jnp.zeros((LANES - 32, lc), F32)], axis=0).T
    tril = scol <= srow

    heads = [slice(h * M_HEAD_DIM, (h + 1) * M_HEAD_DIM) for h in range(M_HEADS)]
    qbs = [qc[:, sl].astype(BF16) for sl in heads]
    qk = [_dot_nt(qbs[h], kc[:, heads[h]].astype(BF16)) for h in range(M_HEADS)]
    qcs = [_dot(qbs[h], c_scr[h].astype(BF16)) for h in range(M_HEADS)]
    for h in range(M_HEADS):
        sl = heads[h]
        qh = qc[:, sl]
        kh = kc[:, sl]
        vh = v_ref[:, sl]
        u_col = cols[:, h:h + 1]
        wi_col = cols[:, 8 + h:9 + h]
        en_col = cols[:, 16 + h:17 + h]
        win_col = cols[:, 24 + h:25 + h]
        dmat = jnp.where(tril, u_col + r[h:h + 1, :], NEG_BIG)
        p = jnp.exp(dmat) * qk[h]
        n_h = n_scr[h:h + 1, :]
        num = wi_col * qcs[h] + _dot(p.astype(BF16), vh)
        den = wi_col * jnp.sum(qh * n_h, axis=-1, keepdims=True) + jnp.sum(p, axis=-1, keepdims=True)
        ht = num / jnp.maximum(jnp.abs(den), en_col)
        kw = kh * win_col
        ah = a_prev[h:h + 1, :]
        c_scr[h] = ah * c_scr[h] + _dot_tn(kw.astype(BF16), vh)
        n_scr[h:h + 1, :] = ah * n_h + jnp.sum(kw, axis=0, keepdims=True)
        hn = _rms(ht, hg_ref[:, sl])
        out_ref[:, sl] = (_sigmoid(og_ref[:, sl].astype(F32)) * hn).astype(BF16)


def _mlstm(y, misc, cwq, cwk, cbq, cbk, gate_bias, hg, bsz, seq, lc):
    t = bsz * seq
    nch = seq // lc
    row = lambda b, c: b * nch + c
    return pl.pallas_call(
        functools.partial(_mlstm_kernel, lc=lc),
        grid=(bsz, nch),
        in_specs=[pl.BlockSpec((lc, M_WIDTH), lambda b, c: (row(b, c), 0)),
                  pl.BlockSpec((lc, M_WIDTH), lambda b, c: (row(b, c), 1)),
                  pl.BlockSpec((lc, M_WIDTH), lambda b, c: (row(b, c), 2)),
                  pl.BlockSpec((lc, M_WIDTH), lambda b, c: (row(b, c), 3)),
                  pl.BlockSpec((lc, LANES), lambda b, c: (row(b, c), 0)),
                  _const_spec((CONV_WIDTH, M_WIDTH)), _const_spec((CONV_WIDTH, M_WIDTH)),
                  _const_spec((1, M_WIDTH)), _const_spec((1, M_WIDTH)),
                  _const_spec((1, LANES)), _const_spec((1, M_WIDTH))],
        out_specs=pl.BlockSpec((lc, M_WIDTH), lambda b, c: (row(b, c), 0)),
        out_shape=jax.ShapeDtypeStruct((t, M_WIDTH), BF16),
        scratch_shapes=[pltpu.VMEM((lc + 8, M_WIDTH), F32), pltpu.VMEM((lc + 8, M_WIDTH), F32),
                        pltpu.VMEM((M_HEADS, M_HEAD_DIM, M_HEAD_DIM), F32),
                        pltpu.VMEM((8, M_HEAD_DIM), F32), pltpu.VMEM((8, LANES), F32)],
        compiler_params=_cparams("parallel", "arbitrary"),
        name="mlstm",
    )(y, y, y, y, misc, cwq, cwk, cbq, cbk, gate_bias, hg)


def _mla_prep_kernel(cq_ref, ckv_ref, misc_ref, cos_ref, sin_ref, cqg_ref, wuq_ref, wuqs_ref, ckvg_ref, wuk_ref,
                     wuv_ref, swap_ref, qng_ref, qngs_ref, kng_ref, kngs_ref, q_ref, k_ref, v_ref):
    cqn = _rms(cq_ref[...].astype(F32), cqg_ref[...]).astype(BF16)
    qa = _dot(cqn, wuq_ref[...])
    qs = _dot(cqn, wuqs_ref[...])
    ckvn = _rms(ckv_ref[...].astype(F32), ckvg_ref[...]).astype(BF16)
    ka = _dot(ckvn, wuk_ref[...])
    va = _dot(ckvn, wuv_ref[...])
    lane = lax.broadcasted_iota(jnp.int32, misc_ref.shape, 1)
    kr = jnp.where((lane >= KR_LANE) & (lane < KR_LANE + QK_ROPE), misc_ref[...], 0.0)
    kr_hi = kr.astype(BF16)
    kr_lo = (kr - kr_hi.astype(F32)).astype(BF16)
    kr_swap = _dot(kr_hi, swap_ref[...]) + _dot(kr_lo, swap_ref[...])
    cos = cos_ref[...]
    sin = sin_ref[...]
    q_cos = qng_ref[...] * cos
    q_sin = qngs_ref[...] * sin
    k_cos = kng_ref[...] * cos
    k_rot = kr_swap * (kngs_ref[...] * sin)

    def inv_rms(xh):
        return lax.rsqrt(jnp.sum(xh * xh, axis=-1, keepdims=True) * (1.0 / QK_HEAD) + NORM_EPS)

    for h in range(A_HEADS):
        sl = slice(h * HEAD_PAD, (h + 1) * HEAD_PAD)
        qh = qa[:, sl]
        kh = ka[:, sl] + kr
        q_ref[0, h, 0] = ((qh * q_cos + qs[:, sl] * q_sin)
                          * (inv_rms(qh) * (LOG2_E * QK_HEAD ** -0.5))).T.astype(BF16)
        k_ref[0, h] = ((kh * k_cos + k_rot) * inv_rms(kh)).astype(BF16)
        v_ref[0, h, 0] = jnp.where(lane < V_HEAD, va[:, sl], 1.0).T.astype(BF16)


def _mla_prep(y, misc, cos_t, sin_s, cqg, wuq, wuq_swap, ckvg, wuk, wuv, swap, qng, qng_swap, kng, kng_swap,
              bsz, seq, ts):
    t = bsz * seq
    nt = seq // ts
    row = lambda b, i: (b * nt + i, 0)
    cq_blk = (4 * M_WIDTH) // Q_LORA
    ckv_blk = (4 * M_WIDTH + Q_LORA) // KV_LORA
    hw = A_HEADS * HEAD_PAD
    return pl.pallas_call(
        _mla_prep_kernel,
        grid=(bsz, nt),
        in_specs=[pl.BlockSpec((ts, Q_LORA), lambda b, i: (b * nt + i, cq_blk)),
                  pl.BlockSpec((ts, KV_LORA), lambda b, i: (b * nt + i, ckv_blk)),
                  pl.BlockSpec((ts, LANES), row), pl.BlockSpec((ts, LANES), row), pl.BlockSpec((ts, LANES), row),
                  _const_spec((1, Q_LORA)), _const_spec((Q_LORA, hw)), _const_spec((Q_LORA, hw)),
                  _const_spec((1, KV_LORA)), _const_spec((KV_LORA, hw)), _const_spec((KV_LORA, hw)),
                  _const_spec((HEAD_PAD, HEAD_PAD)),
                  _const_spec((1, HEAD_PAD)), _const_spec((1, HEAD_PAD)),
                  _const_spec((1, HEAD_PAD)), _const_spec((1, HEAD_PAD))],
        out_specs=[pl.BlockSpec((1, A_HEADS, 1, HEAD_PAD, ts), lambda b, i: (b, 0, i, 0, 0)),
                   pl.BlockSpec((1, A_HEADS, ts, HEAD_PAD), lambda b, i: (b, 0, i, 0)),
                   pl.BlockSpec((1, A_HEADS, 1, HEAD_PAD, ts), lambda b, i: (b, 0, i, 0, 0))],
        out_shape=[jax.ShapeDtypeStruct((bsz, A_HEADS, nt, HEAD_PAD, ts), BF16),
                   jax.ShapeDtypeStruct((bsz, A_HEADS, seq, HEAD_PAD), BF16),
                   jax.ShapeDtypeStruct((bsz, A_HEADS, nt, HEAD_PAD, ts), BF16)],
        compiler_params=_cparams("parallel", "parallel"),
        name="mla_prep",
    )(y, y, misc, cos_t, sin_s, cqg, wuq, wuq_swap, ckvg, wuk, wuv, swap, qng, qng_swap, kng, kng_swap)


def _attn_kernel(q_ref, k_ref, v_ref, o_ref, *scratch, tq):
    m_refs, acc_refs = scratch[:A_HEADS], scratch[A_HEADS:]
    i = pl.program_id(1)
    key = lax.broadcasted_iota(jnp.int32, (tq, tq), 0)
    qry = lax.broadcasted_iota(jnp.int32, (tq, tq), 1)
    dmask = (key // MLA_CHUNK) <= (qry // MLA_CHUNK)
    for h in range(A_HEADS):
        m_refs[h][...] = jnp.full((1, tq), NEG_BIG, F32)
        acc_refs[h][...] = jnp.zeros((HEAD_PAD, tq), F32)

    def update(kt, masked):
        off = pl.multiple_of(kt * tq, tq)
        scores = [_dot(k_ref[0, h, pl.ds(off, tq), :], q_ref[0, h, 0]) for h in range(A_HEADS)]
        for h in range(A_HEADS):
            st = scores[h]
            if masked:
                st = jnp.where(dmask, st, NEG_BIG)
            m = m_refs[h][...]
            mn = jnp.maximum(m, jnp.max(st, axis=0, keepdims=True))
            p = jnp.exp2(st - mn)
            acc_refs[h][...] = jnp.exp2(m - mn) * acc_refs[h][...] + _dot(v_ref[0, h, kt], p.astype(BF16))
            m_refs[h][...] = mn

    def body(kt, carry):
        update(kt, False)
        return carry

    lax.fori_loop(0, i, body, 0)
    update(i, True)
    for hp in range(A_HEADS // 2):
        a0 = acc_refs[2 * hp][...]
        a1 = acc_refs[2 * hp + 1][...]
        pair = jnp.concatenate([a0[:V_HEAD] / a0[V_HEAD:V_HEAD + 1], a1[:V_HEAD] / a1[V_HEAD:V_HEAD + 1]], axis=0)
        o_ref[:, hp * LANES:(hp + 1) * LANES] = pair.T.astype(BF16)


def _attention(q_t, k, v_t, bsz, seq, tq):
    nq = seq // tq
    return pl.pallas_call(
        functools.partial(_attn_kernel, tq=tq),
        grid=(bsz, nq),
        in_specs=[pl.BlockSpec((1, A_HEADS, 1, HEAD_PAD, tq), lambda b, i: (b, 0, i, 0, 0)),
                  pl.BlockSpec((1, A_HEADS, seq, HEAD_PAD), lambda b, i: (b, 0, 0, 0)),
                  pl.BlockSpec((1, A_HEADS, nq, HEAD_PAD, tq), lambda b, i: (b, 0, 0, 0, 0))],
        out_specs=pl.BlockSpec((tq, A_WIDTH), lambda b, i: (b * nq + i, 0)),
        out_shape=jax.ShapeDtypeStruct((bsz * seq, A_WIDTH), BF16),
        scratch_shapes=([pltpu.VMEM((1, tq), F32)] * A_HEADS + [pltpu.VMEM((HEAD_PAD, tq), F32)] * A_HEADS),
        compiler_params=_cparams("parallel", "arbitrary"),
        name="mla_attention",
    )(q_t, k, v_t)


def _outproj_ffn_kernel(m_ref, a_ref, x_ref, wom_ref, woa_ref, g_ref, w1_ref, w3_ref, w2_ref, o_ref):
    x1 = x_ref[...] + _dot(m_ref[...], wom_ref[...]) + _dot(a_ref[...], woa_ref[...])
    hn = _rms(x1, g_ref[...]).astype(BF16)
    a = _dot(hn, w1_ref[...])
    b = _dot(hn, w3_ref[...])
    o_ref[...] = x1 + _dot((a * _sigmoid(a) * b).astype(BF16), w2_ref[...])


def _outproj_ffn(m_out, a_out, x, wom, woa, g, w1, w3, w2, tm):
    t = x.shape[0]
    return pl.pallas_call(
        _outproj_ffn_kernel,
        grid=(t // tm,),
        in_specs=[pl.BlockSpec((tm, M_WIDTH), lambda i: (i, 0)),
                  pl.BlockSpec((tm, A_WIDTH), lambda i: (i, 0)),
                  pl.BlockSpec((tm, D_MODEL), lambda i: (i, 0)),
                  _const_spec((M_WIDTH, D_MODEL)), _const_spec((A_WIDTH, D_MODEL)), _const_spec((1, D_MODEL)),
                  _const_spec((D_MODEL, D_FF)), _const_spec((D_MODEL, D_FF)), _const_spec((D_FF, D_MODEL))],
        out_specs=pl.BlockSpec((tm, D_MODEL), lambda i: (i, 0)),
        out_shape=jax.ShapeDtypeStruct((t, D_MODEL), F32),
        compiler_params=_cparams("parallel"),
        name="outproj_ffn",
    )(m_out, a_out, x, wom, woa, g, w1, w3, w2)


S5_CHUNK = 16
S5_ROW = S5_CHUNK * S5_GROUP


GROUPS_PER_TILE = LANES // S5_GROUP
RELAYOUT_ROWS = 32


def _lane_group_masks(rows):
    grp = lax.broadcasted_iota(jnp.int32, (rows, LANES), 1) // S5_GROUP
    return [grp == g for g in range(GROUPS_PER_TILE)]


def _to_group_major(vs, masks):
    n = GROUPS_PER_TILE
    rolled = []
    for r in range(n):
        d = vs[r % n]
        for g in range(1, n):
            d = jnp.where(masks[g], vs[(g + r) % n], d)
        rolled.append(d if r == 0 else pltpu.roll(d, r * S5_GROUP, axis=1))
    outs = []
    for grp in range(n):
        o = rolled[(-grp) % n]
        for w in range(1, n):
            o = jnp.where(masks[w], rolled[(w - grp) % n], o)
        outs.append(o)
    return outs


def _to_token_major(outs, masks):
    n = GROUPS_PER_TILE
    rolled = []
    for r in range(n):
        d = outs[(-r) % n]
        for w in range(1, n):
            d = jnp.where(masks[w], outs[(w - r) % n], d)
        rolled.append(d if r == 0 else pltpu.roll(d, LANES - r * S5_GROUP, axis=1))
    vs = []
    for w in range(n):
        v = rolled[w]
        for g in range(1, n):
            v = jnp.where(masks[g], rolled[(w - g) % n], v)
        vs.append(v)
    return vs


def _relayout_steps(tile_rows):
    nrow = tile_rows // S5_CHUNK
    return [(j, k, rg) for j in range(D_MODEL // LANES) for k in range(S5_CHUNK // GROUPS_PER_TILE)
            for rg in range(nrow // RELAYOUT_ROWS)]


def _norm_group_major_kernel(x_ref, g_ref, u_ref, un_scr):
    un = _rms(x_ref[...], g_ref[...])
    for j in range(D_MODEL // LANES):
        un_scr[j] = un[:, j * LANES:(j + 1) * LANES]
    masks = _lane_group_masks(RELAYOUT_ROWS)
    n = GROUPS_PER_TILE
    for j, k, rg in _relayout_steps(x_ref.shape[0]):
        rows = slice(rg * RELAYOUT_ROWS, (rg + 1) * RELAYOUT_ROWS)
        first = rg * RELAYOUT_ROWS * S5_CHUNK + k * n
        vs = [un_scr[j, pl.ds(first + w, RELAYOUT_ROWS, stride=S5_CHUNK), :] for w in range(n)]
        outs = _to_group_major(vs, masks)
        for grp in range(n):
            u_ref[j * n + grp, rows, k * LANES:(k + 1) * LANES] = outs[grp].astype(BF16)


def _norm_group_major(x, g, tm):
    t = x.shape[0]
    nrow = tm // S5_CHUNK
    return pl.pallas_call(
        _norm_group_major_kernel,
        grid=(t // tm,),
        in_specs=[pl.BlockSpec((tm, D_MODEL), lambda i: (i, 0)), _const_spec((1, D_MODEL))],
        out_specs=pl.BlockSpec((S5_GROUPS, nrow, S5_ROW), lambda i: (0, i, 0)),
        out_shape=jax.ShapeDtypeStruct((S5_GROUPS, t // S5_CHUNK, S5_ROW), BF16),
        scratch_shapes=[pltpu.VMEM((D_MODEL // LANES, tm, LANES), F32)],
        compiler_params=_cparams("parallel"),
        name="norm_group_major",
    )(x, g)


def _gelu_tanh(x):
    return 0.5 * x * (1.0 + jnp.tanh(math.sqrt(2.0 / math.pi) * (x + 0.044715 * (x * x * x))))


def _s5_kernel(u_ref, w_ref, cst_ref, a1_ref, a2_ref, dk_ref, z_ref, vv_scr, xp_scr, *, nch, bsz):
    two_p = 2 * S5_STATE
    u = u_ref[0]
    r = _dot(u, w_ref[0])
    vv_scr[0] = r[:, S5_ROW:S5_ROW + two_p]
    vv_scr[1] = r[:, S5_ROW + two_p:]
    a1 = a1_ref[0]
    a2 = a2_ref[0]

    def body(c, carry):
        p, q = carry
        rows = pl.ds(c, bsz, stride=nch)
        xp_scr[rows, :] = p
        pn = a1 * p + a2 * q + vv_scr[0, rows, :]
        qn = a1 * q - a2 * p + vv_scr[1, rows, :]
        return pn, qn

    zero = jnp.zeros((bsz, two_p), F32)
    lax.fori_loop(0, nch, body, (zero, zero), unroll=4)
    y = r[:, :S5_ROW] + _dot(xp_scr[...].astype(BF16), cst_ref[0]) + dk_ref[0] * u.astype(F32)
    z_ref[0] = _gelu_tanh(y).astype(BF16)


def _s5(u_g, w_all, cst, a1, a2, dk, nch, bsz):
    g, n, _ = u_g.shape
    two_p = 2 * S5_STATE
    wcols = S5_ROW + 2 * two_p
    return pl.pallas_call(
        functools.partial(_s5_kernel, nch=nch, bsz=bsz),
        grid=(g,),
        in_specs=[pl.BlockSpec((1, n, S5_ROW), lambda i: (i, 0, 0)),
                  pl.BlockSpec((1, S5_ROW, wcols), lambda i: (i, 0, 0)),
                  pl.BlockSpec((1, two_p, S5_ROW), lambda i: (i, 0, 0)),
                  pl.BlockSpec((1, 1, two_p), lambda i: (i, 0, 0)),
                  pl.BlockSpec((1, 1, two_p), lambda i: (i, 0, 0)),
                  pl.BlockSpec((1, 1, S5_ROW), lambda i: (i, 0, 0))],
        out_specs=pl.BlockSpec((1, n, S5_ROW), lambda i: (i, 0, 0)),
        out_shape=jax.ShapeDtypeStruct((g, n, S5_ROW), BF16),
        scratch_shapes=[pltpu.VMEM((2, n, two_p), F32), pltpu.VMEM((n, two_p), F32)],
        compiler_params=_cparams("parallel"),
        name="s5",
    )(u_g, w_all, cst, a1, a2, dk)


def _s5_weights(lam_re, lam_im, log_dt, b_re, b_im, c_re, c_im, d_skip):
    lc, h, p = S5_CHUNK, S5_GROUP, S5_STATE
    lam = lax.complex(jnp.minimum(lam_re.astype(F32), -1e-4), lam_im.astype(F32))
    dt = jnp.exp(log_dt.astype(F32))[:, None]
    ldt = lam * dt
    a_bar = jnp.exp(ldt)
    b_bar = ((a_bar - 1.0) / lam)[..., None] * lax.complex(b_re.astype(F32), b_im.astype(F32))
    c_mat = lax.complex(c_re.astype(F32), c_im.astype(F32))
    steps = jnp.arange(lc + 1, dtype=F32)
    apow = jnp.exp(ldt[None] * steps[:, None, None].astype(ldt.dtype))
    g = lam.shape[0]
    ker = jnp.einsum('gip,kgp,gpj->gkij', c_mat, apow[:lc], b_bar).real
    lag = np.arange(lc)[None, :] - np.arange(lc)[:, None]
    toe = jnp.where((lag >= 0)[None, :, :, None, None], ker[:, np.clip(lag, 0, lc - 1)], 0.0)
    toe = toe.transpose(0, 1, 4, 2, 3).reshape(g, lc * h, lc * h)
    wst = apow[:lc][::-1].transpose(1, 0, 2)[:, :, None, :] * jnp.swapaxes(b_bar, 1, 2)[:, None, :, :]
    wst = wst.reshape(g, lc * h, p)
    w_all = jnp.concatenate([toe, wst.real, wst.imag, wst.imag, wst.real], axis=-1).astype(BF16)
    ca = c_mat[:, None, :, :] * apow[1:lc + 1].transpose(1, 0, 2)[:, :, None, :]
    ca = ca.reshape(g, lc * h, p)
    cst = jnp.concatenate([ca.real, -ca.imag], axis=-1).transpose(0, 2, 1).astype(BF16)
    al = apow[lc]
    a1 = jnp.concatenate([al.real, al.real], axis=-1)[:, None, :]
    a2 = jnp.concatenate([-al.imag, al.imag], axis=-1)[:, None, :]
    dk = jnp.tile(d_skip.astype(F32).reshape(g, 1, h), (1, lc, 1)).reshape(g, 1, lc * h)
    return w_all, cst, a1, a2, dk


def _glu_kernel(z_ref, x_ref, wv_ref, wg_ref, g_ref, x2_ref, xn_ref, zt_scr):
    masks = _lane_group_masks(RELAYOUT_ROWS)
    n = GROUPS_PER_TILE
    for j, k, rg in _relayout_steps(x_ref.shape[0]):
        rows = slice(rg * RELAYOUT_ROWS, (rg + 1) * RELAYOUT_ROWS)
        outs = [z_ref[j * n + grp, rows, k * LANES:(k + 1) * LANES].astype(F32) for grp in range(n)]
        vs = _to_token_major(outs, masks)
        first = rg * RELAYOUT_ROWS * S5_CHUNK + k * n
        for w in range(n):
            zt_scr[j, pl.ds(first + w, RELAYOUT_ROWS, stride=S5_CHUNK), :] = vs[w]
    z = jnp.concatenate([zt_scr[j] for j in range(D_MODEL // LANES)], axis=1).astype(BF16)
    x2 = x_ref[...] + _dot(z, wv_ref[...]) * _sigmoid(_dot(z, wg_ref[...]))
    x2_ref[...] = x2
    xn_ref[...] = _rms(x2, g_ref[...]).astype(BF16)


def _glu(z_g, x, wv, wg, g, tm):
    t = x.shape[0]
    nrow = tm // S5_CHUNK
    return pl.pallas_call(
        _glu_kernel,
        grid=(t // tm,),
        in_specs=[pl.BlockSpec((S5_GROUPS, nrow, S5_ROW), lambda i: (0, i, 0)),
                  pl.BlockSpec((tm, D_MODEL), lambda i: (i, 0)),
                  _const_spec((D_MODEL, D_MODEL)), _const_spec((D_MODEL, D_MODEL)), _const_spec((1, D_MODEL))],
        out_specs=[pl.BlockSpec((tm, D_MODEL), lambda i: (i, 0)), pl.BlockSpec((tm, D_MODEL), lambda i: (i, 0))],
        out_shape=[jax.ShapeDtypeStruct((t, D_MODEL), F32), jax.ShapeDtypeStruct((t, D_MODEL), BF16)],
        scratch_shapes=[pltpu.VMEM((D_MODEL // LANES, tm, LANES), F32)],
        compiler_params=_cparams("parallel"),
        name="glu",
    )(z_g, x, wv, wg, g)


def _router_kernel(xn_ref, wh_ref, wl_ref, pos_ref, gate_ref, posc_ref, cnt_ref, *, tb):
    xn = xn_ref[...]
    lg = (_dot_nt(wh_ref[...], xn) + _dot_nt(wl_ref[...], xn))[0:N_EXPERTS]
    eidx = lax.broadcasted_iota(jnp.int32, (N_EXPERTS, tb), 0)
    m1 = jnp.max(lg, axis=0, keepdims=True)
    i1 = jnp.min(jnp.where(lg == m1, eidx, N_EXPERTS), axis=0, keepdims=True)
    rest = jnp.where(eidx == i1, -jnp.inf, lg)
    m2 = jnp.max(rest, axis=0, keepdims=True)
    i2 = jnp.min(jnp.where(rest == m2, eidx, N_EXPERTS), axis=0, keepdims=True)
    e21 = jnp.exp(m2 - m1)
    p1 = 1.0 / (1.0 + e21)
    sel1 = eidx == i1
    sel2 = eidx == i2
    sel = sel1 | sel2
    gate_ref[...] = jnp.where(sel1, p1, jnp.where(sel2, e21 * p1, 0.0))
    onehot = jnp.concatenate([sel.astype(F32), jnp.zeros((N_EXPERTS, tb), F32)], axis=0).astype(BF16)
    srow = lax.broadcasted_iota(jnp.int32, (tb, tb), 0)
    scol = lax.broadcasted_iota(jnp.int32, (tb, tb), 1)
    rank = _dot(onehot, (srow < scol).astype(BF16))[0:N_EXPERTS]
    pos = jnp.where(sel, rank, -1.0)
    pos_ref[...] = pos
    posc_ref[...] = jnp.concatenate([pos, jnp.full((LANES - N_EXPERTS, tb), -1.0, F32)], axis=0).T
    cnt = jnp.sum(sel.astype(F32), axis=1, keepdims=True)
    cnt_ref[0] = jnp.broadcast_to(cnt, (N_EXPERTS, LANES))


def _router(xn, wr_hi, wr_lo, tb):
    t = xn.shape[0]
    nb = t // tb
    return pl.pallas_call(
        functools.partial(_router_kernel, tb=tb),
        grid=(nb,),
        in_specs=[pl.BlockSpec((tb, D_MODEL), lambda i: (i, 0)),
                  _const_spec((2 * N_EXPERTS, D_MODEL)), _const_spec((2 * N_EXPERTS, D_MODEL))],
        out_specs=[pl.BlockSpec((N_EXPERTS, tb), lambda i: (0, i)), pl.BlockSpec((N_EXPERTS, tb), lambda i: (0, i)),
                   pl.BlockSpec((tb, LANES), lambda i: (i, 0)),
                   pl.BlockSpec((1, N_EXPERTS, LANES), lambda i: (i, 0, 0))],
        out_shape=[jax.ShapeDtypeStruct((N_EXPERTS, t), F32), jax.ShapeDtypeStruct((N_EXPERTS, t), F32),
                   jax.ShapeDtypeStruct((t, LANES), F32), jax.ShapeDtypeStruct((nb, N_EXPERTS, LANES), F32)],
        compiler_params=_cparams("parallel"),
        name="router",
    )(xn, wr_hi, wr_lo)


MOE_UNIT = 128
MOE_ALIGN = 16
MOE_TM = 512
MOE_FF_SPLIT = 2


def _moe_plan(cnt, n_tokens):
    nb = cnt.shape[0]
    seg = (cnt + MOE_ALIGN - 1) // MOE_ALIGN * MOE_ALIGN
    used = jnp.sum(seg, axis=0)
    gsize = (used + MOE_UNIT + MOE_TM - 1) // MOE_TM * MOE_TM
    gstart = jnp.cumsum(gsize) - gsize
    dest = gstart[None, :] + jnp.cumsum(seg, axis=0) - seg
    max_rows = 2 * n_tokens + N_EXPERTS * (nb * (MOE_ALIGN - 1) + MOE_UNIT + MOE_TM - 1)
    n_tiles = -(-max_rows // MOE_TM)
    tile_start = jnp.arange(n_tiles, dtype=jnp.int32) * MOE_TM
    tile_e = jnp.minimum(jnp.sum(tile_start[:, None] >= (gstart + gsize)[None, :], axis=1), N_EXPERTS - 1)
    tile_valid = tile_start < (gstart + used)[tile_e]
    return dest.reshape(-1).astype(jnp.int32), tile_e.astype(jnp.int32), tile_valid.astype(jnp.int32), n_tiles


def _moe_pack_kernel(dest_ref, cnt_ref, xn_ref, pos_ref, xs_in_ref, xs_ref, stage, sem, *, tb):
    del xs_in_ref
    b = pl.program_id(0)
    unit = MOE_UNIT

    def n_units(e):
        return (cnt_ref[b * N_EXPERTS + e] + unit - 1) // unit

    def chunk_copy(e, r):
        dst = pl.multiple_of(dest_ref[b * N_EXPERTS + e] + r * unit, MOE_ALIGN)
        return pltpu.make_async_copy(stage.at[e % 2, pl.ds(pl.multiple_of(r * unit, unit), unit)],
                                     xs_ref.at[pl.ds(dst, unit)], sem.at[e % 2])

    def wait_expert(e):
        def body(r, carry):
            chunk_copy(e, r).wait()
            return carry
        lax.fori_loop(0, n_units(e), body, 0)

    for e in range(N_EXPERTS):
        if e >= 2:
            wait_expert(e - 2)
        prow = pos_ref[e:e + 1, :]
        n_unit = n_units(e)
        n_pair = n_unit // 2

        def gather(base, m, e=e, prow=prow):
            slot = lax.broadcasted_iota(jnp.int32, (m, tb), 0).astype(F32) + base.astype(F32)
            stage[e % 2, pl.ds(base, m), :] = _dot((prow == slot).astype(BF16), xn_ref[...]).astype(BF16)

        def pair(r, carry, e=e, gather=gather):
            gather(pl.multiple_of(r * (2 * unit), 2 * unit), 2 * unit)
            chunk_copy(e, 2 * r).start()
            chunk_copy(e, 2 * r + 1).start()
            return carry

        lax.fori_loop(0, n_pair, pair, 0)

        @pl.when(n_unit % 2 == 1)
        def _(e=e, gather=gather, n_pair=n_pair):
            gather(pl.multiple_of(n_pair * (2 * unit), 2 * unit), unit)
            chunk_copy(e, 2 * n_pair).start()
    wait_expert(N_EXPERTS - 2)
    wait_expert(N_EXPERTS - 1)


def _moe_pack(dest, cnt, xn, pos, n_rows, tb):
    t = xn.shape[0]
    grid_spec = pltpu.PrefetchScalarGridSpec(
        num_scalar_prefetch=2,
        grid=(t // tb,),
        in_specs=[pl.BlockSpec((tb, D_MODEL), lambda b, d, c: (b, 0)),
                  pl.BlockSpec((N_EXPERTS, tb), lambda b, d, c: (0, b)),
                  pl.BlockSpec(memory_space=pl.ANY)],
        out_specs=pl.BlockSpec(memory_space=pl.ANY),
        scratch_shapes=[pltpu.VMEM((2, tb, D_MODEL), BF16), pltpu.SemaphoreType.DMA((2,))],
    )
    return pl.pallas_call(
        functools.partial(_moe_pack_kernel, tb=tb),
        grid_spec=grid_spec,
        out_shape=jax.ShapeDtypeStruct((n_rows, D_MODEL), BF16),
        input_output_aliases={4: 0},
        compiler_params=_cparams("arbitrary"),
        name="moe_pack",
    )(dest, cnt, xn, pos, jnp.zeros((n_rows, D_MODEL), BF16))


def _moe_ffn_kernel(te_ref, tv_ref, xs_ref, w1_ref, w3_ref, w2_ref, y_ref, acc_ref):
    del te_ref
    i = pl.program_id(0)
    f = pl.program_id(1)
    valid = tv_ref[i] != 0

    @pl.when(valid)
    def _():
        xe = xs_ref[...]
        a = _dot(xe, w1_ref[0])
        g = (a * _sigmoid(a) * _dot(xe, w3_ref[0])).astype(BF16)
        y = _dot(g, w2_ref[0])

        @pl.when(f == 0)
        def _():
            acc_ref[...] = y

        @pl.when(f != 0)
        def _():
            acc_ref[...] += y

    @pl.when(f == MOE_FF_SPLIT - 1)
    def _():
        @pl.when(valid)
        def _():
            y_ref[...] = acc_ref[...].astype(BF16)

        @pl.when(jnp.logical_not(valid))
        def _():
            y_ref[...] = jnp.zeros(y_ref.shape, BF16)


def _moe_ffn(tile_e, tile_valid, xs, w1, w3, w2, n_tiles):
    fs = D_FF_EXPERT // MOE_FF_SPLIT
    grid_spec = pltpu.PrefetchScalarGridSpec(
        num_scalar_prefetch=2,
        grid=(n_tiles, MOE_FF_SPLIT),
        in_specs=[pl.BlockSpec((MOE_TM, D_MODEL), lambda i, f, te, tv: (i, 0)),
                  pl.BlockSpec((1, D_MODEL, fs), lambda i, f, te, tv: (te[i], 0, f)),
                  pl.BlockSpec((1, D_MODEL, fs), lambda i, f, te, tv: (te[i], 0, f)),
                  pl.BlockSpec((1, fs, D_MODEL), lambda i, f, te, tv: (te[i], f, 0))],
        out_specs=pl.BlockSpec((MOE_TM, D_MODEL), lambda i, f, te, tv: (i, 0)),
        scratch_shapes=[pltpu.VMEM((MOE_TM, D_MODEL), F32)],
    )
    return pl.pallas_call(
        _moe_ffn_kernel,
        grid_spec=grid_spec,
        out_shape=jax.ShapeDtypeStruct((n_tiles * MOE_TM, D_MODEL), BF16),
        compiler_params=_cparams("parallel", "arbitrary"),
        name="moe_ffn",
    )(tile_e, tile_valid, xs, w1, w3, w2)


def _moe_combine_kernel(dest_ref, cnt_ref, x_ref, pos_ref, gate_ref, posc_ref, ys_ref, o_ref, ybuf, sem, *, tb):
    b = pl.program_id(0)
    unit = MOE_UNIT

    def n_units(e):
        return (cnt_ref[b * N_EXPERTS + e] + unit - 1) // unit

    def chunk_copy(e, r):
        src = pl.multiple_of(dest_ref[b * N_EXPERTS + e] + r * unit, MOE_ALIGN)
        return pltpu.make_async_copy(ys_ref.at[pl.ds(src, unit)],
                                     ybuf.at[e % 2, pl.ds(pl.multiple_of(r * unit, unit), unit)], sem.at[e % 2])

    def start_expert(e):
        def body(r, carry):
            chunk_copy(e, r).start()
            return carry
        lax.fori_loop(0, n_units(e), body, 0)

    def wait_expert(e):
        def body(r, carry):
            chunk_copy(e, r).wait()
            return carry
        lax.fori_loop(0, n_units(e), body, 0)

    start_expert(0)
    o_ref[...] = x_ref[...]
    lane = lax.broadcasted_iota(jnp.int32, (tb, LANES), 1)
    for e in range(N_EXPERTS):
        if e + 1 < N_EXPERTS:
            start_expert(e + 1)
        wait_expert(e)
        prow = pos_ref[e:e + 1, :]
        grow = gate_ref[e:e + 1, :]
        n_unit = n_units(e)
        n_pair = n_unit // 2

        def scatter_chunk(base, m, e=e, prow=prow, grow=grow):
            hit = prow == lax.broadcasted_iota(jnp.int32, (m, tb), 0).astype(F32) + base.astype(F32)
            gs = jnp.sum(jnp.where(hit, grow, 0.0), axis=-1, keepdims=True)
            yg = (ybuf[e % 2, pl.ds(base, m), :].astype(F32) * gs).astype(BF16)
            o_ref[...] += _dot_tn(hit.astype(BF16), yg)

        def scatter_pair(r, carry, scatter_chunk=scatter_chunk):
            scatter_chunk(pl.multiple_of(r * (2 * unit), 2 * unit), 2 * unit)
            return carry

        lax.fori_loop(0, n_pair, scatter_pair, 0)

        @pl.when(n_unit % 2 == 1)
        def _(scatter_chunk=scatter_chunk, n_pair=n_pair):
            scatter_chunk(pl.multiple_of(n_pair * (2 * unit), 2 * unit), unit)


def _moe_combine(dest, cnt, x, pos, gate, posc, ys, tb):
    t = x.shape[0]
    grid_spec = pltpu.PrefetchScalarGridSpec(
        num_scalar_prefetch=2,
        grid=(t // tb,),
        in_specs=[pl.BlockSpec((tb, D_MODEL), lambda b, d, c: (b, 0)),
                  pl.BlockSpec((N_EXPERTS, tb), lambda b, d, c: (0, b)),
                  pl.BlockSpec((N_EXPERTS, tb), lambda b, d, c: (0, b)),
                  pl.BlockSpec((tb, LANES), lambda b, d, c: (b, 0)),
                  pl.BlockSpec(memory_space=pl.ANY)],
        out_specs=pl.BlockSpec((tb, D_MODEL), lambda b, d, c: (b, 0)),
        scratch_shapes=[pltpu.VMEM((2, tb, D_MODEL), BF16), pltpu.SemaphoreType.DMA((2,))],
    )
    return pl.pallas_call(
        functools.partial(_moe_combine_kernel, tb=tb),
        grid_spec=grid_spec,
        out_shape=jax.ShapeDtypeStruct((t, D_MODEL), F32),
        compiler_params=_cparams("parallel"),
        name="moe_combine",
    )(dest, cnt, x, pos, gate, posc, ys)


def _moe(cnt, xn, x, pos, gate, posc, w1, w3, w2, tb):
    t = x.shape[0]
    dest, tile_e, tile_valid, n_tiles = _moe_plan(cnt, t)
    cnt = cnt.reshape(-1)
    xs = _moe_pack(dest, cnt, xn, pos, n_tiles * MOE_TM, tb)
    ys = _moe_ffn(tile_e, tile_valid, xs, w1, w3, w2, n_tiles)
    return _moe_combine(dest, cnt, x, pos, gate, posc, ys, tb)


def _rope_tables(positions):
    half = QK_ROPE // 2
    inv_freq = ROPE_THETA ** (-jnp.arange(0, QK_ROPE, 2, dtype=F32) / QK_ROPE)
    ang = positions.astype(F32).reshape(-1, 1) * inv_freq
    cos, sin = jnp.cos(ang), jnp.sin(ang)
    t = ang.shape[0]
    ones_lo = jnp.ones((t, QK_NOPE), F32)
    zeros_lo = jnp.zeros((t, QK_NOPE), F32)
    tail1 = jnp.ones((t, HEAD_PAD - QK_HEAD), F32)
    tail0 = jnp.zeros((t, HEAD_PAD - QK_HEAD), F32)
    cos_t = jnp.concatenate([ones_lo, cos, cos, tail1], axis=-1)
    sin_s = jnp.concatenate([zeros_lo, -sin, sin, tail0], axis=-1)
    return cos_t, sin_s


def _pad_heads(w, per_head_in, keep, heads):
    k = w.shape[0]
    w = w.reshape(k, heads, per_head_in)[:, :, :keep]
    return jnp.pad(w, ((0, 0), (0, 0), (0, HEAD_PAD - keep))).reshape(k, heads * HEAD_PAD)


def _even_layer(x, rope, bsz, seq, norm1_g, w_in, conv_w, conv_b, b_igate, b_fgate, mh_norm_g, cq_norm_g, w_uq,
                ckv_norm_g, w_ukv, qn_g, kn_g, w_o, norm2_g, ffn_w1, ffn_w3, ffn_w2):
    row = lambda v: v.astype(F32).reshape(1, -1)
    mw = M_WIDTH
    o_ig, o_fg, o_og = 3 * mw, 3 * mw + M_HEADS, 3 * mw + 2 * M_HEADS
    o_cq = o_og + mw
    o_ckv, o_kr = o_cq + Q_LORA, o_cq + Q_LORA + KV_LORA
    misc_w = jnp.zeros((D_MODEL, LANES), F32)
    misc_w = misc_w.at[:, IG_LANE:IG_LANE + M_HEADS].set(w_in[:, o_ig:o_fg])
    misc_w = misc_w.at[:, FG_LANE:FG_LANE + M_HEADS].set(w_in[:, o_fg:o_og])
    misc_w = misc_w.at[:, KR_LANE:KR_LANE + QK_ROPE].set(w_in[:, o_kr:o_kr + QK_ROPE])
    w_packed = jnp.concatenate([w_in[:, :3 * mw], w_in[:, o_og:o_cq], w_in[:, o_cq:o_ckv], w_in[:, o_ckv:o_kr],
                                misc_w], axis=-1).astype(BF16)
    gate_bias = jnp.zeros((1, LANES), F32)
    gate_bias = gate_bias.at[0, IG_LANE:IG_LANE + M_HEADS].set(b_igate.astype(F32))
    gate_bias = gate_bias.at[0, FG_LANE:FG_LANE + M_HEADS].set(b_fgate.astype(F32))

    tm = min(512, x.shape[0])
    y, misc = _inproj(x, row(norm1_g), w_packed, tm)

    lc = min(256, seq)
    m_out = _mlstm(y, misc, conv_w[:, :mw].astype(F32), conv_w[:, mw:].astype(F32), row(conv_b[:mw]),
                   row(conv_b[mw:]), gate_bias, row(mh_norm_g), bsz, seq, lc)

    wuq = _pad_heads(w_uq, QK_HEAD, QK_HEAD, A_HEADS).astype(BF16)
    wuk = _pad_heads(w_ukv, QK_NOPE + V_HEAD, QK_NOPE, A_HEADS).astype(BF16)
    wuv = _pad_heads(w_ukv.reshape(KV_LORA, A_HEADS, QK_NOPE + V_HEAD)[:, :, QK_NOPE:].reshape(KV_LORA, A_WIDTH),
                     V_HEAD, V_HEAD, A_HEADS).astype(BF16)
    pad_g = lambda g: jnp.pad(g.astype(F32), (0, HEAD_PAD - QK_HEAD)).reshape(1, HEAD_PAD)
    half = QK_ROPE // 2

    def swap_halves(v):
        r1, r2 = v[..., QK_NOPE:QK_NOPE + half], v[..., QK_NOPE + half:QK_HEAD]
        return jnp.concatenate([jnp.zeros_like(v[..., :QK_NOPE]), r2, r1], axis=-1)

    wuq_swap = _pad_heads(swap_halves(w_uq.reshape(Q_LORA, A_HEADS, QK_HEAD)).reshape(Q_LORA, A_HEADS * QK_HEAD),
                          QK_HEAD, QK_HEAD, A_HEADS).astype(BF16)
    swap = swap_halves(jnp.eye(HEAD_PAD, dtype=F32)[:, :QK_HEAD])
    swap = jnp.pad(swap, ((0, 0), (0, HEAD_PAD - QK_HEAD))).astype(BF16)
    ts = min(256, seq)
    q, k, v = _mla_prep(y, misc, rope[0], rope[1], row(cq_norm_g), wuq, wuq_swap, row(ckv_norm_g), wuk, wuv, swap,
                        pad_g(qn_g), pad_g(swap_halves(qn_g)), pad_g(kn_g), pad_g(swap_halves(kn_g)), bsz, seq, ts)
    a_out = _attention(q, k, v, bsz, seq, ts)

    return _outproj_ffn(m_out, a_out, x, w_o[:mw].astype(BF16), w_o[mw:].astype(BF16), row(norm2_g),
                        ffn_w1.astype(BF16), ffn_w3.astype(BF16), ffn_w2.astype(BF16), tm)


def _odd_layer(x, bsz, seq, norm1_g, lam_re, lam_im, log_dt, b_re, b_im, c_re, c_im, d_skip, glu_wv, glu_wg,
               norm2_g, router_w, moe_w1, moe_w3, moe_w2):
    row = lambda v: v.astype(F32).reshape(1, -1)
    t = x.shape[0]
    tm = min(512, t)
    nch = seq // S5_CHUNK
    tr = min(1024, seq)
    u_g = _norm_group_major(x, row(norm1_g), tr)
    w_all, cst, a1, a2, dk = _s5_weights(lam_re, lam_im, log_dt, b_re, b_im, c_re, c_im, d_skip)
    z_g = _s5(u_g, w_all, cst, a1, a2, dk, nch, bsz)
    x2, xn = _glu(z_g, x, glu_wv.astype(BF16), glu_wg.astype(BF16), row(norm2_g), tr)

    tb = min(1024, t)
    wr = jnp.pad(router_w.astype(F32).T, ((0, N_EXPERTS), (0, 0)))
    wr_hi = wr.astype(BF16)
    wr_lo = (wr - wr_hi.astype(F32)).astype(BF16)
    pos, gate, posc, cnt = _router(xn, wr_hi, wr_lo, tb)
    cnt = cnt[:, :, 0].astype(jnp.int32)
    return _moe(cnt, xn, x2, pos, gate, posc, moe_w1.astype(BF16), moe_w3.astype(BF16), moe_w2.astype(BF16), tb)


def kernel(x, positions, l0_norm1_g, l0_w_in, l0_conv_w, l0_conv_b, l0_b_igate, l0_b_fgate, l0_mh_norm_g, l0_cq_norm_g, l0_w_uq, l0_ckv_norm_g, l0_w_ukv, l0_qn_g, l0_kn_g, l0_w_o, l0_norm2_g, l0_ffn_w1, l0_ffn_w3, l0_ffn_w2, l1_norm1_g, l1_lam_re, l1_lam_im, l1_log_dt, l1_b_re, l1_b_im, l1_c_re, l1_c_im, l1_d_skip, l1_glu_wv, l1_glu_wg, l1_norm2_g, l1_router_w, l1_moe_w1, l1_moe_w3, l1_moe_w2, l2_norm1_g, l2_w_in, l2_conv_w, l2_conv_b, l2_b_igate, l2_b_fgate, l2_mh_norm_g, l2_cq_norm_g, l2_w_uq, l2_ckv_norm_g, l2_w_ukv, l2_qn_g, l2_kn_g, l2_w_o, l2_norm2_g, l2_ffn_w1, l2_ffn_w3, l2_ffn_w2, l3_norm1_g, l3_lam_re, l3_lam_im, l3_log_dt, l3_b_re, l3_b_im, l3_c_re, l3_c_im, l3_d_skip, l3_glu_wv, l3_glu_wg, l3_norm2_g, l3_router_w, l3_moe_w1, l3_moe_w3, l3_moe_w2):
    bsz, seq, _ = x.shape
    rope = _rope_tables(positions)
    h = x.reshape(bsz * seq, D_MODEL)
    h = _even_layer(h, rope, bsz, seq, l0_norm1_g, l0_w_in, l0_conv_w, l0_conv_b, l0_b_igate, l0_b_fgate,
                    l0_mh_norm_g, l0_cq_norm_g, l0_w_uq, l0_ckv_norm_g, l0_w_ukv, l0_qn_g, l0_kn_g, l0_w_o,
                    l0_norm2_g, l0_ffn_w1, l0_ffn_w3, l0_ffn_w2)
    h = _odd_layer(h, bsz, seq, l1_norm1_g, l1_lam_re, l1_lam_im, l1_log_dt, l1_b_re, l1_b_im, l1_c_re, l1_c_im,
                   l1_d_skip, l1_glu_wv, l1_glu_wg, l1_norm2_g, l1_router_w, l1_moe_w1, l1_moe_w3, l1_moe_w2)
    h = _even_layer(h, rope, bsz, seq, l2_norm1_g, l2_w_in, l2_conv_w, l2_conv_b, l2_b_igate, l2_b_fgate,
                    l2_mh_norm_g, l2_cq_norm_g, l2_w_uq, l2_ckv_norm_g, l2_w_ukv, l2_qn_g, l2_kn_g, l2_w_o,
                    l2_norm2_g, l2_ffn_w1, l2_ffn_w3, l2_ffn_w2)
    h = _odd_layer(h, bsz, seq, l3_norm1_g, l3_lam_re, l3_lam_im, l3_log_dt, l3_b_re, l3_b_im, l3_c_re, l3_c_im,
                   l3_d_skip, l3_glu_wv, l3_glu_wg, l3_norm2_g, l3_router_w, l3_moe_w1, l3_moe_w3, l3_moe_w2)
    return h.reshape(bsz, seq, D_MODEL)
```

```python
import functools
import math

import jax
import jax.numpy as jnp
import numpy as np
from jax import lax
from jax.experimental import pallas as pl
from jax.experimental.pallas import tpu as pltpu

F32 = jnp.float32
BF16 = jnp.bfloat16

D_MODEL = 1024
NORM_EPS = 1e-6
M_HEADS = 4
M_HEAD_DIM = 128
M_WIDTH = M_HEADS * M_HEAD_DIM
CONV_WIDTH = 4
A_HEADS = 8
Q_LORA = 256
KV_LORA = 128
QK_NOPE = 64
QK_ROPE = 32
QK_HEAD = QK_NOPE + QK_ROPE
V_HEAD = 64
A_WIDTH = A_HEADS * V_HEAD
ROPE_THETA = 10000.0
MLA_CHUNK = 64
S5_GROUP = 16
S5_GROUPS = D_MODEL // S5_GROUP
S5_STATE = 64
D_FF = 2816
N_EXPERTS = 8
D_FF_EXPERT = 3584

LANES = 128
HEAD_PAD = 128
Y_COLS = 4 * M_WIDTH + Q_LORA + KV_LORA
IG_LANE = 0
FG_LANE = 8
KR_LANE = QK_NOPE

VMEM_LIMIT = 56 * 1024 * 1024
NEG_BIG = -1e30
LOG2_E = math.log2(math.e)


def _cparams(*sem):
    return pltpu.CompilerParams(dimension_semantics=sem, vmem_limit_bytes=VMEM_LIMIT)


def _rms(x, g):
    return x * lax.rsqrt(jnp.mean(x * x, axis=-1, keepdims=True) + NORM_EPS) * g


def _sigmoid(x):
    return 1.0 / (1.0 + jnp.exp(-x))


def _dot(a, b):
    return jnp.dot(a, b, preferred_element_type=F32)


def _dot_nt(a, b):
    return lax.dot_general(a, b, (((1,), (1,)), ((), ())), preferred_element_type=F32)


def _dot_tn(a, b):
    return lax.dot_general(a, b, (((0,), (0,)), ((), ())), preferred_element_type=F32)


def _const_spec(shape):
    nd = len(shape)
    return pl.BlockSpec(shape, lambda *_: (0,) * nd, pipeline_mode=pl.Buffered(1))


def _inproj_kernel(x_ref, g_ref, w_ref, y_ref, misc_ref):
    xn = _rms(x_ref[...], g_ref[...]).astype(BF16)
    y = _dot(xn, w_ref[...])
    y_ref[...] = y[:, :Y_COLS].astype(BF16)
    misc_ref[...] = y[:, Y_COLS:]


def _inproj(x, g, w_packed, tm):
    t = x.shape[0]
    return pl.pallas_call(
        _inproj_kernel,
        grid=(t // tm,),
        in_specs=[pl.BlockSpec((tm, D_MODEL), lambda i: (i, 0)),
                  _const_spec((1, D_MODEL)),
                  _const_spec((D_MODEL, Y_COLS + LANES))],
        out_specs=[pl.BlockSpec((tm, Y_COLS), lambda i: (i, 0)),
                   pl.BlockSpec((tm, LANES), lambda i: (i, 0))],
        out_shape=[jax.ShapeDtypeStruct((t, Y_COLS), BF16),
                   jax.ShapeDtypeStruct((t, LANES), F32)],
        compiler_params=_cparams("parallel"),
        name="inproj",
    )(x, g, w_packed)


def _cummax_lanes(r, width):
    lane = lax.broadcasted_iota(jnp.int32, r.shape, 1)
    s = 1
    while s < width:
        shifted = pltpu.roll(r, s, axis=1)
        r = jnp.where(lane >= s, jnp.maximum(r, shifted), r)
        s *= 2
    return r


def _mlstm_kernel(q_ref, k_ref, v_ref, og_ref, misc_ref, cwq_ref, cwk_ref, cbq_ref, cbk_ref, gb_ref, hg_ref,
                  out_ref, qpad, kpad, c_scr, n_scr, m_scr, *, lc):
    @pl.when(pl.program_id(1) == 0)
    def _():
        qpad[0:8, :] = jnp.zeros((8, M_WIDTH), F32)
        kpad[0:8, :] = jnp.zeros((8, M_WIDTH), F32)
        c_scr[...] = jnp.zeros_like(c_scr)
        n_scr[...] = jnp.zeros_like(n_scr)
        m_scr[...] = jnp.zeros_like(m_scr)

    qpad[8:8 + lc, :] = q_ref[...].astype(F32)
    kpad[8:8 + lc, :] = k_ref[...].astype(F32)

    def conv_silu(pad, cw_ref, cb_ref):
        acc = jnp.broadcast_to(cb_ref[...], (lc, M_WIDTH))
        for j in range(CONV_WIDTH):
            off = 8 - (CONV_WIDTH - 1) + j
            acc = acc + pad[off:off + lc, :] * cw_ref[j:j + 1, :]
        return acc * _sigmoid(acc)

    qc = conv_silu(qpad, cwq_ref, cbq_ref) * (M_HEAD_DIM ** -0.5)
    kc = conv_silu(kpad, cwk_ref, cbk_ref)
    qpad[0:8, :] = qpad[lc:lc + 8, :]
    kpad[0:8, :] = kpad[lc:lc + 8, :]

    gt = (misc_ref[...] + gb_ref[...]).T
    ig = gt[IG_LANE:IG_LANE + 8, :]
    fpre = gt[FG_LANE:FG_LANE + 8, :]
    lf = jnp.minimum(fpre, 0.0) - jnp.log(1.0 + jnp.exp(-jnp.abs(fpre)))
    hi = lf.astype(BF16).astype(F32)
    r1 = lf - hi
    mid = r1.astype(BF16).astype(F32)
    lo = r1 - mid
    stack = jnp.concatenate([hi, mid, lo, jnp.zeros_like(lo)], axis=0).astype(BF16)
    srow = lax.broadcasted_iota(jnp.int32, (lc, lc), 0)
    scol = lax.broadcasted_iota(jnp.int32, (lc, lc), 1)
    upper = (srow <= scol).astype(BF16)
    cs = _dot(stack, upper)
    b = cs[0:8] + cs[8:16] + cs[16:24]

    r = ig - b
    cm = _cummax_lanes(r, lc)
    mprev = m_scr[:, 0:1]
    mx = jnp.maximum(mprev, cm)
    w_inter = jnp.exp(mprev - mx)
    e_neg = jnp.exp(-(b + mx))
    total = b[:, lc - 1:lc]
    mnew = total + mx[:, lc - 1:lc]
    a_prev = jnp.exp(total + mprev - mnew)
    w_in = jnp.exp(total + r - mnew)
    m_scr[...] = jnp.broadcast_to(mnew, (8, LANES))

    cols = jnp.concatenate([-mx, w_inter, e_neg, w_in, jnp.zeros((LANES - 32, lc), F32)], axis=0).T
    tril = scol <= srow

    heads = [slice(h * M_HEAD_DIM, (h + 1) * M_HEAD_DIM) for h in range(M_HEADS)]
    qbs = [qc[:, sl].astype(BF16) for sl in heads]
    qk = [_dot_nt(qbs[h], kc[:, heads[h]].astype(BF16)) for h in range(M_HEADS)]
    qcs = [_dot(qbs[h], c_scr[h].astype(BF16)) for h in range(M_HEADS)]
    for h in range(M_HEADS):
        sl = heads[h]
        qh = qc[:, sl]
        kh = kc[:, sl]
        vh = v_ref[:, sl]
        u_col = cols[:, h:h + 1]
        wi_col = cols[:, 8 + h:9 + h]
        en_col = cols[:, 16 + h:17 + h]
        win_col = cols[:, 24 + h:25 + h]
        dmat = jnp.where(tril, u_col + r[h:h + 1, :], NEG_BIG)
        p = jnp.exp(dmat) * qk[h]
        n_h = n_scr[h:h + 1, :]
        num = wi_col * qcs[h] + _dot(p.astype(BF16), vh)
        den = wi_col * jnp.sum(qh * n_h, axis=-1, keepdims=True) + jnp.sum(p, axis=-1, keepdims=True)
        ht = num / jnp.maximum(jnp.abs(den), en_col)
        kw = kh * win_col
        ah = a_prev[h:h + 1, :]
        c_scr[h] = ah * c_scr[h] + _dot_tn(kw.astype(BF16), vh)
        n_scr[h:h + 1, :] = ah * n_h + jnp.sum(kw, axis=0, keepdims=True)
        hn = _rms(ht, hg_ref[:, sl])
        out_ref[:, sl] = (_sigmoid(og_ref[:, sl].astype(F32)) * hn).astype(BF16)


def _mlstm(y, misc, cwq, cwk, cbq, cbk, gate_bias, hg, bsz, seq, lc):
    t = bsz * seq
    nch = seq // lc
    row = lambda b, c: b * nch + c
    return pl.pallas_call(
        functools.partial(_mlstm_kernel, lc=lc),
        grid=(bsz, nch),
        in_specs=[pl.BlockSpec((lc, M_WIDTH), lambda b, c: (row(b, c), 0)),
                  pl.BlockSpec((lc, M_WIDTH), lambda b, c: (row(b, c), 1)),
                  pl.BlockSpec((lc, M_WIDTH), lambda b, c: (row(b, c), 2)),
                  pl.BlockSpec((lc, M_WIDTH), lambda b, c: (row(b, c), 3)),
                  pl.BlockSpec((lc, LANES), lambda b, c: (row(b, c), 0)),
                  _const_spec((CONV_WIDTH, M_WIDTH)), _const_spec((CONV_WIDTH, M_WIDTH)),
                  _const_spec((1, M_WIDTH)), _const_spec((1, M_WIDTH)),
                  _const_spec((1, LANES)), _const_spec((1, M_WIDTH))],
        out_specs=pl.BlockSpec((lc, M_WIDTH), lambda b, c: (row(b, c), 0)),
        out_shape=jax.ShapeDtypeStruct((t, M_WIDTH), BF16),
        scratch_shapes=[pltpu.VMEM((lc + 8, M_WIDTH), F32), pltpu.VMEM((lc + 8, M_WIDTH), F32),
                        pltpu.VMEM((M_HEADS, M_HEAD_DIM, M_HEAD_DIM), F32),
                        pltpu.VMEM((8, M_HEAD_DIM), F32), pltpu.VMEM((8, LANES), F32)],
        compiler_params=_cparams("parallel", "arbitrary"),
        name="mlstm",
    )(y, y, y, y, misc, cwq, cwk, cbq, cbk, gate_bias, hg)


def _mla_prep_kernel(cq_ref, ckv_ref, misc_ref, cos_ref, sin_ref, cqg_ref, wuq_ref, wuqs_ref, ckvg_ref, wuk_ref,
                     wuv_ref, swap_ref, qng_ref, qngs_ref, kng_ref, kngs_ref, q_ref, k_ref, v_ref):
    cqn = _rms(cq_ref[...].astype(F32), cqg_ref[...]).astype(BF16)
    qa = _dot(cqn, wuq_ref[...])
    qs = _dot(cqn, wuqs_ref[...])
    ckvn = _rms(ckv_ref[...].astype(F32), ckvg_ref[...]).astype(BF16)
    ka = _dot(ckvn, wuk_ref[...])
    va = _dot(ckvn, wuv_ref[...])
    lane = lax.broadcasted_iota(jnp.int32, misc_ref.shape, 1)
    kr = jnp.where((lane >= KR_LANE) & (lane < KR_LANE + QK_ROPE), misc_ref[...], 0.0)
    kr_hi = kr.astype(BF16)
    kr_lo = (kr - kr_hi.astype(F32)).astype(BF16)
    kr_swap = _dot(kr_hi, swap_ref[...]) + _dot(kr_lo, swap_ref[...])
    cos = cos_ref[...]
    sin = sin_ref[...]
    q_cos = qng_ref[...] * cos
    q_sin = qngs_ref[...] * sin
    k_cos = kng_ref[...] * cos
    k_rot = kr_swap * (kngs_ref[...] * sin)

    def inv_rms(xh):
        return lax.rsqrt(jnp.sum(xh * xh, axis=-1, keepdims=True) * (1.0 / QK_HEAD) + NORM_EPS)

    for h in range(A_HEADS):
        sl = slice(h * HEAD_PAD, (h + 1) * HEAD_PAD)
        qh = qa[:, sl]
        kh = ka[:, sl] + kr
        q_ref[0, h, 0] = ((qh * q_cos + qs[:, sl] * q_sin)
                          * (inv_rms(qh) * (LOG2_E * QK_HEAD ** -0.5))).T.astype(BF16)
        k_ref[0, h] = ((kh * k_cos + k_rot) * inv_rms(kh)).astype(BF16)
        v_ref[0, h, 0] = jnp.where(lane < V_HEAD, va[:, sl], 1.0).T.astype(BF16)


def _mla_prep(y, misc, cos_t, sin_s, cqg, wuq, wuq_swap, ckvg, wuk, wuv, swap, qng, qng_swap, kng, kng_swap,
              bsz, seq, ts):
    t = bsz * seq
    nt = seq // ts
    row = lambda b, i: (b * nt + i, 0)
    cq_blk = (4 * M_WIDTH) // Q_LORA
    ckv_blk = (4 * M_WIDTH + Q_LORA) // KV_LORA
    hw = A_HEADS * HEAD_PAD
    return pl.pallas_call(
        _mla_prep_kernel,
        grid=(bsz, nt),
        in_specs=[pl.BlockSpec((ts, Q_LORA), lambda b, i: (b * nt + i, cq_blk)),
                  pl.BlockSpec((ts, KV_LORA), lambda b, i: (b * nt + i, ckv_blk)),
                  pl.BlockSpec((ts, LANES), row), pl.BlockSpec((ts, LANES), row), pl.BlockSpec((ts, LANES), row),
                  _const_spec((1, Q_LORA)), _const_spec((Q_LORA, hw)), _const_spec((Q_LORA, hw)),
                  _const_spec((1, KV_LORA)), _const_spec((KV_LORA, hw)), _const_spec((KV_LORA, hw)),
                  _const_spec((HEAD_PAD, HEAD_PAD)),
                  _const_spec((1, HEAD_PAD)), _const_spec((1, HEAD_PAD)),
                  _const_spec((1, HEAD_PAD)), _const_spec((1, HEAD_PAD))],
        out_specs=[pl.BlockSpec((1, A_HEADS, 1, HEAD_PAD, ts), lambda b, i: (b, 0, i, 0, 0)),
                   pl.BlockSpec((1, A_HEADS, ts, HEAD_PAD), lambda b, i: (b, 0, i, 0)),
                   pl.BlockSpec((1, A_HEADS, 1, HEAD_PAD, ts), lambda b, i: (b, 0, i, 0, 0))],
        out_shape=[jax.ShapeDtypeStruct((bsz, A_HEADS, nt, HEAD_PAD, ts), BF16),
                   jax.ShapeDtypeStruct((bsz, A_HEADS, seq, HEAD_PAD), BF16),
                   jax.ShapeDtypeStruct((bsz, A_HEADS, nt, HEAD_PAD, ts), BF16)],
        compiler_params=_cparams("parallel", "parallel"),
        name="mla_prep",
    )(y, y, misc, cos_t, sin_s, cqg, wuq, wuq_swap, ckvg, wuk, wuv, swap, qng, qng_swap, kng, kng_swap)


def _attn_kernel(q_ref, k_ref, v_ref, o_ref, *scratch, tq):
    m_refs, acc_refs = scratch[:A_HEADS], scratch[A_HEADS:]
    i = pl.program_id(1)
    key = lax.broadcasted_iota(jnp.int32, (tq, tq), 0)
    qry = lax.broadcasted_iota(jnp.int32, (tq, tq), 1)
    dmask = (key // MLA_CHUNK) <= (qry // MLA_CHUNK)
    for h in range(A_HEADS):
        m_refs[h][...] = jnp.full((1, tq), NEG_BIG, F32)
        acc_refs[h][...] = jnp.zeros((HEAD_PAD, tq), F32)

    def update(kt, masked):
        off = pl.multiple_of(kt * tq, tq)
        scores = [_dot(k_ref[0, h, pl.ds(off, tq), :], q_ref[0, h, 0]) for h in range(A_HEADS)]
        for h in range(A_HEADS):
            st = scores[h]
            if masked:
                st = jnp.where(dmask, st, NEG_BIG)
            m = m_refs[h][...]
            mn = jnp.maximum(m, jnp.max(st, axis=0, keepdims=True))
            p = jnp.exp2(st - mn)
            acc_refs[h][...] = jnp.exp2(m - mn) * acc_refs[h][...] + _dot(v_ref[0, h, kt], p.astype(BF16))
            m_refs[h][...] = mn

    def body(kt, carry):
        update(kt, False)
        return carry

    lax.fori_loop(0, i, body, 0)
    update(i, True)
    for hp in range(A_HEADS // 2):
        a0 = acc_refs[2 * hp][...]
        a1 = acc_refs[2 * hp + 1][...]
        pair = jnp.concatenate([a0[:V_HEAD] / a0[V_HEAD:V_HEAD + 1], a1[:V_HEAD] / a1[V_HEAD:V_HEAD + 1]], axis=0)
        o_ref[:, hp * LANES:(hp + 1) * LANES] = pair.T.astype(BF16)


def _attention(q_t, k, v_t, bsz, seq, tq):
    nq = seq // tq
    return pl.pallas_call(
        functools.partial(_attn_kernel, tq=tq),
        grid=(bsz, nq),
        in_specs=[pl.BlockSpec((1, A_HEADS, 1, HEAD_PAD, tq), lambda b, i: (b, 0, i, 0, 0)),
                  pl.BlockSpec((1, A_HEADS, seq, HEAD_PAD), lambda b, i: (b, 0, 0, 0)),
                  pl.BlockSpec((1, A_HEADS, nq, HEAD_PAD, tq), lambda b, i: (b, 0, 0, 0, 0))],
        out_specs=pl.BlockSpec((tq, A_WIDTH), lambda b, i: (b * nq + i, 0)),
        out_shape=jax.ShapeDtypeStruct((bsz * seq, A_WIDTH), BF16),
        scratch_shapes=([pltpu.VMEM((1, tq), F32)] * A_HEADS + [pltpu.VMEM((HEAD_PAD, tq), F32)] * A_HEADS),
        compiler_params=_cparams("parallel", "arbitrary"),
        name="mla_attention",
    )(q_t, k, v_t)


def _outproj_ffn_kernel(m_ref, a_ref, x_ref, wom_ref, woa_ref, g_ref, w1_ref, w3_ref, w2_ref, o_ref):
    x1 = x_ref[...] + _dot(m_ref[...], wom_ref[...]) + _dot(a_ref[...], woa_ref[...])
    hn = _rms(x1, g_ref[...]).astype(BF16)
    a = _dot(hn, w1_ref[...])
    b = _dot(hn, w3_ref[...])
    o_ref[...] = x1 + _dot((a * _sigmoid(a) * b).astype(BF16), w2_ref[...])


def _outproj_ffn(m_out, a_out, x, wom, woa, g, w1, w3, w2, tm):
    t = x.shape[0]
    return pl.pallas_call(
        _outproj_ffn_kernel,
        grid=(t // tm,),
        in_specs=[pl.BlockSpec((tm, M_WIDTH), lambda i: (i, 0)),
                  pl.BlockSpec((tm, A_WIDTH), lambda i: (i, 0)),
                  pl.BlockSpec((tm, D_MODEL), lambda i: (i, 0)),
                  _const_spec((M_WIDTH, D_MODEL)), _const_spec((A_WIDTH, D_MODEL)), _const_spec((1, D_MODEL)),
                  _const_spec((D_MODEL, D_FF)), _const_spec((D_MODEL, D_FF)), _const_spec((D_FF, D_MODEL))],
        out_specs=pl.BlockSpec((tm, D_MODEL), lambda i: (i, 0)),
        out_shape=jax.ShapeDtypeStruct((t, D_MODEL), F32),
        compiler_params=_cparams("parallel"),
        name="outproj_ffn",
    )(m_out, a_out, x, wom, woa, g, w1, w3, w2)


S5_CHUNK = 16
S5_ROW = S5_CHUNK * S5_GROUP


GROUPS_PER_TILE = LANES // S5_GROUP
RELAYOUT_ROWS = 32


def _lane_group_masks(rows):
    grp = lax.broadcasted_iota(jnp.int32, (rows, LANES), 1) // S5_GROUP
    return [grp == g for g in range(GROUPS_PER_TILE)]


def _to_group_major(vs, masks):
    n = GROUPS_PER_TILE
    rolled = []
    for r in range(n):
        d = vs[r % n]
        for g in range(1, n):
            d = jnp.where(masks[g], vs[(g + r) % n], d)
        rolled.append(d if r == 0 else pltpu.roll(d, r * S5_GROUP, axis=1))
    outs = []
    for grp in range(n):
        o = rolled[(-grp) % n]
        for w in range(1, n):
            o = jnp.where(masks[w], rolled[(w - grp) % n], o)
        outs.append(o)
    return outs


def _to_token_major(outs, masks):
    n = GROUPS_PER_TILE
    rolled = []
    for r in range(n):
        d = outs[(-r) % n]
        for w in range(1, n):
            d = jnp.where(masks[w], outs[(w - r) % n], d)
        rolled.append(d if r == 0 else pltpu.roll(d, LANES - r * S5_GROUP, axis=1))
    vs = []
    for w in range(n):
        v = rolled[w]
        for g in range(1, n):
            v = jnp.where(masks[g], rolled[(w - g) % n], v)
        vs.append(v)
    return vs


def _relayout_steps(tile_rows):
    nrow = tile_rows // S5_CHUNK
    return [(j, k, rg) for j in range(D_MODEL // LANES) for k in range(S5_CHUNK // GROUPS_PER_TILE)
            for rg in range(nrow // RELAYOUT_ROWS)]


def _norm_group_major_kernel(x_ref, g_ref, u_ref, un_scr):
    un = _rms(x_ref[...], g_ref[...])
    for j in range(D_MODEL // LANES):
        un_scr[j] = un[:, j * LANES:(j + 1) * LANES]
    masks = _lane_group_masks(RELAYOUT_ROWS)
    n = GROUPS_PER_TILE
    for j, k, rg in _relayout_steps(x_ref.shape[0]):
        rows = slice(rg * RELAYOUT_ROWS, (rg + 1) * RELAYOUT_ROWS)
        first = rg * RELAYOUT_ROWS * S5_CHUNK + k * n
        vs = [un_scr[j, pl.ds(first + w, RELAYOUT_ROWS, stride=S5_CHUNK), :] for w in range(n)]
        outs = _to_group_major(vs, masks)
        for grp in range(n):
            u_ref[j * n + grp, rows, k * LANES:(k + 1) * LANES] = outs[grp].astype(BF16)


def _norm_group_major(x, g, tm):
    t = x.shape[0]
    nrow = tm // S5_CHUNK
    return pl.pallas_call(
        _norm_group_major_kernel,
        grid=(t // tm,),
        in_specs=[pl.BlockSpec((tm, D_MODEL), lambda i: (i, 0)), _const_spec((1, D_MODEL))],
        out_specs=pl.BlockSpec((S5_GROUPS, nrow, S5_ROW), lambda i: (0, i, 0)),
        out_shape=jax.ShapeDtypeStruct((S5_GROUPS, t // S5_CHUNK, S5_ROW), BF16),
        scratch_shapes=[pltpu.VMEM((D_MODEL // LANES, tm, LANES), F32)],
        compiler_params=_cparams("parallel"),
        name="norm_group_major",
    )(x, g)


def _gelu_tanh(x):
    return 0.5 * x * (1.0 + jnp.tanh(math.sqrt(2.0 / math.pi) * (x + 0.044715 * (x * x * x))))


def _s5_kernel(u_ref, w_ref, cst_ref, a1_ref, a2_ref, dk_ref, z_ref, vv_scr, xp_scr, *, nch, bsz):
    two_p = 2 * S5_STATE
    u = u_ref[0]
    r = _dot(u, w_ref[0])
    vv_scr[0] = r[:, S5_ROW:S5_ROW + two_p]
    vv_scr[1] = r[:, S5_ROW + two_p:]
    a1 = a1_ref[0]
    a2 = a2_ref[0]

    def body(c, carry):
        p, q = carry
        rows = pl.ds(c, bsz, stride=nch)
        xp_scr[rows, :] = p
        pn = a1 * p + a2 * q + vv_scr[0, rows, :]
        qn = a1 * q - a2 * p + vv_scr[1, rows, :]
        return pn, qn

    zero = jnp.zeros((bsz, two_p), F32)
    lax.fori_loop(0, nch, body, (zero, zero), unroll=4)
    y = r[:, :S5_ROW] + _dot(xp_scr[...].astype(BF16), cst_ref[0]) + dk_ref[0] * u.astype(F32)
    z_ref[0] = _gelu_tanh(y).astype(BF16)


def _s5(u_g, w_all, cst, a1, a2, dk, nch, bsz):
    g, n, _ = u_g.shape
    two_p = 2 * S5_STATE
    wcols = S5_ROW + 2 * two_p
    return pl.pallas_call(
        functools.partial(_s5_kernel, nch=nch, bsz=bsz),
        grid=(g,),
        in_specs=[pl.BlockSpec((1, n, S5_ROW), lambda i: (i, 0, 0)),
                  pl.BlockSpec((1, S5_ROW, wcols), lambda i: (i, 0, 0)),
                  pl.BlockSpec((1, two_p, S5_ROW), lambda i: (i, 0, 0)),
                  pl.BlockSpec((1, 1, two_p), lambda i: (i, 0, 0)),
                  pl.BlockSpec((1, 1, two_p), lambda i: (i, 0, 0)),
                  pl.BlockSpec((1, 1, S5_ROW), lambda i: (i, 0, 0))],
        out_specs=pl.BlockSpec((1, n, S5_ROW), lambda i: (i, 0, 0)),
        out_shape=jax.ShapeDtypeStruct((g, n, S5_ROW), BF16),
        scratch_shapes=[pltpu.VMEM((2, n, two_p), F32), pltpu.VMEM((n, two_p), F32)],
        compiler_params=_cparams("parallel"),
        name="s5",
    )(u_g, w_all, cst, a1, a2, dk)


def _s5_weights(lam_re, lam_im, log_dt, b_re, b_im, c_re, c_im, d_skip):
    lc, h, p = S5_CHUNK, S5_GROUP, S5_STATE
    lam = lax.complex(jnp.minimum(lam_re.astype(F32), -1e-4), lam_im.astype(F32))
    dt = jnp.exp(log_dt.astype(F32))[:, None]
    ldt = lam * dt
    a_bar = jnp.exp(ldt)
    b_bar = ((a_bar - 1.0) / lam)[..., None] * lax.complex(b_re.astype(F32), b_im.astype(F32))
    c_mat = lax.complex(c_re.astype(F32), c_im.astype(F32))
    steps = jnp.arange(lc + 1, dtype=F32)
    apow = jnp.exp(ldt[None] * steps[:, None, None].astype(ldt.dtype))
    g = lam.shape[0]
    ker = jnp.einsum('gip,kgp,gpj->gkij', c_mat, apow[:lc], b_bar).real
    lag = np.arange(lc)[None, :] - np.arange(lc)[:, None]
    toe = jnp.where((lag >= 0)[None, :, :, None, None], ker[:, np.clip(lag, 0, lc - 1)], 0.0)
    toe = toe.transpose(0, 1, 4, 2, 3).reshape(g, lc * h, lc * h)
    wst = apow[:lc][::-1].transpose(1, 0, 2)[:, :, None, :] * jnp.swapaxes(b_bar, 1, 2)[:, None, :, :]
    wst = wst.reshape(g, lc * h, p)
    w_all = jnp.concatenate([toe, wst.real, wst.imag, wst.imag, wst.real], axis=-1).astype(BF16)
    ca = c_mat[:, None, :, :] * apow[1:lc + 1].transpose(1, 0, 2)[:, :, None, :]
    ca = ca.reshape(g, lc * h, p)
    cst = jnp.concatenate([ca.real, -ca.imag], axis=-1).transpose(0, 2, 1).astype(BF16)
    al = apow[lc]
    a1 = jnp.concatenate([al.real, al.real], axis=-1)[:, None, :]
    a2 = jnp.concatenate([-al.imag, al.imag], axis=-1)[:, None, :]
    dk = jnp.tile(d_skip.astype(F32).reshape(g, 1, h), (1, lc, 1)).reshape(g, 1, lc * h)
    return w_all, cst, a1, a2, dk


def _glu_kernel(z_ref, x_ref, wv_ref, wg_ref, g_ref, x2_ref, xn_ref, zt_scr):
    masks = _lane_group_masks(RELAYOUT_ROWS)
    n = GROUPS_PER_TILE
    for j, k, rg in _relayout_steps(x_ref.shape[0]):
        rows = slice(rg * RELAYOUT_ROWS, (rg + 1) * RELAYOUT_ROWS)
        outs = [z_ref[j * n + grp, rows, k * LANES:(k + 1) * LANES].astype(F32) for grp in range(n)]
        vs = _to_token_major(outs, masks)
        first = rg * RELAYOUT_ROWS * S5_CHUNK + k * n
        for w in range(n):
            zt_scr[j, pl.ds(first + w, RELAYOUT_ROWS, stride=S5_CHUNK), :] = vs[w]
    z = jnp.concatenate([zt_scr[j] for j in range(D_MODEL // LANES)], axis=1).astype(BF16)
    x2 = x_ref[...] + _dot(z, wv_ref[...]) * _sigmoid(_dot(z, wg_ref[...]))
    x2_ref[...] = x2
    xn_ref[...] = _rms(x2, g_ref[...]).astype(BF16)


def _glu(z_g, x, wv, wg, g, tm):
    t = x.shape[0]
    nrow = tm // S5_CHUNK
    return pl.pallas_call(
        _glu_kernel,
        grid=(t // tm,),
        in_specs=[pl.BlockSpec((S5_GROUPS, nrow, S5_ROW), lambda i: (0, i, 0)),
                  pl.BlockSpec((tm, D_MODEL), lambda i: (i, 0)),
                  _const_spec((D_MODEL, D_MODEL)), _const_spec((D_MODEL, D_MODEL)), _const_spec((1, D_MODEL))],
        out_specs=[pl.BlockSpec((tm, D_MODEL), lambda i: (i, 0)), pl.BlockSpec((tm, D_MODEL), lambda i: (i, 0))],
        out_shape=[jax.ShapeDtypeStruct((t, D_MODEL), F32), jax.ShapeDtypeStruct((t, D_MODEL), BF16)],
        scratch_shapes=[pltpu.VMEM((D_MODEL // LANES, tm, LANES), F32)],
        compiler_params=_cparams("parallel"),
        name="glu",
    )(z_g, x, wv, wg, g)


def _router_kernel(xn_ref, wh_ref, wl_ref, pos_ref, gate_ref, posc_ref, cnt_ref, *, tb):
    xn = xn_ref[...]
    lg = (_dot_nt(wh_ref[...], xn) + _dot_nt(wl_ref[...], xn))[0:N_EXPERTS]
    eidx = lax.broadcasted_iota(jnp.int32, (N_EXPERTS, tb), 0)
    m1 = jnp.max(lg, axis=0, keepdims=True)
    i1 = jnp.min(jnp.where(lg == m1, eidx, N_EXPERTS), axis=0, keepdims=True)
    rest = jnp.where(eidx == i1, -jnp.inf, lg)
    m2 = jnp.max(rest, axis=0, keepdims=True)
    i2 = jnp.min(jnp.where(rest == m2, eidx, N_EXPERTS), axis=0, keepdims=True)
    e21 = jnp.exp(m2 - m1)
    p1 = 1.0 / (1.0 + e21)
    sel1 = eidx == i1
    sel2 = eidx == i2
    sel = sel1 | sel2
    gate_ref[...] = jnp.where(sel1, p1, jnp.where(sel2, e21 * p1, 0.0))
    onehot = jnp.concatenate([sel.astype(F32), jnp.zeros((N_EXPERTS, tb), F32)], axis=0).astype(BF16)
    srow = lax.broadcasted_iota(jnp.int32, (tb, tb), 0)
    scol = lax.broadcasted_iota(jnp.int32, (tb, tb), 1)
    rank = _dot(onehot, (srow < scol).astype(BF16))[0:N_EXPERTS]
    pos = jnp.where(sel, rank, -1.0)
    pos_ref[...] = pos
    posc_ref[...] = jnp.concatenate([pos, jnp.full((LANES - N_EXPERTS, tb), -1.0, F32)], axis=0).T
    cnt = jnp.sum(sel.astype(F32), axis=1, keepdims=True)
    cnt_ref[0] = jnp.broadcast_to(cnt, (N_EXPERTS, LANES))


def _router(xn, wr_hi, wr_lo, tb):
    t = xn.shape[0]
    nb = t // tb
    return pl.pallas_call(
        functools.partial(_router_kernel, tb=tb),
        grid=(nb,),
        in_specs=[pl.BlockSpec((tb, D_MODEL), lambda i: (i, 0)),
                  _const_spec((2 * N_EXPERTS, D_MODEL)), _const_spec((2 * N_EXPERTS, D_MODEL))],
        out_specs=[pl.BlockSpec((N_EXPERTS, tb), lambda i: (0, i)), pl.BlockSpec((N_EXPERTS, tb), lambda i: (0, i)),
                   pl.BlockSpec((tb, LANES), lambda i: (i, 0)),
                   pl.BlockSpec((1, N_EXPERTS, LANES), lambda i: (i, 0, 0))],
        out_shape=[jax.ShapeDtypeStruct((N_EXPERTS, t), F32), jax.ShapeDtypeStruct((N_EXPERTS, t), F32),
                   jax.ShapeDtypeStruct((t, LANES), F32), jax.ShapeDtypeStruct((nb, N_EXPERTS, LANES), F32)],
        compiler_params=_cparams("parallel"),
        name="router",
    )(xn, wr_hi, wr_lo)


MOE_UNIT = 128
MOE_ALIGN = 16
MOE_TM = 512


def _moe_plan(cnt, n_tokens):
    nb = cnt.shape[0]
    seg = (cnt + MOE_ALIGN - 1) // MOE_ALIGN * MOE_ALIGN
    used = jnp.sum(seg, axis=0)
    gsize = (used + MOE_UNIT + MOE_TM - 1) // MOE_TM * MOE_TM
    gstart = jnp.cumsum(gsize) - gsize
    dest = gstart[None, :] + jnp.cumsum(seg, axis=0) - seg
    max_rows = 2 * n_tokens + N_EXPERTS * (nb * (MOE_ALIGN - 1) + MOE_UNIT + MOE_TM - 1)
    n_tiles = -(-max_rows // MOE_TM)
    tile_start = jnp.arange(n_tiles, dtype=jnp.int32) * MOE_TM
    tile_e = jnp.minimum(jnp.sum(tile_start[:, None] >= (gstart + gsize)[None, :], axis=1), N_EXPERTS - 1)
    tile_valid = tile_start < (gstart + used)[tile_e]
    return dest.reshape(-1).astype(jnp.int32), tile_e.astype(jnp.int32), tile_valid.astype(jnp.int32), n_tiles


def _moe_pack_kernel(dest_ref, cnt_ref, xn_ref, pos_ref, xs_in_ref, xs_ref, stage, sem, *, tb):
    del xs_in_ref
    b = pl.program_id(0)
    unit = MOE_UNIT

    def n_units(e):
        return (cnt_ref[b * N_EXPERTS + e] + unit - 1) // unit

    def chunk_copy(e, r):
        dst = pl.multiple_of(dest_ref[b * N_EXPERTS + e] + r * unit, MOE_ALIGN)
        return pltpu.make_async_copy(stage.at[e % 2, pl.ds(pl.multiple_of(r * unit, unit), unit)],
                                     xs_ref.at[pl.ds(dst, unit)], sem.at[e % 2])

    def wait_expert(e):
        def body(r, carry):
            chunk_copy(e, r).wait()
            return carry
        lax.fori_loop(0, n_units(e), body, 0)

    for e in range(N_EXPERTS):
        if e >= 2:
            wait_expert(e - 2)
        prow = pos_ref[e:e + 1, :]
        n_unit = n_units(e)
        n_pair = n_unit // 2

        def gather(base, m, e=e, prow=prow):
            slot = lax.broadcasted_iota(jnp.int32, (m, tb), 0).astype(F32) + base.astype(F32)
            stage[e % 2, pl.ds(base, m), :] = _dot((prow == slot).astype(BF16), xn_ref[...]).astype(BF16)

        def pair(r, carry, e=e, gather=gather):
            gather(pl.multiple_of(r * (2 * unit), 2 * unit), 2 * unit)
            chunk_copy(e, 2 * r).start()
            chunk_copy(e, 2 * r + 1).start()
            return carry

        lax.fori_loop(0, n_pair, pair, 0)

        @pl.when(n_unit % 2 == 1)
        def _(e=e, gather=gather, n_pair=n_pair):
            gather(pl.multiple_of(n_pair * (2 * unit), 2 * unit), unit)
            chunk_copy(e, 2 * n_pair).start()
    wait_expert(N_EXPERTS - 2)
    wait_expert(N_EXPERTS - 1)


def _moe_pack(dest, cnt, xn, pos, xs_init, tb):
    t = xn.shape[0]
    grid_spec = pltpu.PrefetchScalarGridSpec(
        num_scalar_prefetch=2,
        grid=(t // tb,),
        in_specs=[pl.BlockSpec((tb, D_MODEL), lambda b, d, c: (b, 0)),
                  pl.BlockSpec((N_EXPERTS, tb), lambda b, d, c: (0, b)),
                  pl.BlockSpec(memory_space=pl.ANY)],
        out_specs=pl.BlockSpec(memory_space=pl.ANY),
        scratch_shapes=[pltpu.VMEM((2, tb, D_MODEL), BF16), pltpu.SemaphoreType.DMA((2,))],
    )
    return pl.pallas_call(
        functools.partial(_moe_pack_kernel, tb=tb),
        grid_spec=grid_spec,
        out_shape=jax.ShapeDtypeStruct(xs_init.shape, BF16),
        input_output_aliases={4: 0},
        compiler_params=_cparams("arbitrary"),
        name="moe_pack",
    )(dest, cnt, xn, pos, xs_init)


def _moe_ffn_kernel(te_ref, tv_ref, xs_ref, w1_ref, w3_ref, w2_ref, y_ref):
    del te_ref
    valid = tv_ref[pl.program_id(0)] != 0

    @pl.when(valid)
    def _():
        xe = xs_ref[...]
        a = _dot(xe, w1_ref[0])
        g = (a * _sigmoid(a) * _dot(xe, w3_ref[0])).astype(BF16)
        y_ref[...] = _dot(g, w2_ref[0]).astype(BF16)

    @pl.when(jnp.logical_not(valid))
    def _():
        y_ref[...] = jnp.zeros(y_ref.shape, BF16)


def _moe_ffn(tile_e, tile_valid, xs, w1, w3, w2, n_tiles):
    w_up = pl.BlockSpec((1, D_MODEL, D_FF_EXPERT), lambda i, te, tv: (te[i], 0, 0), pipeline_mode=pl.Buffered(1))
    w_down = pl.BlockSpec((1, D_FF_EXPERT, D_MODEL), lambda i, te, tv: (te[i], 0, 0), pipeline_mode=pl.Buffered(1))
    grid_spec = pltpu.PrefetchScalarGridSpec(
        num_scalar_prefetch=2,
        grid=(n_tiles,),
        in_specs=[pl.BlockSpec((MOE_TM, D_MODEL), lambda i, te, tv: (i, 0)), w_up, w_up, w_down],
        out_specs=pl.BlockSpec((MOE_TM, D_MODEL), lambda i, te, tv: (i, 0)),
    )
    return pl.pallas_call(
        _moe_ffn_kernel,
        grid_spec=grid_spec,
        out_shape=jax.ShapeDtypeStruct((n_tiles * MOE_TM, D_MODEL), BF16),
        compiler_params=_cparams("parallel"),
        name="moe_ffn",
    )(tile_e, tile_valid, xs, w1, w3, w2)


def _moe_combine_kernel(dest_ref, cnt_ref, x_ref, pos_ref, gate_ref, posc_ref, ys_ref, o_ref, ybuf, sem, *, tb):
    b = pl.program_id(0)
    unit = MOE_UNIT

    def n_units(e):
        return (cnt_ref[b * N_EXPERTS + e] + unit - 1) // unit

    def chunk_copy(e, r):
        src = pl.multiple_of(dest_ref[b * N_EXPERTS + e] + r * unit, MOE_ALIGN)
        return pltpu.make_async_copy(ys_ref.at[pl.ds(src, unit)],
                                     ybuf.at[e % 2, pl.ds(pl.multiple_of(r * unit, unit), unit)], sem.at[e % 2])

    def start_expert(e):
        def body(r, carry):
            chunk_copy(e, r).start()
            return carry
        lax.fori_loop(0, n_units(e), body, 0)

    def wait_expert(e):
        def body(r, carry):
            chunk_copy(e, r).wait()
            return carry
        lax.fori_loop(0, n_units(e), body, 0)

    start_expert(0)
    o_ref[...] = x_ref[...]
    lane = lax.broadcasted_iota(jnp.int32, (tb, LANES), 1)
    for e in range(N_EXPERTS):
        if e + 1 < N_EXPERTS:
            start_expert(e + 1)
        wait_expert(e)
        prow = pos_ref[e:e + 1, :]
        grow = gate_ref[e:e + 1, :]
        n_unit = n_units(e)
        n_pair = n_unit // 2

        def scatter_chunk(base, m, e=e, prow=prow, grow=grow):
            hit = prow == lax.broadcasted_iota(jnp.int32, (m, tb), 0).astype(F32) + base.astype(F32)
            gs = jnp.sum(jnp.where(hit, grow, 0.0), axis=-1, keepdims=True)
            yg = (ybuf[e % 2, pl.ds(base, m), :].astype(F32) * gs).astype(BF16)
            o_ref[...] += _dot_tn(hit.astype(BF16), yg)

        def scatter_pair(r, carry, scatter_chunk=scatter_chunk):
            scatter_chunk(pl.multiple_of(r * (2 * unit), 2 * unit), 2 * unit)
            return carry

        lax.fori_loop(0, n_pair, scatter_pair, 0)

        @pl.when(n_unit % 2 == 1)
        def _(scatter_chunk=scatter_chunk, n_pair=n_pair):
            scatter_chunk(pl.multiple_of(n_pair * (2 * unit), 2 * unit), unit)


def _moe_combine(dest, cnt, x, pos, gate, posc, ys, tb):
    t = x.shape[0]
    grid_spec = pltpu.PrefetchScalarGridSpec(
        num_scalar_prefetch=2,
        grid=(t // tb,),
        in_specs=[pl.BlockSpec((tb, D_MODEL), lambda b, d, c: (b, 0)),
                  pl.BlockSpec((N_EXPERTS, tb), lambda b, d, c: (0, b)),
                  pl.BlockSpec((N_EXPERTS, tb), lambda b, d, c: (0, b)),
                  pl.BlockSpec((tb, LANES), lambda b, d, c: (b, 0)),
                  pl.BlockSpec(memory_space=pl.ANY)],
        out_specs=pl.BlockSpec((tb, D_MODEL), lambda b, d, c: (b, 0)),
        scratch_shapes=[pltpu.VMEM((2, tb, D_MODEL), BF16), pltpu.SemaphoreType.DMA((2,))],
    )
    return pl.pallas_call(
        functools.partial(_moe_combine_kernel, tb=tb),
        grid_spec=grid_spec,
        out_shape=jax.ShapeDtypeStruct((t, D_MODEL), F32),
        compiler_params=_cparams("parallel"),
        name="moe_combine",
    )(dest, cnt, x, pos, gate, posc, ys)


def _moe(cnt, xn, x, pos, gate, posc, w1, w3, w2, tb, xs_init):
    t = x.shape[0]
    dest, tile_e, tile_valid, n_tiles = _moe_plan(cnt, t)
    cnt = cnt.reshape(-1)
    if xs_init is None:
        xs_init = jnp.zeros((n_tiles * MOE_TM, D_MODEL), BF16)
    xs = _moe_pack(dest, cnt, xn, pos, xs_init, tb)
    ys = _moe_ffn(tile_e, tile_valid, xs, w1, w3, w2, n_tiles)
    return _moe_combine(dest, cnt, x, pos, gate, posc, ys, tb), xs


def _rope_tables(positions):
    half = QK_ROPE // 2
    inv_freq = ROPE_THETA ** (-jnp.arange(0, QK_ROPE, 2, dtype=F32) / QK_ROPE)
    ang = positions.astype(F32).reshape(-1, 1) * inv_freq
    cos, sin = jnp.cos(ang), jnp.sin(ang)
    t = ang.shape[0]
    ones_lo = jnp.ones((t, QK_NOPE), F32)
    zeros_lo = jnp.zeros((t, QK_NOPE), F32)
    tail1 = jnp.ones((t, HEAD_PAD - QK_HEAD), F32)
    tail0 = jnp.zeros((t, HEAD_PAD - QK_HEAD), F32)
    cos_t = jnp.concatenate([ones_lo, cos, cos, tail1], axis=-1)
    sin_s = jnp.concatenate([zeros_lo, -sin, sin, tail0], axis=-1)
    return cos_t, sin_s


def _pad_heads(w, per_head_in, keep, heads):
    k = w.shape[0]
    w = w.reshape(k, heads, per_head_in)[:, :, :keep]
    return jnp.pad(w, ((0, 0), (0, 0), (0, HEAD_PAD - keep))).reshape(k, heads * HEAD_PAD)


def _even_layer(x, rope, bsz, seq, norm1_g, w_in, conv_w, conv_b, b_igate, b_fgate, mh_norm_g, cq_norm_g, w_uq,
                ckv_norm_g, w_ukv, qn_g, kn_g, w_o, norm2_g, ffn_w1, ffn_w3, ffn_w2):
    row = lambda v: v.astype(F32).reshape(1, -1)
    mw = M_WIDTH
    o_ig, o_fg, o_og = 3 * mw, 3 * mw + M_HEADS, 3 * mw + 2 * M_HEADS
    o_cq = o_og + mw
    o_ckv, o_kr = o_cq + Q_LORA, o_cq + Q_LORA + KV_LORA
    misc_w = jnp.zeros((D_MODEL, LANES), F32)
    misc_w = misc_w.at[:, IG_LANE:IG_LANE + M_HEADS].set(w_in[:, o_ig:o_fg])
    misc_w = misc_w.at[:, FG_LANE:FG_LANE + M_HEADS].set(w_in[:, o_fg:o_og])
    misc_w = misc_w.at[:, KR_LANE:KR_LANE + QK_ROPE].set(w_in[:, o_kr:o_kr + QK_ROPE])
    w_packed = jnp.concatenate([w_in[:, :3 * mw], w_in[:, o_og:o_cq], w_in[:, o_cq:o_ckv], w_in[:, o_ckv:o_kr],
                                misc_w], axis=-1).astype(BF16)
    gate_bias = jnp.zeros((1, LANES), F32)
    gate_bias = gate_bias.at[0, IG_LANE:IG_LANE + M_HEADS].set(b_igate.astype(F32))
    gate_bias = gate_bias.at[0, FG_LANE:FG_LANE + M_HEADS].set(b_fgate.astype(F32))

    tm = min(512, x.shape[0])
    y, misc = _inproj(x, row(norm1_g), w_packed, tm)

    lc = min(256, seq)
    m_out = _mlstm(y, misc, conv_w[:, :mw].astype(F32), conv_w[:, mw:].astype(F32), row(conv_b[:mw]),
                   row(conv_b[mw:]), gate_bias, row(mh_norm_g), bsz, seq, lc)

    wuq = _pad_heads(w_uq, QK_HEAD, QK_HEAD, A_HEADS).astype(BF16)
    wuk = _pad_heads(w_ukv, QK_NOPE + V_HEAD, QK_NOPE, A_HEADS).astype(BF16)
    wuv = _pad_heads(w_ukv.reshape(KV_LORA, A_HEADS, QK_NOPE + V_HEAD)[:, :, QK_NOPE:].reshape(KV_LORA, A_WIDTH),
                     V_HEAD, V_HEAD, A_HEADS).astype(BF16)
    pad_g = lambda g: jnp.pad(g.astype(F32), (0, HEAD_PAD - QK_HEAD)).reshape(1, HEAD_PAD)
    half = QK_ROPE // 2

    def swap_halves(v):
        r1, r2 = v[..., QK_NOPE:QK_NOPE + half], v[..., QK_NOPE + half:QK_HEAD]
        return jnp.concatenate([jnp.zeros_like(v[..., :QK_NOPE]), r2, r1], axis=-1)

    wuq_swap = _pad_heads(swap_halves(w_uq.reshape(Q_LORA, A_HEADS, QK_HEAD)).reshape(Q_LORA, A_HEADS * QK_HEAD),
                          QK_HEAD, QK_HEAD, A_HEADS).astype(BF16)
    swap = swap_halves(jnp.eye(HEAD_PAD, dtype=F32)[:, :QK_HEAD])
    swap = jnp.pad(swap, ((0, 0), (0, HEAD_PAD - QK_HEAD))).astype(BF16)
    ts = min(256, seq)
    q, k, v = _mla_prep(y, misc, rope[0], rope[1], row(cq_norm_g), wuq, wuq_swap, row(ckv_norm_g), wuk, wuv, swap,
                        pad_g(qn_g), pad_g(swap_halves(qn_g)), pad_g(kn_g), pad_g(swap_halves(kn_g)), bsz, seq, ts)
    a_out = _attention(q, k, v, bsz, seq, ts)

    return _outproj_ffn(m_out, a_out, x, w_o[:mw].astype(BF16), w_o[mw:].astype(BF16), row(norm2_g),
                        ffn_w1.astype(BF16), ffn_w3.astype(BF16), ffn_w2.astype(BF16), tm)


def _odd_layer(x, bsz, seq, norm1_g, lam_re, lam_im, log_dt, b_re, b_im, c_re, c_im, d_skip, glu_wv, glu_wg,
               norm2_g, router_w, moe_w1, moe_w3, moe_w2, xs_init=None):
    row = lambda v: v.astype(F32).reshape(1, -1)
    t = x.shape[0]
    tm = min(512, t)
    nch = seq // S5_CHUNK
    tr = min(1024, seq)
    u_g = _norm_group_major(x, row(norm1_g), tr)
    w_all, cst, a1, a2, dk = _s5_weights(lam_re, lam_im, log_dt, b_re, b_im, c_re, c_im, d_skip)
    z_g = _s5(u_g, w_all, cst, a1, a2, dk, nch, bsz)
    x2, xn = _glu(z_g, x, glu_wv.astype(BF16), glu_wg.astype(BF16), row(norm2_g), tr)

    tb = min(512, t)
    wr = jnp.pad(router_w.astype(F32).T, ((0, N_EXPERTS), (0, 0)))
    wr_hi = wr.astype(BF16)
    wr_lo = (wr - wr_hi.astype(F32)).astype(BF16)
    pos, gate, posc, cnt = _router(xn, wr_hi, wr_lo, tb)
    cnt = cnt[:, :, 0].astype(jnp.int32)
    return _moe(cnt, xn, x2, pos, gate, posc, moe_w1.astype(BF16), moe_w3.astype(BF16), moe_w2.astype(BF16), tb,
                xs_init)


def kernel(x, positions, l0_norm1_g, l0_w_in, l0_conv_w, l0_conv_b, l0_b_igate, l0_b_fgate, l0_mh_norm_g, l0_cq_norm_g, l0_w_uq, l0_ckv_norm_g, l0_w_ukv, l0_qn_g, l0_kn_g, l0_w_o, l0_norm2_g, l0_ffn_w1, l0_ffn_w3, l0_ffn_w2, l1_norm1_g, l1_lam_re, l1_lam_im, l1_log_dt, l1_b_re, l1_b_im, l1_c_re, l1_c_im, l1_d_skip, l1_glu_wv, l1_glu_wg, l1_norm2_g, l1_router_w, l1_moe_w1, l1_moe_w3, l1_moe_w2, l2_norm1_g, l2_w_in, l2_conv_w, l2_conv_b, l2_b_igate, l2_b_fgate, l2_mh_norm_g, l2_cq_norm_g, l2_w_uq, l2_ckv_norm_g, l2_w_ukv, l2_qn_g, l2_kn_g, l2_w_o, l2_norm2_g, l2_ffn_w1, l2_ffn_w3, l2_ffn_w2, l3_norm1_g, l3_lam_re, l3_lam_im, l3_log_dt, l3_b_re, l3_b_im, l3_c_re, l3_c_im, l3_d_skip, l3_glu_wv, l3_glu_wg, l3_norm2_g, l3_router_w, l3_moe_w1, l3_moe_w3, l3_moe_w2):
    bsz, seq, _ = x.shape
    rope = _rope_tables(positions)
    h = x.reshape(bsz * seq, D_MODEL)
    h = _even_layer(h, rope, bsz, seq, l0_norm1_g, l0_w_in, l0_conv_w, l0_conv_b, l0_b_igate, l0_b_fgate,
                    l0_mh_norm_g, l0_cq_norm_g, l0_w_uq, l0_ckv_norm_g, l0_w_ukv, l0_qn_g, l0_kn_g, l0_w_o,
                    l0_norm2_g, l0_ffn_w1, l0_ffn_w3, l0_ffn_w2)
    h, xs = _odd_layer(h, bsz, seq, l1_norm1_g, l1_lam_re, l1_lam_im, l1_log_dt, l1_b_re, l1_b_im, l1_c_re, l1_c_im,
                       l1_d_skip, l1_glu_wv, l1_glu_wg, l1_norm2_g, l1_router_w, l1_moe_w1, l1_moe_w3, l1_moe_w2)
    h = _even_layer(h, rope, bsz, seq, l2_norm1_g, l2_w_in, l2_conv_w, l2_conv_b, l2_b_igate, l2_b_fgate,
                    l2_mh_norm_g, l2_cq_norm_g, l2_w_uq, l2_ckv_norm_g, l2_w_ukv, l2_qn_g, l2_kn_g, l2_w_o,
                    l2_norm2_g, l2_ffn_w1, l2_ffn_w3, l2_ffn_w2)
    h, _ = _odd_layer(h, bsz, seq, l3_norm1_g, l3_lam_re, l3_lam_im, l3_log_dt, l3_b_re, l3_b_im, l3_c_re, l3_c_im,
                      l3_d_skip, l3_glu_wv, l3_glu_wg, l3_norm2_g, l3_router_w, l3_moe_w1, l3_moe_w3, l3_moe_w2, xs)
    return h.reshape(bsz, seq, D_MODEL)
```

```python
import functools
import math

import jax
import jax.numpy as jnp
import numpy as np
from jax import lax
from jax.experimental import pallas as pl
from jax.experimental.pallas import tpu as pltpu

F32 = jnp.float32
BF16 = jnp.bfloat16

D_MODEL = 1024
NORM_EPS = 1e-6
M_HEADS = 4
M_HEAD_DIM = 128
M_WIDTH = M_HEADS * M_HEAD_DIM
CONV_WIDTH = 4
A_HEADS = 8
Q_LORA = 256
KV_LORA = 128
QK_NOPE = 64
QK_ROPE = 32
QK_HEAD = QK_NOPE + QK_ROPE
V_HEAD = 64
A_WIDTH = A_HEADS * V_HEAD
ROPE_THETA = 10000.0
MLA_CHUNK = 64
S5_GROUP = 16
S5_GROUPS = D_MODEL // S5_GROUP
S5_STATE = 64
D_FF = 2816
N_EXPERTS = 8
D_FF_EXPERT = 3584

LANES = 128
HEAD_PAD = 128
Y_COLS = 4 * M_WIDTH + Q_LORA + KV_LORA
IG_LANE = 0
FG_LANE = 8
KR_LANE = QK_NOPE

VMEM_LIMIT = 56 * 1024 * 1024
NEG_BIG = -1e30
LOG2_E = math.log2(math.e)


def _cparams(*sem):
    return pltpu.CompilerParams(dimension_semantics=sem, vmem_limit_bytes=VMEM_LIMIT)


def _rms(x, g):
    return x * lax.rsqrt(jnp.mean(x * x, axis=-1, keepdims=True) + NORM_EPS) * g


def _sigmoid(x):
    return 1.0 / (1.0 + jnp.exp(-x))


def _dot(a, b):
    return jnp.dot(a, b, preferred_element_type=F32)


def _dot_nt(a, b):
    return lax.dot_general(a, b, (((1,), (1,)), ((), ())), preferred_element_type=F32)


def _dot_tn(a, b):
    return lax.dot_general(a, b, (((0,), (0,)), ((), ())), preferred_element_type=F32)


def _const_spec(shape):
    nd = len(shape)
    return pl.BlockSpec(shape, lambda *_: (0,) * nd, pipeline_mode=pl.Buffered(1))


def _inproj_kernel(x_ref, g_ref, w_ref, y_ref, misc_ref):
    xn = _rms(x_ref[...], g_ref[...]).astype(BF16)
    y = _dot(xn, w_ref[...])
    y_ref[...] = y[:, :Y_COLS].astype(BF16)
    misc_ref[...] = y[:, Y_COLS:]


def _inproj(x, g, w_packed, tm):
    t = x.shape[0]
    return pl.pallas_call(
        _inproj_kernel,
        grid=(t // tm,),
        in_specs=[pl.BlockSpec((tm, D_MODEL), lambda i: (i, 0)),
                  _const_spec((1, D_MODEL)),
                  _const_spec((D_MODEL, Y_COLS + LANES))],
        out_specs=[pl.BlockSpec((tm, Y_COLS), lambda i: (i, 0)),
                   pl.BlockSpec((tm, LANES), lambda i: (i, 0))],
        out_shape=[jax.ShapeDtypeStruct((t, Y_COLS), BF16),
                   jax.ShapeDtypeStruct((t, LANES), F32)],
        compiler_params=_cparams("parallel"),
        name="inproj",
    )(x, g, w_packed)


def _cummax_lanes(r, width):
    lane = lax.broadcasted_iota(jnp.int32, r.shape, 1)
    s = 1
    while s < width:
        shifted = pltpu.roll(r, s, axis=1)
        r = jnp.where(lane >= s, jnp.maximum(r, shifted), r)
        s *= 2
    return r


def _mlstm_kernel(q_ref, k_ref, v_ref, og_ref, misc_ref, cwq_ref, cwk_ref, cbq_ref, cbk_ref, gb_ref, hg_ref,
                  out_ref, qpad, kpad, c_scr, n_scr, m_scr, *, lc):
    @pl.when(pl.program_id(1) == 0)
    def _():
        qpad[0:8, :] = jnp.zeros((8, M_WIDTH), F32)
        kpad[0:8, :] = jnp.zeros((8, M_WIDTH), F32)
        c_scr[...] = jnp.zeros_like(c_scr)
        n_scr[...] = jnp.zeros_like(n_scr)
        m_scr[...] = jnp.zeros_like(m_scr)

    qpad[8:8 + lc, :] = q_ref[...].astype(F32)
    kpad[8:8 + lc, :] = k_ref[...].astype(F32)

    def conv_silu(pad, cw_ref, cb_ref):
        acc = jnp.broadcast_to(cb_ref[...], (lc, M_WIDTH))
        for j in range(CONV_WIDTH):
            off = 8 - (CONV_WIDTH - 1) + j
            acc = acc + pad[off:off + lc, :] * cw_ref[j:j + 1, :]
        return acc * _sigmoid(acc)

    qc = conv_silu(qpad, cwq_ref, cbq_ref) * (M_HEAD_DIM ** -0.5)
    kc = conv_silu(kpad, cwk_ref, cbk_ref)
    qpad[0:8, :] = qpad[lc:lc + 8, :]
    kpad[0:8, :] = kpad[lc:lc + 8, :]

    gt = (misc_ref[...] + gb_ref[...]).T
    ig = gt[IG_LANE:IG_LANE + 8, :]
    fpre = gt[FG_LANE:FG_LANE + 8, :]
    lf = jnp.minimum(fpre, 0.0) - jnp.log(1.0 + jnp.exp(-jnp.abs(fpre)))
    hi = lf.astype(BF16).astype(F32)
    r1 = lf - hi
    mid = r1.astype(BF16).astype(F32)
    lo = r1 - mid
    stack = jnp.concatenate([hi, mid, lo, jnp.zeros_like(lo)], axis=0).astype(BF16)
    srow = lax.broadcasted_iota(jnp.int32, (lc, lc), 0)
    scol = lax.broadcasted_iota(jnp.int32, (lc, lc), 1)
    upper = (srow <= scol).astype(BF16)
    cs = _dot(stack, upper)
    b = cs[0:8] + cs[8:16] + cs[16:24]

    r = ig - b
    cm = _cummax_lanes(r, lc)
    mprev = m_scr[:, 0:1]
    mx = jnp.maximum(mprev, cm)
    w_inter = jnp.exp(mprev - mx)
    e_neg = jnp.exp(-(b + mx))
    total = b[:, lc - 1:lc]
    mnew = total + mx[:, lc - 1:lc]
    a_prev = jnp.exp(total + mprev - mnew)
    w_in = jnp.exp(total + r - mnew)
    m_scr[...] = jnp.broadcast_to(mnew, (8, LANES))

    cols = jnp.concatenate([-mx, w_inter, e_neg, w_in, jnp.zeros((LANES - 32, lc), F32)], axis=0).T
    tril = scol <= srow

    heads = [slice(h * M_HEAD_DIM, (h + 1) * M_HEAD_DIM) for h in range(M_HEADS)]
    qbs = [qc[:, sl].astype(BF16) for sl in heads]
    qk = [_dot_nt(qbs[h], kc[:, heads[h]].astype(BF16)) for h in range(M_HEADS)]
    qcs = [_dot(qbs[h], c_scr[h].astype(BF16)) for h in range(M_HEADS)]
    for h in range(M_HEADS):
        sl = heads[h]
        qh = qc[:, sl]
        kh = kc[:, sl]
        vh = v_ref[:, sl]
        u_col = cols[:, h:h + 1]
        wi_col = cols[:, 8 + h:9 + h]
        en_col = cols[:, 16 + h:17 + h]
        win_col = cols[:, 24 + h:25 + h]
        dmat = jnp.where(tril, u_col + r[h:h + 1, :], NEG_BIG)
        p = jnp.exp(dmat) * qk[h]
        n_h = n_scr[h:h + 1, :]
        num = wi_col * qcs[h] + _dot(p.astype(BF16), vh)
        den = wi_col * jnp.sum(qh * n_h, axis=-1, keepdims=True) + jnp.sum(p, axis=-1, keepdims=True)
        ht = num / jnp.maximum(jnp.abs(den), en_col)
        kw = kh * win_col
        ah = a_prev[h:h + 1, :]
        c_scr[h] = ah * c_scr[h] + _dot_tn(kw.astype(BF16), vh)
        n_scr[h:h + 1, :] = ah * n_h + jnp.sum(kw, axis=0, keepdims=True)
        hn = _rms(ht, hg_ref[:, sl])
        out_ref[:, sl] = (_sigmoid(og_ref[:, sl].astype(F32)) * hn).astype(BF16)


def _mlstm(y, misc, cwq, cwk, cbq, cbk, gate_bias, hg, bsz, seq, lc):
    t = bsz * seq
    nch = seq // lc
    row = lambda b, c: b * nch + c
    return pl.pallas_call(
        functools.partial(_mlstm_kernel, lc=lc),
        grid=(bsz, nch),
        in_specs=[pl.BlockSpec((lc, M_WIDTH), lambda b, c: (row(b, c), 0)),
                  pl.BlockSpec((lc, M_WIDTH), lambda b, c: (row(b, c), 1)),
                  pl.BlockSpec((lc, M_WIDTH), lambda b, c: (row(b, c), 2)),
                  pl.BlockSpec((lc, M_WIDTH), lambda b, c: (row(b, c), 3)),
                  pl.BlockSpec((lc, LANES), lambda b, c: (row(b, c), 0)),
                  _const_spec((CONV_WIDTH, M_WIDTH)), _const_spec((CONV_WIDTH, M_WIDTH)),
                  _const_spec((1, M_WIDTH)), _const_spec((1, M_WIDTH)),
                  _const_spec((1, LANES)), _const_spec((1, M_WIDTH))],
        out_specs=pl.BlockSpec((lc, M_WIDTH), lambda b, c: (row(b, c), 0)),
        out_shape=jax.ShapeDtypeStruct((t, M_WIDTH), BF16),
        scratch_shapes=[pltpu.VMEM((lc + 8, M_WIDTH), F32), pltpu.VMEM((lc + 8, M_WIDTH), F32),
                        pltpu.VMEM((M_HEADS, M_HEAD_DIM, M_HEAD_DIM), F32),
                        pltpu.VMEM((8, M_HEAD_DIM), F32), pltpu.VMEM((8, LANES), F32)],
        compiler_params=_cparams("parallel", "arbitrary"),
        name="mlstm",
    )(y, y, y, y, misc, cwq, cwk, cbq, cbk, gate_bias, hg)


def _mla_prep_kernel(cq_ref, ckv_ref, misc_ref, cos_ref, sin_ref, cqg_ref, wuq_ref, wuqs_ref, ckvg_ref, wuk_ref,
                     wuv_ref, swap_ref, qng_ref, qngs_ref, kng_ref, kngs_ref, q_ref, k_ref, v_ref):
    cqn = _rms(cq_ref[...].astype(F32), cqg_ref[...]).astype(BF16)
    qa = _dot(cqn, wuq_ref[...])
    qs = _dot(cqn, wuqs_ref[...])
    ckvn = _rms(ckv_ref[...].astype(F32), ckvg_ref[...]).astype(BF16)
    ka = _dot(ckvn, wuk_ref[...])
    va = _dot(ckvn, wuv_ref[...])
    lane = lax.broadcasted_iota(jnp.int32, misc_ref.shape, 1)
    kr = jnp.where((lane >= KR_LANE) & (lane < KR_LANE + QK_ROPE), misc_ref[...], 0.0)
    kr_hi = kr.astype(BF16)
    kr_lo = (kr - kr_hi.astype(F32)).astype(BF16)
    kr_swap = _dot(kr_hi, swap_ref[...]) + _dot(kr_lo, swap_ref[...])
    cos = cos_ref[...]
    sin = sin_ref[...]
    q_cos = qng_ref[...] * cos
    q_sin = qngs_ref[...] * sin
    k_cos = kng_ref[...] * cos
    k_rot = kr_swap * (kngs_ref[...] * sin)

    def inv_rms(xh):
        return lax.rsqrt(jnp.sum(xh * xh, axis=-1, keepdims=True) * (1.0 / QK_HEAD) + NORM_EPS)

    for h in range(A_HEADS):
        sl = slice(h * HEAD_PAD, (h + 1) * HEAD_PAD)
        qh = qa[:, sl]
        kh = ka[:, sl] + kr
        q_ref[0, h, 0] = ((qh * q_cos + qs[:, sl] * q_sin)
                          * (inv_rms(qh) * (LOG2_E * QK_HEAD ** -0.5))).T.astype(BF16)
        k_ref[0, h] = ((kh * k_cos + k_rot) * inv_rms(kh)).astype(BF16)
        v_ref[0, h, 0] = jnp.where(lane < V_HEAD, va[:, sl], 1.0).T.astype(BF16)


def _mla_prep(y, misc, cos_t, sin_s, cqg, wuq, wuq_swap, ckvg, wuk, wuv, swap, qng, qng_swap, kng, kng_swap,
              bsz, seq, ts):
    t = bsz * seq
    nt = seq // ts
    row = lambda b, i: (b * nt + i, 0)
    cq_blk = (4 * M_WIDTH) // Q_LORA
    ckv_blk = (4 * M_WIDTH + Q_LORA) // KV_LORA
    hw = A_HEADS * HEAD_PAD
    return pl.pallas_call(
        _mla_prep_kernel,
        grid=(bsz, nt),
        in_specs=[pl.BlockSpec((ts, Q_LORA), lambda b, i: (b * nt + i, cq_blk)),
                  pl.BlockSpec((ts, KV_LORA), lambda b, i: (b * nt + i, ckv_blk)),
                  pl.BlockSpec((ts, LANES), row), pl.BlockSpec((ts, LANES), row), pl.BlockSpec((ts, LANES), row),
                  _const_spec((1, Q_LORA)), _const_spec((Q_LORA, hw)), _const_spec((Q_LORA, hw)),
                  _const_spec((1, KV_LORA)), _const_spec((KV_LORA, hw)), _const_spec((KV_LORA, hw)),
                  _const_spec((HEAD_PAD, HEAD_PAD)),
                  _const_spec((1, HEAD_PAD)), _const_spec((1, HEAD_PAD)),
                  _const_spec((1, HEAD_PAD)), _const_spec((1, HEAD_PAD))],
        out_specs=[pl.BlockSpec((1, A_HEADS, 1, HEAD_PAD, ts), lambda b, i: (b, 0, i, 0, 0)),
                   pl.BlockSpec((1, A_HEADS, ts, HEAD_PAD), lambda b, i: (b, 0, i, 0)),
                   pl.BlockSpec((1, A_HEADS, 1, HEAD_PAD, ts), lambda b, i: (b, 0, i, 0, 0))],
        out_shape=[jax.ShapeDtypeStruct((bsz, A_HEADS, nt, HEAD_PAD, ts), BF16),
                   jax.ShapeDtypeStruct((bsz, A_HEADS, seq, HEAD_PAD), BF16),
                   jax.ShapeDtypeStruct((bsz, A_HEADS, nt, HEAD_PAD, ts), BF16)],
        compiler_params=_cparams("parallel", "parallel"),
        name="mla_prep",
    )(y, y, misc, cos_t, sin_s, cqg, wuq, wuq_swap, ckvg, wuk, wuv, swap, qng, qng_swap, kng, kng_swap)


def _attn_kernel(q_ref, k_ref, v_ref, o_ref, *scratch, tq):
    m_refs, acc_refs = scratch[:A_HEADS], scratch[A_HEADS:]
    i = pl.program_id(1)
    key = lax.broadcasted_iota(jnp.int32, (tq, tq), 0)
    qry = lax.broadcasted_iota(jnp.int32, (tq, tq), 1)
    dmask = (key // MLA_CHUNK) <= (qry // MLA_CHUNK)
    for h in range(A_HEADS):
        m_refs[h][...] = jnp.full((1, tq), NEG_BIG, F32)
        acc_refs[h][...] = jnp.zeros((HEAD_PAD, tq), F32)

    def update(kt, masked):
        off = pl.multiple_of(kt * tq, tq)
        scores = [_dot(k_ref[0, h, pl.ds(off, tq), :], q_ref[0, h, 0]) for h in range(A_HEADS)]
        for h in range(A_HEADS):
            st = scores[h]
            if masked:
                st = jnp.where(dmask, st, NEG_BIG)
            m = m_refs[h][...]
            mn = jnp.maximum(m, jnp.max(st, axis=0, keepdims=True))
            p = jnp.exp2(st - mn)
            acc_refs[h][...] = jnp.exp2(m - mn) * acc_refs[h][...] + _dot(v_ref[0, h, kt], p.astype(BF16))
            m_refs[h][...] = mn

    def body(kt, carry):
        update(kt, False)
        return carry

    lax.fori_loop(0, i, body, 0)
    update(i, True)
    for hp in range(A_HEADS // 2):
        a0 = acc_refs[2 * hp][...]
        a1 = acc_refs[2 * hp + 1][...]
        pair = jnp.concatenate([a0[:V_HEAD] / a0[V_HEAD:V_HEAD + 1], a1[:V_HEAD] / a1[V_HEAD:V_HEAD + 1]], axis=0)
        o_ref[:, hp * LANES:(hp + 1) * LANES] = pair.T.astype(BF16)


def _attention(q_t, k, v_t, bsz, seq, tq):
    nq = seq // tq
    return pl.pallas_call(
        functools.partial(_attn_kernel, tq=tq),
        grid=(bsz, nq),
        in_specs=[pl.BlockSpec((1, A_HEADS, 1, HEAD_PAD, tq), lambda b, i: (b, 0, i, 0, 0)),
                  pl.BlockSpec((1, A_HEADS, seq, HEAD_PAD), lambda b, i: (b, 0, 0, 0)),
                  pl.BlockSpec((1, A_HEADS, nq, HEAD_PAD, tq), lambda b, i: (b, 0, 0, 0, 0))],
        out_specs=pl.BlockSpec((tq, A_WIDTH), lambda b, i: (b * nq + i, 0)),
        out_shape=jax.ShapeDtypeStruct((bsz * seq, A_WIDTH), BF16),
        scratch_shapes=([pltpu.VMEM((1, tq), F32)] * A_HEADS + [pltpu.VMEM((HEAD_PAD, tq), F32)] * A_HEADS),
        compiler_params=_cparams("parallel", "arbitrary"),
        name="mla_attention",
    )(q_t, k, v_t)


def _outproj_ffn_kernel(m_ref, a_ref, x_ref, wom_ref, woa_ref, g_ref, w1_ref, w3_ref, w2_ref, o_ref):
    x1 = x_ref[...] + _dot(m_ref[...], wom_ref[...]) + _dot(a_ref[...], woa_ref[...])
    hn = _rms(x1, g_ref[...]).astype(BF16)
    a = _dot(hn, w1_ref[...])
    b = _dot(hn, w3_ref[...])
    o_ref[...] = x1 + _dot((a * _sigmoid(a) * b).astype(BF16), w2_ref[...])


def _outproj_ffn(m_out, a_out, x, wom, woa, g, w1, w3, w2, tm):
    t = x.shape[0]
    return pl.pallas_call(
        _outproj_ffn_kernel,
        grid=(t // tm,),
        in_specs=[pl.BlockSpec((tm, M_WIDTH), lambda i: (i, 0)),
                  pl.BlockSpec((tm, A_WIDTH), lambda i: (i, 0)),
                  pl.BlockSpec((tm, D_MODEL), lambda i: (i, 0)),
                  _const_spec((M_WIDTH, D_MODEL)), _const_spec((A_WIDTH, D_MODEL)), _const_spec((1, D_MODEL)),
                  _const_spec((D_MODEL, D_FF)), _const_spec((D_MODEL, D_FF)), _const_spec((D_FF, D_MODEL))],
        out_specs=pl.BlockSpec((tm, D_MODEL), lambda i: (i, 0)),
        out_shape=jax.ShapeDtypeStruct((t, D_MODEL), F32),
        compiler_params=_cparams("parallel"),
        name="outproj_ffn",
    )(m_out, a_out, x, wom, woa, g, w1, w3, w2)


S5_CHUNK = 16
S5_ROW = S5_CHUNK * S5_GROUP


GROUPS_PER_TILE = LANES // S5_GROUP
RELAYOUT_ROWS = 32


def _lane_group_masks(rows):
    grp = lax.broadcasted_iota(jnp.int32, (rows, LANES), 1) // S5_GROUP
    return [grp == g for g in range(GROUPS_PER_TILE)]


def _to_group_major(vs, masks):
    n = GROUPS_PER_TILE
    rolled = []
    for r in range(n):
        d = vs[r % n]
        for g in range(1, n):
            d = jnp.where(masks[g], vs[(g + r) % n], d)
        rolled.append(d if r == 0 else pltpu.roll(d, r * S5_GROUP, axis=1))
    outs = []
    for grp in range(n):
        o = rolled[(-grp) % n]
        for w in range(1, n):
            o = jnp.where(masks[w], rolled[(w - grp) % n], o)
        outs.append(o)
    return outs


def _to_token_major(outs, masks):
    n = GROUPS_PER_TILE
    rolled = []
    for r in range(n):
        d = outs[(-r) % n]
        for w in range(1, n):
            d = jnp.where(masks[w], outs[(w - r) % n], d)
        rolled.append(d if r == 0 else pltpu.roll(d, LANES - r * S5_GROUP, axis=1))
    vs = []
    for w in range(n):
        v = rolled[w]
        for g in range(1, n):
            v = jnp.where(masks[g], rolled[(w - g) % n], v)
        vs.append(v)
    return vs


def _relayout_steps(tile_rows):
    nrow = tile_rows // S5_CHUNK
    return [(j, k, rg) for j in range(D_MODEL // LANES) for k in range(S5_CHUNK // GROUPS_PER_TILE)
            for rg in range(nrow // RELAYOUT_ROWS)]


def _norm_group_major_kernel(x_ref, g_ref, u_ref, un_scr):
    un = _rms(x_ref[...], g_ref[...])
    for j in range(D_MODEL // LANES):
        un_scr[j] = un[:, j * LANES:(j + 1) * LANES]
    masks = _lane_group_masks(RELAYOUT_ROWS)
    n = GROUPS_PER_TILE
    for j, k, rg in _relayout_steps(x_ref.shape[0]):
        rows = slice(rg * RELAYOUT_ROWS, (rg + 1) * RELAYOUT_ROWS)
        first = rg * RELAYOUT_ROWS * S5_CHUNK + k * n
        vs = [un_scr[j, pl.ds(first + w, RELAYOUT_ROWS, stride=S5_CHUNK), :] for w in range(n)]
        outs = _to_group_major(vs, masks)
        for grp in range(n):
            u_ref[j * n + grp, rows, k * LANES:(k + 1) * LANES] = outs[grp].astype(BF16)


def _norm_group_major(x, g, tm):
    t = x.shape[0]
    nrow = tm // S5_CHUNK
    return pl.pallas_call(
        _norm_group_major_kernel,
        grid=(t // tm,),
        in_specs=[pl.BlockSpec((tm, D_MODEL), lambda i: (i, 0)), _const_spec((1, D_MODEL))],
        out_specs=pl.BlockSpec((S5_GROUPS, nrow, S5_ROW), lambda i: (0, i, 0)),
        out_shape=jax.ShapeDtypeStruct((S5_GROUPS, t // S5_CHUNK, S5_ROW), BF16),
        scratch_shapes=[pltpu.VMEM((D_MODEL // LANES, tm, LANES), F32)],
        compiler_params=_cparams("parallel"),
        name="norm_group_major",
    )(x, g)


def _gelu_tanh(x):
    return 0.5 * x * (1.0 + jnp.tanh(math.sqrt(2.0 / math.pi) * (x + 0.044715 * (x * x * x))))


def _s5_kernel(u_ref, w_ref, cst_ref, a1_ref, a2_ref, dk_ref, z_ref, vv_scr, xp_scr, *, nch, bsz):
    two_p = 2 * S5_STATE
    u = u_ref[0]
    r = _dot(u, w_ref[0])
    vv_scr[0] = r[:, S5_ROW:S5_ROW + two_p]
    vv_scr[1] = r[:, S5_ROW + two_p:]
    a1 = a1_ref[0]
    a2 = a2_ref[0]

    def body(c, carry):
        p, q = carry
        rows = pl.ds(c, bsz, stride=nch)
        xp_scr[rows, :] = p
        pn = a1 * p + a2 * q + vv_scr[0, rows, :]
        qn = a1 * q - a2 * p + vv_scr[1, rows, :]
        return pn, qn

    zero = jnp.zeros((bsz, two_p), F32)
    lax.fori_loop(0, nch, body, (zero, zero), unroll=4)
    y = r[:, :S5_ROW] + _dot(xp_scr[...].astype(BF16), cst_ref[0]) + dk_ref[0] * u.astype(F32)
    z_ref[0] = _gelu_tanh(y).astype(BF16)


def _s5(u_g, w_all, cst, a1, a2, dk, nch, bsz):
    g, n, _ = u_g.shape
    two_p = 2 * S5_STATE
    wcols = S5_ROW + 2 * two_p
    return pl.pallas_call(
        functools.partial(_s5_kernel, nch=nch, bsz=bsz),
        grid=(g,),
        in_specs=[pl.BlockSpec((1, n, S5_ROW), lambda i: (i, 0, 0)),
                  pl.BlockSpec((1, S5_ROW, wcols), lambda i: (i, 0, 0)),
                  pl.BlockSpec((1, two_p, S5_ROW), lambda i: (i, 0, 0)),
                  pl.BlockSpec((1, 1, two_p), lambda i: (i, 0, 0)),
                  pl.BlockSpec((1, 1, two_p), lambda i: (i, 0, 0)),
                  pl.BlockSpec((1, 1, S5_ROW), lambda i: (i, 0, 0))],
        out_specs=pl.BlockSpec((1, n, S5_ROW), lambda i: (i, 0, 0)),
        out_shape=jax.ShapeDtypeStruct((g, n, S5_ROW), BF16),
        scratch_shapes=[pltpu.VMEM((2, n, two_p), F32), pltpu.VMEM((n, two_p), F32)],
        compiler_params=_cparams("parallel"),
        name="s5",
    )(u_g, w_all, cst, a1, a2, dk)


def _s5_weights(lam_re, lam_im, log_dt, b_re, b_im, c_re, c_im, d_skip):
    lc, h, p = S5_CHUNK, S5_GROUP, S5_STATE
    lam = lax.complex(jnp.minimum(lam_re.astype(F32), -1e-4), lam_im.astype(F32))
    dt = jnp.exp(log_dt.astype(F32))[:, None]
    ldt = lam * dt
    a_bar = jnp.exp(ldt)
    b_bar = ((a_bar - 1.0) / lam)[..., None] * lax.complex(b_re.astype(F32), b_im.astype(F32))
    c_mat = lax.complex(c_re.astype(F32), c_im.astype(F32))
    steps = jnp.arange(lc + 1, dtype=F32)
    apow = jnp.exp(ldt[None] * steps[:, None, None].astype(ldt.dtype))
    g = lam.shape[0]
    ker = jnp.einsum('gip,kgp,gpj->gkij', c_mat, apow[:lc], b_bar).real
    lag = np.arange(lc)[None, :] - np.arange(lc)[:, None]
    toe = jnp.where((lag >= 0)[None, :, :, None, None], ker[:, np.clip(lag, 0, lc - 1)], 0.0)
    toe = toe.transpose(0, 1, 4, 2, 3).reshape(g, lc * h, lc * h)
    wst = apow[:lc][::-1].transpose(1, 0, 2)[:, :, None, :] * jnp.swapaxes(b_bar, 1, 2)[:, None, :, :]
    wst = wst.reshape(g, lc * h, p)
    w_all = jnp.concatenate([toe, wst.real, wst.imag, wst.imag, wst.real], axis=-1).astype(BF16)
    ca = c_mat[:, None, :, :] * apow[1:lc + 1].transpose(1, 0, 2)[:, :, None, :]
    ca = ca.reshape(g, lc * h, p)
    cst = jnp.concatenate([ca.real, -ca.imag], axis=-1).transpose(0, 2, 1).astype(BF16)
    al = apow[lc]
    a1 = jnp.concatenate([al.real, al.real], axis=-1)[:, None, :]
    a2 = jnp.concatenate([-al.imag, al.imag], axis=-1)[:, None, :]
    dk = jnp.tile(d_skip.astype(F32).reshape(g, 1, h), (1, lc, 1)).reshape(g, 1, lc * h)
    return w_all, cst, a1, a2, dk


def _glu_kernel(z_ref, x_ref, wv_ref, wg_ref, g_ref, x2_ref, xn_ref, zt_scr):
    masks = _lane_group_masks(RELAYOUT_ROWS)
    n = GROUPS_PER_TILE
    for j, k, rg in _relayout_steps(x_ref.shape[0]):
        rows = slice(rg * RELAYOUT_ROWS, (rg + 1) * RELAYOUT_ROWS)
        outs = [z_ref[j * n + grp, rows, k * LANES:(k + 1) * LANES].astype(F32) for grp in range(n)]
        vs = _to_token_major(outs, masks)
        first = rg * RELAYOUT_ROWS * S5_CHUNK + k * n
        for w in range(n):
            zt_scr[j, pl.ds(first + w, RELAYOUT_ROWS, stride=S5_CHUNK), :] = vs[w]
    z = jnp.concatenate([zt_scr[j] for j in range(D_MODEL // LANES)], axis=1).astype(BF16)
    x2 = x_ref[...] + _dot(z, wv_ref[...]) * _sigmoid(_dot(z, wg_ref[...]))
    x2_ref[...] = x2
    xn_ref[...] = _rms(x2, g_ref[...]).astype(BF16)


def _glu(z_g, x, wv, wg, g, tm):
    t = x.shape[0]
    nrow = tm // S5_CHUNK
    return pl.pallas_call(
        _glu_kernel,
        grid=(t // tm,),
        in_specs=[pl.BlockSpec((S5_GROUPS, nrow, S5_ROW), lambda i: (0, i, 0)),
                  pl.BlockSpec((tm, D_MODEL), lambda i: (i, 0)),
                  _const_spec((D_MODEL, D_MODEL)), _const_spec((D_MODEL, D_MODEL)), _const_spec((1, D_MODEL))],
        out_specs=[pl.BlockSpec((tm, D_MODEL), lambda i: (i, 0)), pl.BlockSpec((tm, D_MODEL), lambda i: (i, 0))],
        out_shape=[jax.ShapeDtypeStruct((t, D_MODEL), F32), jax.ShapeDtypeStruct((t, D_MODEL), BF16)],
        scratch_shapes=[pltpu.VMEM((D_MODEL // LANES, tm, LANES), F32)],
        compiler_params=_cparams("parallel"),
        name="glu",
    )(z_g, x, wv, wg, g)


def _router_kernel(xn_ref, wh_ref, wl_ref, pos_ref, gate_ref, posc_ref, cnt_ref, *, tb):
    xn = xn_ref[...]
    lg = (_dot_nt(wh_ref[...], xn) + _dot_nt(wl_ref[...], xn))[0:N_EXPERTS]
    eidx = lax.broadcasted_iota(jnp.int32, (N_EXPERTS, tb), 0)
    m1 = jnp.max(lg, axis=0, keepdims=True)
    i1 = jnp.min(jnp.where(lg == m1, eidx, N_EXPERTS), axis=0, keepdims=True)
    rest = jnp.where(eidx == i1, -jnp.inf, lg)
    m2 = jnp.max(rest, axis=0, keepdims=True)
    i2 = jnp.min(jnp.where(rest == m2, eidx, N_EXPERTS), axis=0, keepdims=True)
    e21 = jnp.exp(m2 - m1)
    p1 = 1.0 / (1.0 + e21)
    sel1 = eidx == i1
    sel2 = eidx == i2
    sel = sel1 | sel2
    gate_ref[...] = jnp.where(sel1, p1, jnp.where(sel2, e21 * p1, 0.0))
    onehot = jnp.concatenate([sel.astype(F32), jnp.zeros((N_EXPERTS, tb), F32)], axis=0).astype(BF16)
    srow = lax.broadcasted_iota(jnp.int32, (tb, tb), 0)
    scol = lax.broadcasted_iota(jnp.int32, (tb, tb), 1)
    rank = _dot(onehot, (srow < scol).astype(BF16))[0:N_EXPERTS]
    pos = jnp.where(sel, rank, -1.0)
    pos_ref[...] = pos
    posc_ref[...] = jnp.concatenate([pos, jnp.full((LANES - N_EXPERTS, tb), -1.0, F32)], axis=0).T
    cnt = jnp.sum(sel.astype(F32), axis=1, keepdims=True)
    cnt_ref[0] = jnp.broadcast_to(cnt, (N_EXPERTS, LANES))


def _router(xn, wr_hi, wr_lo, tb):
    t = xn.shape[0]
    nb = t // tb
    return pl.pallas_call(
        functools.partial(_router_kernel, tb=tb),
        grid=(nb,),
        in_specs=[pl.BlockSpec((tb, D_MODEL), lambda i: (i, 0)),
                  _const_spec((2 * N_EXPERTS, D_MODEL)), _const_spec((2 * N_EXPERTS, D_MODEL))],
        out_specs=[pl.BlockSpec((N_EXPERTS, tb), lambda i: (0, i)), pl.BlockSpec((N_EXPERTS, tb), lambda i: (0, i)),
                   pl.BlockSpec((tb, LANES), lambda i: (i, 0)),
                   pl.BlockSpec((1, N_EXPERTS, LANES), lambda i: (i, 0, 0))],
        out_shape=[jax.ShapeDtypeStruct((N_EXPERTS, t), F32), jax.ShapeDtypeStruct((N_EXPERTS, t), F32),
                   jax.ShapeDtypeStruct((t, LANES), F32), jax.ShapeDtypeStruct((nb, N_EXPERTS, LANES), F32)],
        compiler_params=_cparams("parallel"),
        name="router",
    )(xn, wr_hi, wr_lo)


MOE_UNIT = 128
MOE_ALIGN = 16
MOE_TM = 512


def _moe_plan(cnt, n_tokens):
    nb = cnt.shape[0]
    seg = (cnt + MOE_ALIGN - 1) // MOE_ALIGN * MOE_ALIGN
    used = jnp.sum(seg, axis=0)
    gsize = (used + MOE_UNIT + MOE_TM - 1) // MOE_TM * MOE_TM
    gstart = jnp.cumsum(gsize) - gsize
    dest = gstart[None, :] + jnp.cumsum(seg, axis=0) - seg
    max_rows = 2 * n_tokens + N_EXPERTS * (nb * (MOE_ALIGN - 1) + MOE_UNIT + MOE_TM - 1)
    n_tiles = -(-max_rows // MOE_TM)
    tile_start = jnp.arange(n_tiles, dtype=jnp.int32) * MOE_TM
    tile_e = jnp.minimum(jnp.sum(tile_start[:, None] >= (gstart + gsize)[None, :], axis=1), N_EXPERTS - 1)
    tile_valid = tile_start < (gstart + used)[tile_e]
    return dest.reshape(-1).astype(jnp.int32), tile_e.astype(jnp.int32), tile_valid.astype(jnp.int32), n_tiles


def _moe_pack_kernel(dest_ref, cnt_ref, xn_ref, pos_ref, xs_in_ref, xs_ref, stage, sem, *, tb):
    del xs_in_ref
    b = pl.program_id(0)
    unit = MOE_UNIT

    def n_units(e):
        return (cnt_ref[b * N_EXPERTS + e] + unit - 1) // unit

    def chunk_copy(e, r):
        dst = pl.multiple_of(dest_ref[b * N_EXPERTS + e] + r * unit, MOE_ALIGN)
        return pltpu.make_async_copy(stage.at[e % 2, pl.ds(pl.multiple_of(r * unit, unit), unit)],
                                     xs_ref.at[pl.ds(dst, unit)], sem.at[e % 2])

    def wait_expert(e):
        def body(r, carry):
            chunk_copy(e, r).wait()
            return carry
        lax.fori_loop(0, n_units(e), body, 0)

    for e in range(N_EXPERTS):
        if e >= 2:
            wait_expert(e - 2)
        prow = pos_ref[e:e + 1, :]
        n_unit = n_units(e)
        n_pair = n_unit // 2

        def gather(base, m, e=e, prow=prow):
            slot = lax.broadcasted_iota(jnp.int32, (m, tb), 0).astype(F32) + base.astype(F32)
            stage[e % 2, pl.ds(base, m), :] = _dot((prow == slot).astype(BF16), xn_ref[...]).astype(BF16)

        def pair(r, carry, e=e, gather=gather):
            gather(pl.multiple_of(r * (2 * unit), 2 * unit), 2 * unit)
            chunk_copy(e, 2 * r).start()
            chunk_copy(e, 2 * r + 1).start()
            return carry

        lax.fori_loop(0, n_pair, pair, 0)

        @pl.when(n_unit % 2 == 1)
        def _(e=e, gather=gather, n_pair=n_pair):
            gather(pl.multiple_of(n_pair * (2 * unit), 2 * unit), unit)
            chunk_copy(e, 2 * n_pair).start()
    wait_expert(N_EXPERTS - 2)
    wait_expert(N_EXPERTS - 1)


def _moe_pack(dest, cnt, xn, pos, xs_init, tb):
    t = xn.shape[0]
    grid_spec = pltpu.PrefetchScalarGridSpec(
        num_scalar_prefetch=2,
        grid=(t // tb,),
        in_specs=[pl.BlockSpec((tb, D_MODEL), lambda b, d, c: (b, 0)),
                  pl.BlockSpec((N_EXPERTS, tb), lambda b, d, c: (0, b)),
                  pl.BlockSpec(memory_space=pl.ANY)],
        out_specs=pl.BlockSpec(memory_space=pl.ANY),
        scratch_shapes=[pltpu.VMEM((2, tb, D_MODEL), BF16), pltpu.SemaphoreType.DMA((2,))],
    )
    return pl.pallas_call(
        functools.partial(_moe_pack_kernel, tb=tb),
        grid_spec=grid_spec,
        out_shape=jax.ShapeDtypeStruct(xs_init.shape, BF16),
        input_output_aliases={4: 0},
        compiler_params=_cparams("arbitrary"),
        name="moe_pack",
    )(dest, cnt, xn, pos, xs_init)


def _moe_ffn_kernel(te_ref, tv_ref, xs_ref, w1_ref, w3_ref, w2_ref, y_ref):
    del te_ref
    valid = tv_ref[pl.program_id(0)] != 0

    @pl.when(valid)
    def _():
        xe = xs_ref[...]
        a = _dot(xe, w1_ref[0])
        g = (a * _sigmoid(a) * _dot(xe, w3_ref[0])).astype(BF16)
        y_ref[...] = _dot(g, w2_ref[0]).astype(BF16)

    @pl.when(jnp.logical_not(valid))
    def _():
        y_ref[...] = jnp.zeros(y_ref.shape, BF16)


def _moe_ffn(tile_e, tile_valid, xs, w1, w3, w2, n_tiles):
    w_up = pl.BlockSpec((1, D_MODEL, D_FF_EXPERT), lambda i, te, tv: (te[i], 0, 0), pipeline_mode=pl.Buffered(1))
    w_down = pl.BlockSpec((1, D_FF_EXPERT, D_MODEL), lambda i, te, tv: (te[i], 0, 0), pipeline_mode=pl.Buffered(1))
    grid_spec = pltpu.PrefetchScalarGridSpec(
        num_scalar_prefetch=2,
        grid=(n_tiles,),
        in_specs=[pl.BlockSpec((MOE_TM, D_MODEL), lambda i, te, tv: (i, 0)), w_up, w_up, w_down],
        out_specs=pl.BlockSpec((MOE_TM, D_MODEL), lambda i, te, tv: (i, 0)),
    )
    return pl.pallas_call(
        _moe_ffn_kernel,
        grid_spec=grid_spec,
        out_shape=jax.ShapeDtypeStruct((n_tiles * MOE_TM, D_MODEL), BF16),
        compiler_params=_cparams("parallel"),
        name="moe_ffn",
    )(tile_e, tile_valid, xs, w1, w3, w2)


def _moe_combine_kernel(dest_ref, cnt_ref, x_ref, pos_ref, gate_ref, posc_ref, ys_ref, o_ref, ybuf, sem, *, tb):
    b = pl.program_id(0)
    unit = MOE_UNIT

    def n_units(e):
        return (cnt_ref[b * N_EXPERTS + e] + unit - 1) // unit

    def chunk_copy(e, r):
        src = pl.multiple_of(dest_ref[b * N_EXPERTS + e] + r * unit, MOE_ALIGN)
        return pltpu.make_async_copy(ys_ref.at[pl.ds(src, unit)],
                                     ybuf.at[e % 2, pl.ds(pl.multiple_of(r * unit, unit), unit)], sem.at[e % 2])

    def start_expert(e):
        def body(r, carry):
            chunk_copy(e, r).start()
            return carry
        lax.fori_loop(0, n_units(e), body, 0)

    def wait_expert(e):
        def body(r, carry):
            chunk_copy(e, r).wait()
            return carry
        lax.fori_loop(0, n_units(e), body, 0)

    start_expert(0)
    o_ref[...] = x_ref[...]
    lane = lax.broadcasted_iota(jnp.int32, (tb, LANES), 1)
    for e in range(N_EXPERTS):
        if e + 1 < N_EXPERTS:
            start_expert(e + 1)
        wait_expert(e)
        prow = pos_ref[e:e + 1, :]
        grow = gate_ref[e:e + 1, :]
        n_unit = n_units(e)
        n_pair = n_unit // 2

        def scatter_chunk(base, m, e=e, prow=prow, grow=grow):
            hit = prow == lax.broadcasted_iota(jnp.int32, (m, tb), 0).astype(F32) + base.astype(F32)
            gs = jnp.sum(jnp.where(hit, grow, 0.0), axis=-1, keepdims=True)
            yg = (ybuf[e % 2, pl.ds(base, m), :].astype(F32) * gs).astype(BF16)
            o_ref[...] += _dot_tn(hit.astype(BF16), yg)

        def scatter_pair(r, carry, scatter_chunk=scatter_chunk):
            scatter_chunk(pl.multiple_of(r * (2 * unit), 2 * unit), 2 * unit)
            return carry

        lax.fori_loop(0, n_pair, scatter_pair, 0)

        @pl.when(n_unit % 2 == 1)
        def _(scatter_chunk=scatter_chunk, n_pair=n_pair):
            scatter_chunk(pl.multiple_of(n_pair * (2 * unit), 2 * unit), unit)


def _moe_combine(dest, cnt, x, pos, gate, posc, ys, tb):
    t = x.shape[0]
    grid_spec = pltpu.PrefetchScalarGridSpec(
        num_scalar_prefetch=2,
        grid=(t // tb,),
        in_specs=[pl.BlockSpec((tb, D_MODEL), lambda b, d, c: (b, 0)),
                  pl.BlockSpec((N_EXPERTS, tb), lambda b, d, c: (0, b)),
                  pl.BlockSpec((N_EXPERTS, tb), lambda b, d, c: (0, b)),
                  pl.BlockSpec((tb, LANES), lambda b, d, c: (b, 0)),
                  pl.BlockSpec(memory_space=pl.ANY)],
        out_specs=pl.BlockSpec((tb, D_MODEL), lambda b, d, c: (b, 0)),
        scratch_shapes=[pltpu.VMEM((2, tb, D_MODEL), BF16), pltpu.SemaphoreType.DMA((2,))],
    )
    return pl.pallas_call(
        functools.partial(_moe_combine_kernel, tb=tb),
        grid_spec=grid_spec,
        out_shape=jax.ShapeDtypeStruct((t, D_MODEL), F32),
        compiler_params=_cparams("parallel"),
        name="moe_combine",
    )(dest, cnt, x, pos, gate, posc, ys)


def _moe(cnt, xn, x, pos, gate, posc, w1, w3, w2, tb, xs_init):
    t = x.shape[0]
    dest, tile_e, tile_valid, n_tiles = _moe_plan(cnt, t)
    cnt = cnt.reshape(-1)
    if xs_init is None:
        xs_init = jnp.zeros((n_tiles * MOE_TM, D_MODEL), BF16)
    xs = _moe_pack(dest, cnt, xn, pos, xs_init, tb)
    ys = _moe_ffn(tile_e, tile_valid, xs, w1, w3, w2, n_tiles)
    return _moe_combine(dest, cnt, x, pos, gate, posc, ys, tb), xs


def _rope_tables(positions):
    half = QK_ROPE // 2
    inv_freq = ROPE_THETA ** (-jnp.arange(0, QK_ROPE, 2, dtype=F32) / QK_ROPE)
    ang = positions.astype(F32).reshape(-1, 1) * inv_freq
    cos, sin = jnp.cos(ang), jnp.sin(ang)
    t = ang.shape[0]
    ones_lo = jnp.ones((t, QK_NOPE), F32)
    zeros_lo = jnp.zeros((t, QK_NOPE), F32)
    tail1 = jnp.ones((t, HEAD_PAD - QK_HEAD), F32)
    tail0 = jnp.zeros((t, HEAD_PAD - QK_HEAD), F32)
    cos_t = jnp.concatenate([ones_lo, cos, cos, tail1], axis=-1)
    sin_s = jnp.concatenate([zeros_lo, -sin, sin, tail0], axis=-1)
    return cos_t, sin_s


def _pad_heads(w, per_head_in, keep, heads):
    k = w.shape[0]
    w = w.reshape(k, heads, per_head_in)[:, :, :keep]
    return jnp.pad(w, ((0, 0), (0, 0), (0, HEAD_PAD - keep))).reshape(k, heads * HEAD_PAD)


def _even_layer(x, rope, bsz, seq, norm1_g, w_in, conv_w, conv_b, b_igate, b_fgate, mh_norm_g, cq_norm_g, w_uq,
                ckv_norm_g, w_ukv, qn_g, kn_g, w_o, norm2_g, ffn_w1, ffn_w3, ffn_w2):
    row = lambda v: v.astype(F32).reshape(1, -1)
    mw = M_WIDTH
    o_ig, o_fg, o_og = 3 * mw, 3 * mw + M_HEADS, 3 * mw + 2 * M_HEADS
    o_cq = o_og + mw
    o_ckv, o_kr = o_cq + Q_LORA, o_cq + Q_LORA + KV_LORA
    misc_w = jnp.zeros((D_MODEL, LANES), F32)
    misc_w = misc_w.at[:, IG_LANE:IG_LANE + M_HEADS].set(w_in[:, o_ig:o_fg])
    misc_w = misc_w.at[:, FG_LANE:FG_LANE + M_HEADS].set(w_in[:, o_fg:o_og])
    misc_w = misc_w.at[:, KR_LANE:KR_LANE + QK_ROPE].set(w_in[:, o_kr:o_kr + QK_ROPE])
    w_packed = jnp.concatenate([w_in[:, :3 * mw], w_in[:, o_og:o_cq], w_in[:, o_cq:o_ckv], w_in[:, o_ckv:o_kr],
                                misc_w], axis=-1).astype(BF16)
    gate_bias = jnp.zeros((1, LANES), F32)
    gate_bias = gate_bias.at[0, IG_LANE:IG_LANE + M_HEADS].set(b_igate.astype(F32))
    gate_bias = gate_bias.at[0, FG_LANE:FG_LANE + M_HEADS].set(b_fgate.astype(F32))

    tm = min(512, x.shape[0])
    y, misc = _inproj(x, row(norm1_g), w_packed, tm)

    lc = min(256, seq)
    m_out = _mlstm(y, misc, conv_w[:, :mw].astype(F32), conv_w[:, mw:].astype(F32), row(conv_b[:mw]),
                   row(conv_b[mw:]), gate_bias, row(mh_norm_g), bsz, seq, lc)

    wuq = _pad_heads(w_uq, QK_HEAD, QK_HEAD, A_HEADS).astype(BF16)
    wuk = _pad_heads(w_ukv, QK_NOPE + V_HEAD, QK_NOPE, A_HEADS).astype(BF16)
    wuv = _pad_heads(w_ukv.reshape(KV_LORA, A_HEADS, QK_NOPE + V_HEAD)[:, :, QK_NOPE:].reshape(KV_LORA, A_WIDTH),
                     V_HEAD, V_HEAD, A_HEADS).astype(BF16)
    pad_g = lambda g: jnp.pad(g.astype(F32), (0, HEAD_PAD - QK_HEAD)).reshape(1, HEAD_PAD)
    half = QK_ROPE // 2

    def swap_halves(v):
        r1, r2 = v[..., QK_NOPE:QK_NOPE + half], v[..., QK_NOPE + half:QK_HEAD]
        return jnp.concatenate([jnp.zeros_like(v[..., :QK_NOPE]), r2, r1], axis=-1)

    wuq_swap = _pad_heads(swap_halves(w_uq.reshape(Q_LORA, A_HEADS, QK_HEAD)).reshape(Q_LORA, A_HEADS * QK_HEAD),
                          QK_HEAD, QK_HEAD, A_HEADS).astype(BF16)
    swap = swap_halves(jnp.eye(HEAD_PAD, dtype=F32)[:, :QK_HEAD])
    swap = jnp.pad(swap, ((0, 0), (0, HEAD_PAD - QK_HEAD))).astype(BF16)
    ts = min(256, seq)
    q, k, v = _mla_prep(y, misc, rope[0], rope[1], row(cq_norm_g), wuq, wuq_swap, row(ckv_norm_g), wuk, wuv, swap,
                        pad_g(qn_g), pad_g(swap_halves(qn_g)), pad_g(kn_g), pad_g(swap_halves(kn_g)), bsz, seq, ts)
    a_out = _attention(q, k, v, bsz, seq, ts)

    return _outproj_ffn(m_out, a_out, x, w_o[:mw].astype(BF16), w_o[mw:].astype(BF16), row(norm2_g),
                        ffn_w1.astype(BF16), ffn_w3.astype(BF16), ffn_w2.astype(BF16), tm)


def _odd_layer(x, bsz, seq, norm1_g, lam_re, lam_im, log_dt, b_re, b_im, c_re, c_im, d_skip, glu_wv, glu_wg,
               norm2_g, router_w, moe_w1, moe_w3, moe_w2, xs_init=None):
    row = lambda v: v.astype(F32).reshape(1, -1)
    t = x.shape[0]
    tm = min(512, t)
    nch = seq // S5_CHUNK
    tr = min(1024, seq)
    u_g = _norm_group_major(x, row(norm1_g), tr)
    w_all, cst, a1, a2, dk = _s5_weights(lam_re, lam_im, log_dt, b_re, b_im, c_re, c_im, d_skip)
    z_g = _s5(u_g, w_all, cst, a1, a2, dk, nch, bsz)
    x2, xn = _glu(z_g, x, glu_wv.astype(BF16), glu_wg.astype(BF16), row(norm2_g), tr)

    tb = min(1024, t)
    wr = jnp.pad(router_w.astype(F32).T, ((0, N_EXPERTS), (0, 0)))
    wr_hi = wr.astype(BF16)
    wr_lo = (wr - wr_hi.astype(F32)).astype(BF16)
    pos, gate, posc, cnt = _router(xn, wr_hi, wr_lo, tb)
    cnt = cnt[:, :, 0].astype(jnp.int32)
    return _moe(cnt, xn, x2, pos, gate, posc, moe_w1.astype(BF16), moe_w3.astype(BF16), moe_w2.astype(BF16), tb,
                xs_init)


def kernel(x, positions, l0_norm1_g, l0_w_in, l0_conv_w, l0_conv_b, l0_b_igate, l0_b_fgate, l0_mh_norm_g, l0_cq_norm_g, l0_w_uq, l0_ckv_norm_g, l0_w_ukv, l0_qn_g, l0_kn_g, l0_w_o, l0_norm2_g, l0_ffn_w1, l0_ffn_w3, l0_ffn_w2, l1_norm1_g, l1_lam_re, l1_lam_im, l1_log_dt, l1_b_re, l1_b_im, l1_c_re, l1_c_im, l1_d_skip, l1_glu_wv, l1_glu_wg, l1_norm2_g, l1_router_w, l1_moe_w1, l1_moe_w3, l1_moe_w2, l2_norm1_g, l2_w_in, l2_conv_w, l2_conv_b, l2_b_igate, l2_b_fgate, l2_mh_norm_g, l2_cq_norm_g, l2_w_uq, l2_ckv_norm_g, l2_w_ukv, l2_qn_g, l2_kn_g, l2_w_o, l2_norm2_g, l2_ffn_w1, l2_ffn_w3, l2_ffn_w2, l3_norm1_g, l3_lam_re, l3_lam_im, l3_log_dt, l3_b_re, l3_b_im, l3_c_re, l3_c_im, l3_d_skip, l3_glu_wv, l3_glu_wg, l3_norm2_g, l3_router_w, l3_moe_w1, l3_moe_w3, l3_moe_w2):
    bsz, seq, _ = x.shape
    rope = _rope_tables(positions)
    h = x.reshape(bsz * seq, D_MODEL)
    h = _even_layer(h, rope, bsz, seq, l0_norm1_g, l0_w_in, l0_conv_w, l0_conv_b, l0_b_igate, l0_b_fgate,
                    l0_mh_norm_g, l0_cq_norm_g, l0_w_uq, l0_ckv_norm_g, l0_w_ukv, l0_qn_g, l0_kn_g, l0_w_o,
                    l0_norm2_g, l0_ffn_w1, l0_ffn_w3, l0_ffn_w2)
    h, xs = _odd_layer(h, bsz, seq, l1_norm1_g, l1_lam_re, l1_lam_im, l1_log_dt, l1_b_re, l1_b_im, l1_c_re, l1_c_im,
                       l1_d_skip, l1_glu_wv, l1_glu_wg, l1_norm2_g, l1_router_w, l1_moe_w1, l1_moe_w3, l1_moe_w2)
    h = _even_layer(h, rope, bsz, seq, l2_norm1_g, l2_w_in, l2_conv_w, l2_conv_b, l2_b_igate, l2_b_fgate,
                    l2_mh_norm_g, l2_cq_norm_g, l2_w_uq, l2_ckv_norm_g, l2_w_ukv, l2_qn_g, l2_kn_g, l2_w_o,
                    l2_norm2_g, l2_ffn_w1, l2_ffn_w3, l2_ffn_w2)
    h, _ = _odd_layer(h, bsz, seq, l3_norm1_g, l3_lam_re, l3_lam_im, l3_log_dt, l3_b_re, l3_b_im, l3_c_re, l3_c_im,
                      l3_d_skip, l3_glu_wv, l3_glu_wg, l3_norm2_g, l3_router_w, l3_moe_w1, l3_moe_w3, l3_moe_w2, xs)
    return h.reshape(bsz, seq, D_MODEL)
```

```python
import functools
import math

import jax
import jax.numpy as jnp
import numpy as np
from jax import lax
from jax.experimental import pallas as pl
from jax.experimental.pallas import tpu as pltpu

F32 = jnp.float32
BF16 = jnp.bfloat16

D_MODEL = 1024
NORM_EPS = 1e-6
M_HEADS = 4
M_HEAD_DIM = 128
M_WIDTH = M_HEADS * M_HEAD_DIM
CONV_WIDTH = 4
A_HEADS = 8
Q_LORA = 256
KV_LORA = 128
QK_NOPE = 64
QK_ROPE = 32
QK_HEAD = QK_NOPE + QK_ROPE
V_HEAD = 64
A_WIDTH = A_HEADS * V_HEAD
ROPE_THETA = 10000.0
MLA_CHUNK = 64
S5_GROUP = 16
S5_GROUPS = D_MODEL // S5_GROUP
S5_STATE = 64
D_FF = 2816
N_EXPERTS = 8
D_FF_EXPERT = 3584

LANES = 128
HEAD_PAD = 128
Y_COLS = 4 * M_WIDTH + Q_LORA + KV_LORA
IG_LANE = 0
FG_LANE = 8
KR_LANE = QK_NOPE

VMEM_LIMIT = 56 * 1024 * 1024
NEG_BIG = -1e30
LOG2_E = math.log2(math.e)


def _cparams(*sem):
    return pltpu.CompilerParams(dimension_semantics=sem, vmem_limit_bytes=VMEM_LIMIT)


def _rms(x, g):
    return x * lax.rsqrt(jnp.mean(x * x, axis=-1, keepdims=True) + NORM_EPS) * g


def _sigmoid(x):
    return 1.0 / (1.0 + jnp.exp(-x))


def _dot(a, b):
    return jnp.dot(a, b, preferred_element_type=F32)


def _dot_nt(a, b):
    return lax.dot_general(a, b, (((1,), (1,)), ((), ())), preferred_element_type=F32)


def _dot_tn(a, b):
    return lax.dot_general(a, b, (((0,), (0,)), ((), ())), preferred_element_type=F32)


def _const_spec(shape):
    nd = len(shape)
    return pl.BlockSpec(shape, lambda *_: (0,) * nd, pipeline_mode=pl.Buffered(1))


def _inproj_kernel(x_ref, g_ref, w_ref, y_ref, misc_ref):
    xn = _rms(x_ref[...], g_ref[...]).astype(BF16)
    y = _dot(xn, w_ref[...])
    y_ref[...] = y[:, :Y_COLS].astype(BF16)
    misc_ref[...] = y[:, Y_COLS:]


def _inproj(x, g, w_packed, tm):
    t = x.shape[0]
    return pl.pallas_call(
        _inproj_kernel,
        grid=(t // tm,),
        in_specs=[pl.BlockSpec((tm, D_MODEL), lambda i: (i, 0)),
                  _const_spec((1, D_MODEL)),
                  _const_spec((D_MODEL, Y_COLS + LANES))],
        out_specs=[pl.BlockSpec((tm, Y_COLS), lambda i: (i, 0)),
                   pl.BlockSpec((tm, LANES), lambda i: (i, 0))],
        out_shape=[jax.ShapeDtypeStruct((t, Y_COLS), BF16),
                   jax.ShapeDtypeStruct((t, LANES), F32)],
        compiler_params=_cparams("parallel"),
        name="inproj",
    )(x, g, w_packed)


def _cummax_lanes(r, width):
    lane = lax.broadcasted_iota(jnp.int32, r.shape, 1)
    s = 1
    while s < width:
        shifted = pltpu.roll(r, s, axis=1)
        r = jnp.where(lane >= s, jnp.maximum(r, shifted), r)
        s *= 2
    return r


def _mlstm_kernel(q_ref, k_ref, v_ref, og_ref, misc_ref, cwq_ref, cwk_ref, cbq_ref, cbk_ref, gb_ref, hg_ref,
                  out_ref, qpad, kpad, c_scr, n_scr, m_scr, *, lc):
    @pl.when(pl.program_id(1) == 0)
    def _():
        qpad[0:8, :] = jnp.zeros((8, M_WIDTH), F32)
        kpad[0:8, :] = jnp.zeros((8, M_WIDTH), F32)
        c_scr[...] = jnp.zeros_like(c_scr)
        n_scr[...] = jnp.zeros_like(n_scr)
        m_scr[...] = jnp.zeros_like(m_scr)

    qpad[8:8 + lc, :] = q_ref[...].astype(F32)
    kpad[8:8 + lc, :] = k_ref[...].astype(F32)

    def conv_silu(pad, cw_ref, cb_ref):
        acc = jnp.broadcast_to(cb_ref[...], (lc, M_WIDTH))
        for j in range(CONV_WIDTH):
            off = 8 - (CONV_WIDTH - 1) + j
            acc = acc + pad[off:off + lc, :] * cw_ref[j:j + 1, :]
        return acc * _sigmoid(acc)

    qc = conv_silu(qpad, cwq_ref, cbq_ref) * (M_HEAD_DIM ** -0.5)
    kc = conv_silu(kpad, cwk_ref, cbk_ref)
    qpad[0:8, :] = qpad[lc:lc + 8, :]
    kpad[0:8, :] = kpad[lc:lc + 8, :]

    gt = (misc_ref[...] + gb_ref[...]).T
    ig = gt[IG_LANE:IG_LANE + 8, :]
    fpre = gt[FG_LANE:FG_LANE + 8, :]
    lf = jnp.minimum(fpre, 0.0) - jnp.log(1.0 + jnp.exp(-jnp.abs(fpre)))
    hi = lf.astype(BF16).astype(F32)
    r1 = lf - hi
    mid = r1.astype(BF16).astype(F32)
    lo = r1 - mid
    stack = jnp.concatenate([hi, mid, lo, jnp.zeros_like(lo)], axis=0).astype(BF16)
    srow = lax.broadcasted_iota(jnp.int32, (lc, lc), 0)
    scol = lax.broadcasted_iota(jnp.int32, (lc, lc), 1)
    upper = (srow <= scol).astype(BF16)
    cs = _dot(stack, upper)
    b = cs[0:8] + cs[8:16] + cs[16:24]

    r = ig - b
    cm = _cummax_lanes(r, lc)
    mprev = m_scr[:, 0:1]
    mx = jnp.maximum(mprev, cm)
    w_inter = jnp.exp(mprev - mx)
    e_neg = jnp.exp(-(b + mx))
    total = b[:, lc - 1:lc]
    mnew = total + mx[:, lc - 1:lc]
    a_prev = jnp.exp(total + mprev - mnew)
    w_in = jnp.exp(total + r - mnew)
    m_scr[...] = jnp.broadcast_to(mnew, (8, LANES))

    cols = jnp.concatenate([-mx, w_inter, e_neg, w_in, jnp.zeros((LANES - 32, lc), F32)], axis=0).T
    tril = scol <= srow

    heads = [slice(h * M_HEAD_DIM, (h + 1) * M_HEAD_DIM) for h in range(M_HEADS)]
    qbs = [qc[:, sl].astype(BF16) for sl in heads]
    qk = [_dot_nt(qbs[h], kc[:, heads[h]].astype(BF16)) for h in range(M_HEADS)]
    qcs = [_dot(qbs[h], c_scr[h].astype(BF16)) for h in range(M_HEADS)]
    for h in range(M_HEADS):
        sl = heads[h]
        qh = qc[:, sl]
        kh = kc[:, sl]
        vh = v_ref[:, sl]
        u_col = cols[:, h:h + 1]
        wi_col = cols[:, 8 + h:9 + h]
        en_col = cols[:, 16 + h:17 + h]
        win_col = cols[:, 24 + h:25 + h]
        dmat = jnp.where(tril, u_col + r[h:h + 1, :], NEG_BIG)
        p = jnp.exp(dmat) * qk[h]
        n_h = n_scr[h:h + 1, :]
        num = wi_col * qcs[h] + _dot(p.astype(BF16), vh)
        den = wi_col * jnp.sum(qh * n_h, axis=-1, keepdims=True) + jnp.sum(p, axis=-1, keepdims=True)
        ht = num / jnp.maximum(jnp.abs(den), en_col)
        kw = kh * win_col
        ah = a_prev[h:h + 1, :]
        c_scr[h] = ah * c_scr[h] + _dot_tn(kw.astype(BF16), vh)
        n_scr[h:h + 1, :] = ah * n_h + jnp.sum(kw, axis=0, keepdims=True)
        hn = _rms(ht, hg_ref[:, sl])
        out_ref[:, sl] = (_sigmoid(og_ref[:, sl].astype(F32)) * hn).astype(BF16)


def _mlstm(y, misc, cwq, cwk, cbq, cbk, gate_bias, hg, bsz, seq, lc):
    t = bsz * seq
    nch = seq // lc
    row = lambda b, c: b * nch + c
    return pl.pallas_call(
        functools.partial(_mlstm_kernel, lc=lc),
        grid=(bsz, nch),
        in_specs=[pl.BlockSpec((lc, M_WIDTH), lambda b, c: (row(b, c), 0)),
                  pl.BlockSpec((lc, M_WIDTH), lambda b, c: (row(b, c), 1)),
                  pl.BlockSpec((lc, M_WIDTH), lambda b, c: (row(b, c), 2)),
                  pl.BlockSpec((lc, M_WIDTH), lambda b, c: (row(b, c), 3)),
                  pl.BlockSpec((lc, LANES), lambda b, c: (row(b, c), 0)),
                  _const_spec((CONV_WIDTH, M_WIDTH)), _const_spec((CONV_WIDTH, M_WIDTH)),
                  _const_spec((1, M_WIDTH)), _const_spec((1, M_WIDTH)),
                  _const_spec((1, LANES)), _const_spec((1, M_WIDTH))],
        out_specs=pl.BlockSpec((lc, M_WIDTH), lambda b, c: (row(b, c), 0)),
        out_shape=jax.ShapeDtypeStruct((t, M_WIDTH), BF16),
        scratch_shapes=[pltpu.VMEM((lc + 8, M_WIDTH), F32), pltpu.VMEM((lc + 8, M_WIDTH), F32),
                        pltpu.VMEM((M_HEADS, M_HEAD_DIM, M_HEAD_DIM), F32),
                        pltpu.VMEM((8, M_HEAD_DIM), F32), pltpu.VMEM((8, LANES), F32)],
        compiler_params=_cparams("parallel", "arbitrary"),
        name="mlstm",
    )(y, y, y, y, misc, cwq, cwk, cbq, cbk, gate_bias, hg)


def _mla_prep_kernel(cq_ref, ckv_ref, misc_ref, cos_ref, sin_ref, cqg_ref, wuq_ref, wuqs_ref, ckvg_ref, wuk_ref,
                     wuv_ref, swap_ref, qng_ref, qngs_ref, kng_ref, kngs_ref, q_ref, k_ref, v_ref):
    cqn = _rms(cq_ref[...].astype(F32), cqg_ref[...]).astype(BF16)
    qa = _dot(cqn, wuq_ref[...])
    qs = _dot(cqn, wuqs_ref[...])
    ckvn = _rms(ckv_ref[...].astype(F32), ckvg_ref[...]).astype(BF16)
    ka = _dot(ckvn, wuk_ref[...])
    va = _dot(ckvn, wuv_ref[...])
    lane = lax.broadcasted_iota(jnp.int32, misc_ref.shape, 1)
    kr = jnp.where((lane >= KR_LANE) & (lane < KR_LANE + QK_ROPE), misc_ref[...], 0.0)
    kr_hi = kr.astype(BF16)
    kr_lo = (kr - kr_hi.astype(F32)).astype(BF16)
    kr_swap = _dot(kr_hi, swap_ref[...]) + _dot(kr_lo, swap_ref[...])
    cos = cos_ref[...]
    sin = sin_ref[...]
    q_cos = qng_ref[...] * cos
    q_sin = qngs_ref[...] * sin
    k_cos = kng_ref[...] * cos
    k_rot = kr_swap * (kngs_ref[...] * sin)

    def inv_rms(xh):
        return lax.rsqrt(jnp.sum(xh * xh, axis=-1, keepdims=True) * (1.0 / QK_HEAD) + NORM_EPS)

    for h in range(A_HEADS):
        sl = slice(h * HEAD_PAD, (h + 1) * HEAD_PAD)
        qh = qa[:, sl]
        kh = ka[:, sl] + kr
        q_ref[0, h, 0] = ((qh * q_cos + qs[:, sl] * q_sin)
                          * (inv_rms(qh) * (LOG2_E * QK_HEAD ** -0.5))).T.astype(BF16)
        k_ref[0, h] = ((kh * k_cos + k_rot) * inv_rms(kh)).astype(BF16)
        v_ref[0, h, 0] = jnp.where(lane < V_HEAD, va[:, sl], 1.0).T.astype(BF16)


def _mla_prep(y, misc, cos_t, sin_s, cqg, wuq, wuq_swap, ckvg, wuk, wuv, swap, qng, qng_swap, kng, kng_swap,
              bsz, seq, ts):
    t = bsz * seq
    nt = seq // ts
    row = lambda b, i: (b * nt + i, 0)
    cq_blk = (4 * M_WIDTH) // Q_LORA
    ckv_blk = (4 * M_WIDTH + Q_LORA) // KV_LORA
    hw = A_HEADS * HEAD_PAD
    return pl.pallas_call(
        _mla_prep_kernel,
        grid=(bsz, nt),
        in_specs=[pl.BlockSpec((ts, Q_LORA), lambda b, i: (b * nt + i, cq_blk)),
                  pl.BlockSpec((ts, KV_LORA), lambda b, i: (b * nt + i, ckv_blk)),
                  pl.BlockSpec((ts, LANES), row), pl.BlockSpec((ts, LANES), row), pl.BlockSpec((ts, LANES), row),
                  _const_spec((1, Q_LORA)), _const_spec((Q_LORA, hw)), _const_spec((Q_LORA, hw)),
                  _const_spec((1, KV_LORA)), _const_spec((KV_LORA, hw)), _const_spec((KV_LORA, hw)),
                  _const_spec((HEAD_PAD, HEAD_PAD)),
                  _const_spec((1, HEAD_PAD)), _const_spec((1, HEAD_PAD)),
                  _const_spec((1, HEAD_PAD)), _const_spec((1, HEAD_PAD))],
        out_specs=[pl.BlockSpec((1, A_HEADS, 1, HEAD_PAD, ts), lambda b, i: (b, 0, i, 0, 0)),
                   pl.BlockSpec((1, A_HEADS, ts, HEAD_PAD), lambda b, i: (b, 0, i, 0)),
                   pl.BlockSpec((1, A_HEADS, 1, HEAD_PAD, ts), lambda b, i: (b, 0, i, 0, 0))],
        out_shape=[jax.ShapeDtypeStruct((bsz, A_HEADS, nt, HEAD_PAD, ts), BF16),
                   jax.ShapeDtypeStruct((bsz, A_HEADS, seq, HEAD_PAD), BF16),
                   jax.ShapeDtypeStruct((bsz, A_HEADS, nt, HEAD_PAD, ts), BF16)],
        compiler_params=_cparams("parallel", "parallel"),
        name="mla_prep",
    )(y, y, misc, cos_t, sin_s, cqg, wuq, wuq_swap, ckvg, wuk, wuv, swap, qng, qng_swap, kng, kng_swap)


def _attn_kernel(q_ref, k_ref, v_ref, o_ref, *scratch, tq):
    m_refs, acc_refs = scratch[:A_HEADS], scratch[A_HEADS:]
    i = pl.program_id(1)
    key = lax.broadcasted_iota(jnp.int32, (tq, tq), 0)
    qry = lax.broadcasted_iota(jnp.int32, (tq, tq), 1)
    dmask = (key // MLA_CHUNK) <= (qry // MLA_CHUNK)
    for h in range(A_HEADS):
        m_refs[h][...] = jnp.full((1, tq), NEG_BIG, F32)
        acc_refs[h][...] = jnp.zeros((HEAD_PAD, tq), F32)

    def update(kt, masked):
        off = pl.multiple_of(kt * tq, tq)
        scores = [_dot(k_ref[0, h, pl.ds(off, tq), :], q_ref[0, h, 0]) for h in range(A_HEADS)]
        for h in range(A_HEADS):
            st = scores[h]
            if masked:
                st = jnp.where(dmask, st, NEG_BIG)
            m = m_refs[h][...]
            mn = jnp.maximum(m, jnp.max(st, axis=0, keepdims=True))
            p = jnp.exp2(st - mn)
            acc_refs[h][...] = jnp.exp2(m - mn) * acc_refs[h][...] + _dot(v_ref[0, h, kt], p.astype(BF16))
            m_refs[h][...] = mn

    def body(kt, carry):
        update(kt, False)
        return carry

    lax.fori_loop(0, i, body, 0)
    update(i, True)
    for hp in range(A_HEADS // 2):
        a0 = acc_refs[2 * hp][...]
        a1 = acc_refs[2 * hp + 1][...]
        pair = jnp.concatenate([a0[:V_HEAD] / a0[V_HEAD:V_HEAD + 1], a1[:V_HEAD] / a1[V_HEAD:V_HEAD + 1]], axis=0)
        o_ref[:, hp * LANES:(hp + 1) * LANES] = pair.T.astype(BF16)


def _attention(q_t, k, v_t, bsz, seq, tq):
    nq = seq // tq
    return pl.pallas_call(
        functools.partial(_attn_kernel, tq=tq),
        grid=(bsz, nq),
        in_specs=[pl.BlockSpec((1, A_HEADS, 1, HEAD_PAD, tq), lambda b, i: (b, 0, i, 0, 0)),
                  pl.BlockSpec((1, A_HEADS, seq, HEAD_PAD), lambda b, i: (b, 0, 0, 0)),
                  pl.BlockSpec((1, A_HEADS, nq, HEAD_PAD, tq), lambda b, i: (b, 0, 0, 0, 0))],
        out_specs=pl.BlockSpec((tq, A_WIDTH), lambda b, i: (b * nq + i, 0)),
        out_shape=jax.ShapeDtypeStruct((bsz * seq, A_WIDTH), BF16),
        scratch_shapes=([pltpu.VMEM((1, tq), F32)] * A_HEADS + [pltpu.VMEM((HEAD_PAD, tq), F32)] * A_HEADS),
        compiler_params=_cparams("parallel", "arbitrary"),
        name="mla_attention",
    )(q_t, k, v_t)


def _outproj_ffn_kernel(m_ref, a_ref, x_ref, wom_ref, woa_ref, g_ref, w1_ref, w3_ref, w2_ref, o_ref):
    x1 = x_ref[...] + _dot(m_ref[...], wom_ref[...]) + _dot(a_ref[...], woa_ref[...])
    hn = _rms(x1, g_ref[...]).astype(BF16)
    a = _dot(hn, w1_ref[...])
    b = _dot(hn, w3_ref[...])
    o_ref[...] = x1 + _dot((a * _sigmoid(a) * b).astype(BF16), w2_ref[...])


def _outproj_ffn(m_out, a_out, x, wom, woa, g, w1, w3, w2, tm):
    t = x.shape[0]
    return pl.pallas_call(
        _outproj_ffn_kernel,
        grid=(t // tm,),
        in_specs=[pl.BlockSpec((tm, M_WIDTH), lambda i: (i, 0)),
                  pl.BlockSpec((tm, A_WIDTH), lambda i: (i, 0)),
                  pl.BlockSpec((tm, D_MODEL), lambda i: (i, 0)),
                  _const_spec((M_WIDTH, D_MODEL)), _const_spec((A_WIDTH, D_MODEL)), _const_spec((1, D_MODEL)),
                  _const_spec((D_MODEL, D_FF)), _const_spec((D_MODEL, D_FF)), _const_spec((D_FF, D_MODEL))],
        out_specs=pl.BlockSpec((tm, D_MODEL), lambda i: (i, 0)),
        out_shape=jax.ShapeDtypeStruct((t, D_MODEL), F32),
        compiler_params=_cparams("parallel"),
        name="outproj_ffn",
    )(m_out, a_out, x, wom, woa, g, w1, w3, w2)


S5_CHUNK = 16
S5_ROW = S5_CHUNK * S5_GROUP


GROUPS_PER_TILE = LANES // S5_GROUP
RELAYOUT_ROWS = 32


def _lane_group_masks(rows):
    grp = lax.broadcasted_iota(jnp.int32, (rows, LANES), 1) // S5_GROUP
    return [grp == g for g in range(GROUPS_PER_TILE)]


def _to_group_major(vs, masks):
    n = GROUPS_PER_TILE
    rolled = []
    for r in range(n):
        d = vs[r % n]
        for g in range(1, n):
            d = jnp.where(masks[g], vs[(g + r) % n], d)
        rolled.append(d if r == 0 else pltpu.roll(d, r * S5_GROUP, axis=1))
    outs = []
    for grp in range(n):
        o = rolled[(-grp) % n]
        for w in range(1, n):
            o = jnp.where(masks[w], rolled[(w - grp) % n], o)
        outs.append(o)
    return outs


def _to_token_major(outs, masks):
    n = GROUPS_PER_TILE
    rolled = []
    for r in range(n):
        d = outs[(-r) % n]
        for w in range(1, n):
            d = jnp.where(masks[w], outs[(w - r) % n], d)
        rolled.append(d if r == 0 else pltpu.roll(d, LANES - r * S5_GROUP, axis=1))
    vs = []
    for w in range(n):
        v = rolled[w]
        for g in range(1, n):
            v = jnp.where(masks[g], rolled[(w - g) % n], v)
        vs.append(v)
    return vs


def _relayout_steps(tile_rows):
    nrow = tile_rows // S5_CHUNK
    return [(j, k, rg) for j in range(D_MODEL // LANES) for k in range(S5_CHUNK // GROUPS_PER_TILE)
            for rg in range(nrow // RELAYOUT_ROWS)]


def _norm_group_major_kernel(x_ref, g_ref, u_ref, un_scr):
    un = _rms(x_ref[...], g_ref[...])
    for j in range(D_MODEL // LANES):
        un_scr[j] = un[:, j * LANES:(j + 1) * LANES]
    masks = _lane_group_masks(RELAYOUT_ROWS)
    n = GROUPS_PER_TILE
    for j, k, rg in _relayout_steps(x_ref.shape[0]):
        rows = slice(rg * RELAYOUT_ROWS, (rg + 1) * RELAYOUT_ROWS)
        first = rg * RELAYOUT_ROWS * S5_CHUNK + k * n
        vs = [un_scr[j, pl.ds(first + w, RELAYOUT_ROWS, stride=S5_CHUNK), :] for w in range(n)]
        outs = _to_group_major(vs, masks)
        for grp in range(n):
            u_ref[j * n + grp, rows, k * LANES:(k + 1) * LANES] = outs[grp].astype(BF16)


def _norm_group_major(x, g, tm):
    t = x.shape[0]
    nrow = tm // S5_CHUNK
    return pl.pallas_call(
        _norm_group_major_kernel,
        grid=(t // tm,),
        in_specs=[pl.BlockSpec((tm, D_MODEL), lambda i: (i, 0)), _const_spec((1, D_MODEL))],
        out_specs=pl.BlockSpec((S5_GROUPS, nrow, S5_ROW), lambda i: (0, i, 0)),
        out_shape=jax.ShapeDtypeStruct((S5_GROUPS, t // S5_CHUNK, S5_ROW), BF16),
        scratch_shapes=[pltpu.VMEM((D_MODEL // LANES, tm, LANES), F32)],
        compiler_params=_cparams("parallel"),
        name="norm_group_major",
    )(x, g)


def _gelu_tanh(x):
    return 0.5 * x * (1.0 + jnp.tanh(math.sqrt(2.0 / math.pi) * (x + 0.044715 * (x * x * x))))


S5_GROUPS_PER_STEP = 4


def _s5_kernel(u_ref, w_ref, cst_ref, a1_ref, a2_ref, dk_ref, z_ref, vv_scr, xp_scr, *, nch, bsz):
    two_p = 2 * S5_STATE
    ng = S5_GROUPS_PER_STEP
    intra = []
    for i in range(ng):
        r = _dot(u_ref[i], w_ref[i])
        intra.append(r[:, :S5_ROW])
        vv_scr[2 * i] = r[:, S5_ROW:S5_ROW + two_p]
        vv_scr[2 * i + 1] = r[:, S5_ROW + two_p:]
    a1 = [a1_ref[i] for i in range(ng)]
    a2 = [a2_ref[i] for i in range(ng)]

    def body(c, carry):
        rows = pl.ds(c, bsz, stride=nch)
        out = []
        for i in range(ng):
            p, q = carry[2 * i], carry[2 * i + 1]
            xp_scr[i, rows, :] = p
            out.append(a1[i] * p + a2[i] * q + vv_scr[2 * i, rows, :])
            out.append(a1[i] * q - a2[i] * p + vv_scr[2 * i + 1, rows, :])
        return tuple(out)

    zero = jnp.zeros((bsz, two_p), F32)
    lax.fori_loop(0, nch, body, (zero,) * (2 * ng), unroll=2)
    for i in range(ng):
        y = intra[i] + _dot(xp_scr[i].astype(BF16), cst_ref[i]) + dk_ref[i] * u_ref[i].astype(F32)
        z_ref[i] = _gelu_tanh(y).astype(BF16)


def _s5(u_g, w_all, cst, a1, a2, dk, nch, bsz):
    g, n, _ = u_g.shape
    two_p = 2 * S5_STATE
    wcols = S5_ROW + 2 * two_p
    ng = S5_GROUPS_PER_STEP
    blk = lambda *shape: pl.BlockSpec((ng,) + shape, lambda i: (i, 0, 0))
    return pl.pallas_call(
        functools.partial(_s5_kernel, nch=nch, bsz=bsz),
        grid=(g // ng,),
        in_specs=[blk(n, S5_ROW), blk(S5_ROW, wcols), blk(two_p, S5_ROW), blk(1, two_p), blk(1, two_p),
                  blk(1, S5_ROW)],
        out_specs=blk(n, S5_ROW),
        out_shape=jax.ShapeDtypeStruct((g, n, S5_ROW), BF16),
        scratch_shapes=[pltpu.VMEM((2 * ng, n, two_p), F32), pltpu.VMEM((ng, n, two_p), F32)],
        compiler_params=_cparams("parallel"),
        name="s5",
    )(u_g, w_all, cst, a1, a2, dk)


def _s5_weights(lam_re, lam_im, log_dt, b_re, b_im, c_re, c_im, d_skip):
    lc, h, p = S5_CHUNK, S5_GROUP, S5_STATE
    lam = lax.complex(jnp.minimum(lam_re.astype(F32), -1e-4), lam_im.astype(F32))
    dt = jnp.exp(log_dt.astype(F32))[:, None]
    ldt = lam * dt
    a_bar = jnp.exp(ldt)
    b_bar = ((a_bar - 1.0) / lam)[..., None] * lax.complex(b_re.astype(F32), b_im.astype(F32))
    c_mat = lax.complex(c_re.astype(F32), c_im.astype(F32))
    steps = jnp.arange(lc + 1, dtype=F32)
    apow = jnp.exp(ldt[None] * steps[:, None, None].astype(ldt.dtype))
    g = lam.shape[0]
    ker = jnp.einsum('gip,kgp,gpj->gkij', c_mat, apow[:lc], b_bar).real
    lag = np.arange(lc)[None, :] - np.arange(lc)[:, None]
    toe = jnp.where((lag >= 0)[None, :, :, None, None], ker[:, np.clip(lag, 0, lc - 1)], 0.0)
    toe = toe.transpose(0, 1, 4, 2, 3).reshape(g, lc * h, lc * h)
    wst = apow[:lc][::-1].transpose(1, 0, 2)[:, :, None, :] * jnp.swapaxes(b_bar, 1, 2)[:, None, :, :]
    wst = wst.reshape(g, lc * h, p)
    w_all = jnp.concatenate([toe, wst.real, wst.imag, wst.imag, wst.real], axis=-1).astype(BF16)
    ca = c_mat[:, None, :, :] * apow[1:lc + 1].transpose(1, 0, 2)[:, :, None, :]
    ca = ca.reshape(g, lc * h, p)
    cst = jnp.concatenate([ca.real, -ca.imag], axis=-1).transpose(0, 2, 1).astype(BF16)
    al = apow[lc]
    a1 = jnp.concatenate([al.real, al.real], axis=-1)[:, None, :]
    a2 = jnp.concatenate([-al.imag, al.imag], axis=-1)[:, None, :]
    dk = jnp.tile(d_skip.astype(F32).reshape(g, 1, h), (1, lc, 1)).reshape(g, 1, lc * h)
    return w_all, cst, a1, a2, dk


def _glu_kernel(z_ref, x_ref, wv_ref, wg_ref, g_ref, x2_ref, xn_ref, zt_scr):
    masks = _lane_group_masks(RELAYOUT_ROWS)
    n = GROUPS_PER_TILE
    for j, k, rg in _relayout_steps(x_ref.shape[0]):
        rows = slice(rg * RELAYOUT_ROWS, (rg + 1) * RELAYOUT_ROWS)
        outs = [z_ref[j * n + grp, rows, k * LANES:(k + 1) * LANES].astype(F32) for grp in range(n)]
        vs = _to_token_major(outs, masks)
        first = rg * RELAYOUT_ROWS * S5_CHUNK + k * n
        for w in range(n):
            zt_scr[j, pl.ds(first + w, RELAYOUT_ROWS, stride=S5_CHUNK), :] = vs[w]
    z = jnp.concatenate([zt_scr[j] for j in range(D_MODEL // LANES)], axis=1).astype(BF16)
    x2 = x_ref[...] + _dot(z, wv_ref[...]) * _sigmoid(_dot(z, wg_ref[...]))
    x2_ref[...] = x2
    xn_ref[...] = _rms(x2, g_ref[...]).astype(BF16)


def _glu(z_g, x, wv, wg, g, tm):
    t = x.shape[0]
    nrow = tm // S5_CHUNK
    return pl.pallas_call(
        _glu_kernel,
        grid=(t // tm,),
        in_specs=[pl.BlockSpec((S5_GROUPS, nrow, S5_ROW), lambda i: (0, i, 0)),
                  pl.BlockSpec((tm, D_MODEL), lambda i: (i, 0)),
                  _const_spec((D_MODEL, D_MODEL)), _const_spec((D_MODEL, D_MODEL)), _const_spec((1, D_MODEL))],
        out_specs=[pl.BlockSpec((tm, D_MODEL), lambda i: (i, 0)), pl.BlockSpec((tm, D_MODEL), lambda i: (i, 0))],
        out_shape=[jax.ShapeDtypeStruct((t, D_MODEL), F32), jax.ShapeDtypeStruct((t, D_MODEL), BF16)],
        scratch_shapes=[pltpu.VMEM((D_MODEL // LANES, tm, LANES), F32)],
        compiler_params=_cparams("parallel"),
        name="glu",
    )(z_g, x, wv, wg, g)


def _router_kernel(xn_ref, wh_ref, wl_ref, pos_ref, gate_ref, posc_ref, cnt_ref, *, tb):
    xn = xn_ref[...]
    lg = (_dot_nt(wh_ref[...], xn) + _dot_nt(wl_ref[...], xn))[0:N_EXPERTS]
    eidx = lax.broadcasted_iota(jnp.int32, (N_EXPERTS, tb), 0)
    m1 = jnp.max(lg, axis=0, keepdims=True)
    i1 = jnp.min(jnp.where(lg == m1, eidx, N_EXPERTS), axis=0, keepdims=True)
    rest = jnp.where(eidx == i1, -jnp.inf, lg)
    m2 = jnp.max(rest, axis=0, keepdims=True)
    i2 = jnp.min(jnp.where(rest == m2, eidx, N_EXPERTS), axis=0, keepdims=True)
    e21 = jnp.exp(m2 - m1)
    p1 = 1.0 / (1.0 + e21)
    sel1 = eidx == i1
    sel2 = eidx == i2
    sel = sel1 | sel2
    gate_ref[...] = jnp.where(sel1, p1, jnp.where(sel2, e21 * p1, 0.0))
    onehot = jnp.concatenate([sel.astype(F32), jnp.zeros((N_EXPERTS, tb), F32)], axis=0).astype(BF16)
    srow = lax.broadcasted_iota(jnp.int32, (tb, tb), 0)
    scol = lax.broadcasted_iota(jnp.int32, (tb, tb), 1)
    rank = _dot(onehot, (srow < scol).astype(BF16))[0:N_EXPERTS]
    pos = jnp.where(sel, rank, -1.0)
    pos_ref[...] = pos
    posc_ref[...] = jnp.concatenate([pos, jnp.full((LANES - N_EXPERTS, tb), -1.0, F32)], axis=0).T
    cnt = jnp.sum(sel.astype(F32), axis=1, keepdims=True)
    cnt_ref[0] = jnp.broadcast_to(cnt, (N_EXPERTS, LANES))


def _router(xn, wr_hi, wr_lo, tb):
    t = xn.shape[0]
    nb = t // tb
    return pl.pallas_call(
        functools.partial(_router_kernel, tb=tb),
        grid=(nb,),
        in_specs=[pl.BlockSpec((tb, D_MODEL), lambda i: (i, 0)),
                  _const_spec((2 * N_EXPERTS, D_MODEL)), _const_spec((2 * N_EXPERTS, D_MODEL))],
        out_specs=[pl.BlockSpec((N_EXPERTS, tb), lambda i: (0, i)), pl.BlockSpec((N_EXPERTS, tb), lambda i: (0, i)),
                   pl.BlockSpec((tb, LANES), lambda i: (i, 0)),
                   pl.BlockSpec((1, N_EXPERTS, LANES), lambda i: (i, 0, 0))],
        out_shape=[jax.ShapeDtypeStruct((N_EXPERTS, t), F32), jax.ShapeDtypeStruct((N_EXPERTS, t), F32),
                   jax.ShapeDtypeStruct((t, LANES), F32), jax.ShapeDtypeStruct((nb, N_EXPERTS, LANES), F32)],
        compiler_params=_cparams("parallel"),
        name="router",
    )(xn, wr_hi, wr_lo)


MOE_UNIT = 128
MOE_ALIGN = 16
MOE_TM = 512


def _moe_plan(cnt, n_tokens):
    nb = cnt.shape[0]
    seg = (cnt + MOE_ALIGN - 1) // MOE_ALIGN * MOE_ALIGN
    used = jnp.sum(seg, axis=0)
    gsize = (used + MOE_UNIT + MOE_TM - 1) // MOE_TM * MOE_TM
    gstart = jnp.cumsum(gsize) - gsize
    dest = gstart[None, :] + jnp.cumsum(seg, axis=0) - seg
    max_rows = 2 * n_tokens + N_EXPERTS * (nb * (MOE_ALIGN - 1) + MOE_UNIT + MOE_TM - 1)
    n_tiles = -(-max_rows // MOE_TM)
    tile_start = jnp.arange(n_tiles, dtype=jnp.int32) * MOE_TM
    tile_e = jnp.minimum(jnp.sum(tile_start[:, None] >= (gstart + gsize)[None, :], axis=1), N_EXPERTS - 1)
    tile_valid = tile_start < (gstart + used)[tile_e]
    return dest.reshape(-1).astype(jnp.int32), tile_e.astype(jnp.int32), tile_valid.astype(jnp.int32), n_tiles


def _moe_pack_kernel(dest_ref, cnt_ref, xn_ref, pos_ref, xs_in_ref, xs_ref, stage, sem, *, tb):
    del xs_in_ref
    b = pl.program_id(0)
    unit = MOE_UNIT

    def n_units(e):
        return (cnt_ref[b * N_EXPERTS + e] + unit - 1) // unit

    def chunk_copy(e, r):
        dst = pl.multiple_of(dest_ref[b * N_EXPERTS + e] + r * unit, MOE_ALIGN)
        return pltpu.make_async_copy(stage.at[e % 2, pl.ds(pl.multiple_of(r * unit, unit), unit)],
                                     xs_ref.at[pl.ds(dst, unit)], sem.at[e % 2])

    def wait_expert(e):
        def body(r, carry):
            chunk_copy(e, r).wait()
            return carry
        lax.fori_loop(0, n_units(e), body, 0)

    for e in range(N_EXPERTS):
        if e >= 2:
            wait_expert(e - 2)
        prow = pos_ref[e:e + 1, :]
        n_unit = n_units(e)
        n_pair = n_unit // 2

        def gather(base, m, e=e, prow=prow):
            slot = lax.broadcasted_iota(jnp.int32, (m, tb), 0).astype(F32) + base.astype(F32)
            stage[e % 2, pl.ds(base, m), :] = _dot((prow == slot).astype(BF16), xn_ref[...]).astype(BF16)

        def pair(r, carry, e=e, gather=gather):
            gather(pl.multiple_of(r * (2 * unit), 2 * unit), 2 * unit)
            chunk_copy(e, 2 * r).start()
            chunk_copy(e, 2 * r + 1).start()
            return carry

        lax.fori_loop(0, n_pair, pair, 0)

        @pl.when(n_unit % 2 == 1)
        def _(e=e, gather=gather, n_pair=n_pair):
            gather(pl.multiple_of(n_pair * (2 * unit), 2 * unit), unit)
            chunk_copy(e, 2 * n_pair).start()
    wait_expert(N_EXPERTS - 2)
    wait_expert(N_EXPERTS - 1)


def _moe_pack(dest, cnt, xn, pos, xs_init, tb):
    t = xn.shape[0]
    grid_spec = pltpu.PrefetchScalarGridSpec(
        num_scalar_prefetch=2,
        grid=(t // tb,),
        in_specs=[pl.BlockSpec((tb, D_MODEL), lambda b, d, c: (b, 0)),
                  pl.BlockSpec((N_EXPERTS, tb), lambda b, d, c: (0, b)),
                  pl.BlockSpec(memory_space=pl.ANY)],
        out_specs=pl.BlockSpec(memory_space=pl.ANY),
        scratch_shapes=[pltpu.VMEM((2, tb, D_MODEL), BF16), pltpu.SemaphoreType.DMA((2,))],
    )
    return pl.pallas_call(
        functools.partial(_moe_pack_kernel, tb=tb),
        grid_spec=grid_spec,
        out_shape=jax.ShapeDtypeStruct(xs_init.shape, BF16),
        input_output_aliases={4: 0},
        compiler_params=_cparams("arbitrary"),
        name="moe_pack",
    )(dest, cnt, xn, pos, xs_init)


def _moe_ffn_kernel(te_ref, tv_ref, xs_ref, w1_ref, w3_ref, w2_ref, y_ref):
    del te_ref
    valid = tv_ref[pl.program_id(0)] != 0

    @pl.when(valid)
    def _():
        xe = xs_ref[...]
        a = _dot(xe, w1_ref[0])
        g = (a * _sigmoid(a) * _dot(xe, w3_ref[0])).astype(BF16)
        y_ref[...] = _dot(g, w2_ref[0]).astype(BF16)

    @pl.when(jnp.logical_not(valid))
    def _():
        y_ref[...] = jnp.zeros(y_ref.shape, BF16)


def _moe_ffn(tile_e, tile_valid, xs, w1, w3, w2, n_tiles):
    w_up = pl.BlockSpec((1, D_MODEL, D_FF_EXPERT), lambda i, te, tv: (te[i], 0, 0), pipeline_mode=pl.Buffered(1))
    w_down = pl.BlockSpec((1, D_FF_EXPERT, D_MODEL), lambda i, te, tv: (te[i], 0, 0), pipeline_mode=pl.Buffered(1))
    grid_spec = pltpu.PrefetchScalarGridSpec(
        num_scalar_prefetch=2,
        grid=(n_tiles,),
        in_specs=[pl.BlockSpec((MOE_TM, D_MODEL), lambda i, te, tv: (i, 0)), w_up, w_up, w_down],
        out_specs=pl.BlockSpec((MOE_TM, D_MODEL), lambda i, te, tv: (i, 0)),
    )
    return pl.pallas_call(
        _moe_ffn_kernel,
        grid_spec=grid_spec,
        out_shape=jax.ShapeDtypeStruct((n_tiles * MOE_TM, D_MODEL), BF16),
        compiler_params=_cparams("parallel"),
        name="moe_ffn",
    )(tile_e, tile_valid, xs, w1, w3, w2)


def _moe_combine_kernel(dest_ref, cnt_ref, x_ref, pos_ref, gate_ref, posc_ref, ys_ref, o_ref, ybuf, sem, *, tb):
    b = pl.program_id(0)
    unit = MOE_UNIT

    def n_units(e):
        return (cnt_ref[b * N_EXPERTS + e] + unit - 1) // unit

    def chunk_copy(e, r):
        src = pl.multiple_of(dest_ref[b * N_EXPERTS + e] + r * unit, MOE_ALIGN)
        return pltpu.make_async_copy(ys_ref.at[pl.ds(src, unit)],
                                     ybuf.at[e % 2, pl.ds(pl.multiple_of(r * unit, unit), unit)], sem.at[e % 2])

    def start_expert(e):
        def body(r, carry):
            chunk_copy(e, r).start()
            return carry
        lax.fori_loop(0, n_units(e), body, 0)

    def wait_expert(e):
        def body(r, carry):
            chunk_copy(e, r).wait()
            return carry
        lax.fori_loop(0, n_units(e), body, 0)

    start_expert(0)
    o_ref[...] = x_ref[...]
    lane = lax.broadcasted_iota(jnp.int32, (tb, LANES), 1)
    for e in range(N_EXPERTS):
        if e + 1 < N_EXPERTS:
            start_expert(e + 1)
        wait_expert(e)
        prow = pos_ref[e:e + 1, :]
        grow = gate_ref[e:e + 1, :]
        n_unit = n_units(e)
        n_pair = n_unit // 2

        def scatter_chunk(base, m, e=e, prow=prow, grow=grow):
            hit = prow == lax.broadcasted_iota(jnp.int32, (m, tb), 0).astype(F32) + base.astype(F32)
            gs = jnp.sum(jnp.where(hit, grow, 0.0), axis=-1, keepdims=True)
            yg = (ybuf[e % 2, pl.ds(base, m), :].astype(F32) * gs).astype(BF16)
            o_ref[...] += _dot_tn(hit.astype(BF16), yg)

        def scatter_pair(r, carry, scatter_chunk=scatter_chunk):
            scatter_chunk(pl.multiple_of(r * (2 * unit), 2 * unit), 2 * unit)
            return carry

        lax.fori_loop(0, n_pair, scatter_pair, 0)

        @pl.when(n_unit % 2 == 1)
        def _(scatter_chunk=scatter_chunk, n_pair=n_pair):
            scatter_chunk(pl.multiple_of(n_pair * (2 * unit), 2 * unit), unit)


def _moe_combine(dest, cnt, x, pos, gate, posc, ys, tb):
    t = x.shape[0]
    grid_spec = pltpu.PrefetchScalarGridSpec(
        num_scalar_prefetch=2,
        grid=(t // tb,),
        in_specs=[pl.BlockSpec((tb, D_MODEL), lambda b, d, c: (b, 0)),
                  pl.BlockSpec((N_EXPERTS, tb), lambda b, d, c: (0, b)),
                  pl.BlockSpec((N_EXPERTS, tb), lambda b, d, c: (0, b)),
                  pl.BlockSpec((tb, LANES), lambda b, d, c: (b, 0)),
                  pl.BlockSpec(memory_space=pl.ANY)],
        out_specs=pl.BlockSpec((tb, D_MODEL), lambda b, d, c: (b, 0)),
        scratch_shapes=[pltpu.VMEM((2, tb, D_MODEL), BF16), pltpu.SemaphoreType.DMA((2,))],
    )
    return pl.pallas_call(
        functools.partial(_moe_combine_kernel, tb=tb),
        grid_spec=grid_spec,
        out_shape=jax.ShapeDtypeStruct((t, D_MODEL), F32),
        compiler_params=_cparams("parallel"),
        name="moe_combine",
    )(dest, cnt, x, pos, gate, posc, ys)


def _moe(cnt, xn, x, pos, gate, posc, w1, w3, w2, tb, xs_init):
    t = x.shape[0]
    dest, tile_e, tile_valid, n_tiles = _moe_plan(cnt, t)
    cnt = cnt.reshape(-1)
    if xs_init is None:
        xs_init = jnp.zeros((n_tiles * MOE_TM, D_MODEL), BF16)
    xs = _moe_pack(dest, cnt, xn, pos, xs_init, tb)
    ys = _moe_ffn(tile_e, tile_valid, xs, w1, w3, w2, n_tiles)
    return _moe_combine(dest, cnt, x, pos, gate, posc, ys, tb), xs


def _rope_tables(positions):
    half = QK_ROPE // 2
    inv_freq = ROPE_THETA ** (-jnp.arange(0, QK_ROPE, 2, dtype=F32) / QK_ROPE)
    ang = positions.astype(F32).reshape(-1, 1) * inv_freq
    cos, sin = jnp.cos(ang), jnp.sin(ang)
    t = ang.shape[0]
    ones_lo = jnp.ones((t, QK_NOPE), F32)
    zeros_lo = jnp.zeros((t, QK_NOPE), F32)
    tail1 = jnp.ones((t, HEAD_PAD - QK_HEAD), F32)
    tail0 = jnp.zeros((t, HEAD_PAD - QK_HEAD), F32)
    cos_t = jnp.concatenate([ones_lo, cos, cos, tail1], axis=-1)
    sin_s = jnp.concatenate([zeros_lo, -sin, sin, tail0], axis=-1)
    return cos_t, sin_s


def _pad_heads(w, per_head_in, keep, heads):
    k = w.shape[0]
    w = w.reshape(k, heads, per_head_in)[:, :, :keep]
    return jnp.pad(w, ((0, 0), (0, 0), (0, HEAD_PAD - keep))).reshape(k, heads * HEAD_PAD)


def _even_layer(x, rope, bsz, seq, norm1_g, w_in, conv_w, conv_b, b_igate, b_fgate, mh_norm_g, cq_norm_g, w_uq,
                ckv_norm_g, w_ukv, qn_g, kn_g, w_o, norm2_g, ffn_w1, ffn_w3, ffn_w2):
    row = lambda v: v.astype(F32).reshape(1, -1)
    mw = M_WIDTH
    o_ig, o_fg, o_og = 3 * mw, 3 * mw + M_HEADS, 3 * mw + 2 * M_HEADS
    o_cq = o_og + mw
    o_ckv, o_kr = o_cq + Q_LORA, o_cq + Q_LORA + KV_LORA
    misc_w = jnp.zeros((D_MODEL, LANES), F32)
    misc_w = misc_w.at[:, IG_LANE:IG_LANE + M_HEADS].set(w_in[:, o_ig:o_fg])
    misc_w = misc_w.at[:, FG_LANE:FG_LANE + M_HEADS].set(w_in[:, o_fg:o_og])
    misc_w = misc_w.at[:, KR_LANE:KR_LANE + QK_ROPE].set(w_in[:, o_kr:o_kr + QK_ROPE])
    w_packed = jnp.concatenate([w_in[:, :3 * mw], w_in[:, o_og:o_cq], w_in[:, o_cq:o_ckv], w_in[:, o_ckv:o_kr],
                                misc_w], axis=-1).astype(BF16)
    gate_bias = jnp.zeros((1, LANES), F32)
    gate_bias = gate_bias.at[0, IG_LANE:IG_LANE + M_HEADS].set(b_igate.astype(F32))
    gate_bias = gate_bias.at[0, FG_LANE:FG_LANE + M_HEADS].set(b_fgate.astype(F32))

    tm = min(512, x.shape[0])
    y, misc = _inproj(x, row(norm1_g), w_packed, tm)

    lc = min(256, seq)
    m_out = _mlstm(y, misc, conv_w[:, :mw].astype(F32), conv_w[:, mw:].astype(F32), row(conv_b[:mw]),
                   row(conv_b[mw:]), gate_bias, row(mh_norm_g), bsz, seq, lc)

    wuq = _pad_heads(w_uq, QK_HEAD, QK_HEAD, A_HEADS).astype(BF16)
    wuk = _pad_heads(w_ukv, QK_NOPE + V_HEAD, QK_NOPE, A_HEADS).astype(BF16)
    wuv = _pad_heads(w_ukv.reshape(KV_LORA, A_HEADS, QK_NOPE + V_HEAD)[:, :, QK_NOPE:].reshape(KV_LORA, A_WIDTH),
                     V_HEAD, V_HEAD, A_HEADS).astype(BF16)
    pad_g = lambda g: jnp.pad(g.astype(F32), (0, HEAD_PAD - QK_HEAD)).reshape(1, HEAD_PAD)
    half = QK_ROPE // 2

    def swap_halves(v):
        r1, r2 = v[..., QK_NOPE:QK_NOPE + half], v[..., QK_NOPE + half:QK_HEAD]
        return jnp.concatenate([jnp.zeros_like(v[..., :QK_NOPE]), r2, r1], axis=-1)

    wuq_swap = _pad_heads(swap_halves(w_uq.reshape(Q_LORA, A_HEADS, QK_HEAD)).reshape(Q_LORA, A_HEADS * QK_HEAD),
                          QK_HEAD, QK_HEAD, A_HEADS).astype(BF16)
    swap = swap_halves(jnp.eye(HEAD_PAD, dtype=F32)[:, :QK_HEAD])
    swap = jnp.pad(swap, ((0, 0), (0, HEAD_PAD - QK_HEAD))).astype(BF16)
    ts = min(256, seq)
    q, k, v = _mla_prep(y, misc, rope[0], rope[1], row(cq_norm_g), wuq, wuq_swap, row(ckv_norm_g), wuk, wuv, swap,
                        pad_g(qn_g), pad_g(swap_halves(qn_g)), pad_g(kn_g), pad_g(swap_halves(kn_g)), bsz, seq, ts)
    a_out = _attention(q, k, v, bsz, seq, ts)

    return _outproj_ffn(m_out, a_out, x, w_o[:mw].astype(BF16), w_o[mw:].astype(BF16), row(norm2_g),
                        ffn_w1.astype(BF16), ffn_w3.astype(BF16), ffn_w2.astype(BF16), tm)


def _odd_layer(x, bsz, seq, norm1_g, lam_re, lam_im, log_dt, b_re, b_im, c_re, c_im, d_skip, glu_wv, glu_wg,
               norm2_g, router_w, moe_w1, moe_w3, moe_w2, xs_init=None):
    row = lambda v: v.astype(F32).reshape(1, -1)
    t = x.shape[0]
    tm = min(512, t)
    nch = seq // S5_CHUNK
    tr = min(1024, seq)
    u_g = _norm_group_major(x, row(norm1_g), tr)
    w_all, cst, a1, a2, dk = _s5_weights(lam_re, lam_im, log_dt, b_re, b_im, c_re, c_im, d_skip)
    z_g = _s5(u_g, w_all, cst, a1, a2, dk, nch, bsz)
    x2, xn = _glu(z_g, x, glu_wv.astype(BF16), glu_wg.astype(BF16), row(norm2_g), tr)

    tb = min(1024, t)
    wr = jnp.pad(router_w.astype(F32).T, ((0, N_EXPERTS), (0, 0)))
    wr_hi = wr.astype(BF16)
    wr_lo = (wr - wr_hi.astype(F32)).astype(BF16)
    pos, gate, posc, cnt = _router(xn, wr_hi, wr_lo, tb)
    cnt = cnt[:, :, 0].astype(jnp.int32)
    return _moe(cnt, xn, x2, pos, gate, posc, moe_w1.astype(BF16), moe_w3.astype(BF16), moe_w2.astype(BF16), tb,
                xs_init)


def kernel(x, positions, l0_norm1_g, l0_w_in, l0_conv_w, l0_conv_b, l0_b_igate, l0_b_fgate, l0_mh_norm_g, l0_cq_norm_g, l0_w_uq, l0_ckv_norm_g, l0_w_ukv, l0_qn_g, l0_kn_g, l0_w_o, l0_norm2_g, l0_ffn_w1, l0_ffn_w3, l0_ffn_w2, l1_norm1_g, l1_lam_re, l1_lam_im, l1_log_dt, l1_b_re, l1_b_im, l1_c_re, l1_c_im, l1_d_skip, l1_glu_wv, l1_glu_wg, l1_norm2_g, l1_router_w, l1_moe_w1, l1_moe_w3, l1_moe_w2, l2_norm1_g, l2_w_in, l2_conv_w, l2_conv_b, l2_b_igate, l2_b_fgate, l2_mh_norm_g, l2_cq_norm_g, l2_w_uq, l2_ckv_norm_g, l2_w_ukv, l2_qn_g, l2_kn_g, l2_w_o, l2_norm2_g, l2_ffn_w1, l2_ffn_w3, l2_ffn_w2, l3_norm1_g, l3_lam_re, l3_lam_im, l3_log_dt, l3_b_re, l3_b_im, l3_c_re, l3_c_im, l3_d_skip, l3_glu_wv, l3_glu_wg, l3_norm2_g, l3_router_w, l3_moe_w1, l3_moe_w3, l3_moe_w2):
    bsz, seq, _ = x.shape
    rope = _rope_tables(positions)
    h = x.reshape(bsz * seq, D_MODEL)
    h = _even_layer(h, rope, bsz, seq, l0_norm1_g, l0_w_in, l0_conv_w, l0_conv_b, l0_b_igate, l0_b_fgate,
                    l0_mh_norm_g, l0_cq_norm_g, l0_w_uq, l0_ckv_norm_g, l0_w_ukv, l0_qn_g, l0_kn_g, l0_w_o,
                    l0_norm2_g, l0_ffn_w1, l0_ffn_w3, l0_ffn_w2)
    h, xs = _odd_layer(h, bsz, seq, l1_norm1_g, l1_lam_re, l1_lam_im, l1_log_dt, l1_b_re, l1_b_im, l1_c_re, l1_c_im,
                       l1_d_skip, l1_glu_wv, l1_glu_wg, l1_norm2_g, l1_router_w, l1_moe_w1, l1_moe_w3, l1_moe_w2)
    h = _even_layer(h, rope, bsz, seq, l2_norm1_g, l2_w_in, l2_conv_w, l2_conv_b, l2_b_igate, l2_b_fgate,
                    l2_mh_norm_g, l2_cq_norm_g, l2_w_uq, l2_ckv_norm_g, l2_w_ukv, l2_qn_g, l2_kn_g, l2_w_o,
                    l2_norm2_g, l2_ffn_w1, l2_ffn_w3, l2_ffn_w2)
    h, _ = _odd_layer(h, bsz, seq, l3_norm1_g, l3_lam_re, l3_lam_im, l3_log_dt, l3_b_re, l3_b_im, l3_c_re, l3_c_im,
                      l3_d_skip, l3_glu_wv, l3_glu_wg, l3_norm2_g, l3_router_w, l3_moe_w1, l3_moe_w3, l3_moe_w2, xs)
    return h.reshape(bsz, seq, D_MODEL)
```
